```python
import jax
import jax.numpy as jnp
from jax import lax
import numpy as np

D_MODEL = 2048
BATCH = 2
SEQ = 16384
DEPTH = 2

EPS = 1e-6
NEG_INF = -1e30
ROPE_THETA = 500000.0

ML_HEADS = 4
ML_DK = 64
ML_DV = 128
ML_CHUNK = 64
ML_W = ML_HEADS * ML_DV

MLA_HEADS = 6
MLA_Q_RANK = 384
MLA_KV_RANK = 128
MLA_NOPE = 128
MLA_ROPE = 64
MLA_V = 128
MLA_Q_BLOCK = 128
MLA_W = MLA_HEADS * MLA_V

DIL_HEADS = 6
DIL_DH = 128
DIL_ROT = DIL_DH // 4
DIL_PAIRS = ((128, 1), (512, 4), (2048, 16))
DIL_W = DIL_HEADS * DIL_DH

D_MIX = ML_W + MLA_W + DIL_W
D_FF = 4 * D_MODEL

IN_SPLITS = (ML_HEADS * ML_DK, ML_HEADS * ML_DK, ML_W, ML_W, 4 * ML_HEADS,
             MLA_Q_RANK, MLA_KV_RANK + MLA_ROPE, DIL_W, DIL_W, DIL_W)
D_IN = sum(IN_SPLITS)

kernel_name = 'hybrid_parallel_mixer_encoder'


def rms_norm(x, gain):
    x32 = x.astype(jnp.float32)
    y = x32 * lax.rsqrt(jnp.mean(x32 * x32, axis=-1, keepdims=True) + EPS)
    return (y * gain.astype(jnp.float32)).astype(x.dtype)


def rope_tables(seq, rot_dim):
    pos = jnp.arange(seq, dtype=jnp.float32)
    inv_freq = ROPE_THETA ** (-jnp.arange(0, rot_dim, 2, dtype=jnp.float32) / rot_dim)
    ang = pos[:, None] * inv_freq[None, :]
    return jnp.cos(ang), jnp.sin(ang)


def apply_rope(x, cos, sin):
    x1, x2 = jnp.split(x.astype(jnp.float32), 2, axis=-1)
    return jnp.concatenate([x1 * cos - x2 * sin, x2 * cos + x1 * sin], axis=-1).astype(x.dtype)


def mlstm_chunkwise(q, k, v, i_pre, logf):
    N, H, S, dk = q.shape
    dv = v.shape[-1]
    L = ML_CHUNK
    nc = S // L

    def to_chunks(a):
        return jnp.moveaxis(a.reshape(a.shape[:2] + (nc, L) + a.shape[3:]), 2, 0)

    xs = (to_chunks(q), to_chunks(k), to_chunks(v), to_chunks(i_pre), to_chunks(logf))
    tril = jnp.tril(jnp.ones((L, L), dtype=bool))

    def step(carry, inp):
        C, n, m = carry
        qb, kb, vb, ib, fb = inp
        b = jnp.cumsum(fb, axis=-1)
        d_intra = jnp.where(tril, b[..., :, None] - b[..., None, :] + ib[..., None, :], NEG_INF)
        d_inter = b + m[..., None]
        m_t = jnp.maximum(d_inter, jnp.max(d_intra, axis=-1))
        w_intra = jnp.exp(d_intra - m_t[..., None])
        w_inter = jnp.exp(d_inter - m_t)
        s = jnp.einsum('nhtd,nhsd->nhts', qb, kb) * w_intra
        num = jnp.einsum('nhts,nhsv->nhtv', s, vb) + w_inter[..., None] * jnp.einsum('nhvd,nhtd->nhtv', C, qb)
        den = jnp.sum(s, axis=-1) + w_inter * jnp.einsum('nhd,nhtd->nht', n, qb)
        h = num / jnp.maximum(jnp.abs(den), jnp.exp(-m_t))[..., None]
        b_last = b[..., -1]
        d_state = b_last[..., None] - b + ib
        m_new = jnp.maximum(b_last + m, jnp.max(d_state, axis=-1))
        w_s = jnp.exp(d_state - m_new[..., None])
        w_c = jnp.exp(b_last + m - m_new)
        C_new = w_c[..., None, None] * C + jnp.einsum('nhsv,nhsd->nhvd', vb * w_s[..., None], kb)
        n_new = w_c[..., None] * n + jnp.einsum('nhs,nhsd->nhd', w_s, kb)
        return (C_new, n_new, m_new), h

    init = (jnp.zeros((N, H, dv, dk), jnp.float32), jnp.zeros((N, H, dk), jnp.float32),
            jnp.zeros((N, H), jnp.float32))
    _, hs = lax.scan(step, init, xs)
    return jnp.moveaxis(hs, 0, 2).reshape(N, H, S, dv)


def mlstm_mixer(q, k, v, o_pre, gates, i_bias, f_bias, out_norm):
    B, S, _ = q.shape

    def heads(a, d):
        return a.reshape(B, S, ML_HEADS, d).transpose(0, 2, 1, 3).astype(jnp.float32)

    qh = heads(q, ML_DK) * (ML_DK ** -0.5)
    kh = heads(k, ML_DK)
    vh = heads(v, ML_DV)
    g = gates.astype(jnp.float32).reshape(B, S, 2, 2, ML_HEADS)
    i_pre = (g[:, :, :, 0, :] + i_bias.astype(jnp.float32)).transpose(2, 0, 3, 1)
    logf = jax.nn.log_sigmoid(g[:, :, :, 1, :] + f_bias.astype(jnp.float32)).transpose(2, 0, 3, 1)
    qd = jnp.stack([qh, jnp.flip(qh, 2)]).reshape(2 * B, ML_HEADS, S, ML_DK)
    kd = jnp.stack([kh, jnp.flip(kh, 2)]).reshape(2 * B, ML_HEADS, S, ML_DK)
    vd = jnp.stack([vh, jnp.flip(vh, 2)]).reshape(2 * B, ML_HEADS, S, ML_DV)
    i_d = jnp.stack([i_pre[0], jnp.flip(i_pre[1], -1)]).reshape(2 * B, ML_HEADS, S)
    f_d = jnp.stack([logf[0], jnp.flip(logf[1], -1)]).reshape(2 * B, ML_HEADS, S)
    hd = mlstm_chunkwise(qd, kd, vd, i_d, f_d).reshape(2, B, ML_HEADS, S, ML_DV)
    h = hd[0] + jnp.flip(hd[1], 2)
    h = rms_norm(h, out_norm[:, None, :])
    h = h.transpose(0, 2, 1, 3).reshape(B, S, ML_W).astype(q.dtype)
    return jax.nn.sigmoid(o_pre) * h


def mla_blocked_attention(q_nope, q_rope, k_nope, k_rope, v):
    B, H, S, dn = q_nope.shape
    dr = q_rope.shape[-1]
    nb = S // MLA_Q_BLOCK
    scale = (dn + dr) ** -0.5
    qn = q_nope.reshape(B, H, nb, MLA_Q_BLOCK, dn).transpose(2, 0, 1, 3, 4)
    qr = q_rope.reshape(B, H, nb, MLA_Q_BLOCK, dr).transpose(2, 0, 1, 3, 4)

    def block(args):
        qnb, qrb = args
        s = (jnp.einsum('bhqd,bhkd->bhqk', qnb, k_nope)
             + jnp.einsum('bhqd,bkd->bhqk', qrb, k_rope)).astype(jnp.float32) * scale
        p = jax.nn.softmax(s, axis=-1)
        return jnp.einsum('bhqk,bhkd->bhqd', p.astype(v.dtype), v)

    o = lax.map(block, (qn, qr))
    return o.transpose(1, 2, 0, 3, 4).reshape(B, H, S, v.shape[-1])


def mla_mixer(c_q, c_kv_rope, q_lat_norm, w_q_b, kv_lat_norm, w_kv_b, q_head_norm, k_head_norm, cos, sin):
    B, S, _ = c_q.shape
    c_q = rms_norm(c_q, q_lat_norm)
    c_kv, k_rope = jnp.split(c_kv_rope, [MLA_KV_RANK], axis=-1)
    c_kv = rms_norm(c_kv, kv_lat_norm)
    q = jnp.einsum('bsr,re->bse', c_q, w_q_b).reshape(B, S, MLA_HEADS, MLA_NOPE + MLA_ROPE).transpose(0, 2, 1, 3)
    kv = jnp.einsum('bsr,re->bse', c_kv, w_kv_b).reshape(B, S, MLA_HEADS, MLA_NOPE + MLA_V).transpose(0, 2, 1, 3)
    q_nope, q_rope = jnp.split(q, [MLA_NOPE], axis=-1)
    k_nope, v = jnp.split(kv, [MLA_NOPE], axis=-1)
    qg_n, qg_r = jnp.split(q_head_norm, [MLA_NOPE])
    kg_n, kg_r = jnp.split(k_head_norm, [MLA_NOPE])
    q_nope = rms_norm(q_nope, qg_n)
    k_nope = rms_norm(k_nope, kg_n)
    q_rope = apply_rope(rms_norm(q_rope, qg_r), cos, sin)
    k_rope = apply_rope(rms_norm(k_rope, kg_r), cos, sin)
    o = mla_blocked_attention(q_nope, q_rope, k_nope, k_rope, v)
    return o.transpose(0, 2, 1, 3).reshape(B, S, MLA_W)


def banded_attention(q, k, v, half):
    N, L, D = q.shape
    W = half
    nb = -(-L // W)
    Lp = nb * W
    qb = jnp.pad(q, ((0, 0), (0, Lp - L), (0, 0))).reshape(N, nb, W, D)
    kp = jnp.pad(k, ((0, 0), (W, Lp - L + W), (0, 0))).reshape(N, nb + 2, W, D)
    vp = jnp.pad(v, ((0, 0), (W, Lp - L + W), (0, 0))).reshape(N, nb + 2, W, D)
    kb = jnp.concatenate([kp[:, :-2], kp[:, 1:-1], kp[:, 2:]], axis=2)
    vb = jnp.concatenate([vp[:, :-2], vp[:, 1:-1], vp[:, 2:]], axis=2)
    qpos = jnp.arange(Lp).reshape(nb, W)
    kpos = jnp.arange(nb)[:, None] * W - W + jnp.arange(3 * W)[None, :]
    rel = kpos[:, None, :] - qpos[:, :, None]
    mask = (jnp.abs(rel) <= W) & (kpos[:, None, :] >= 0) & (kpos[:, None, :] < L)
    s = jnp.einsum('nbqd,nbkd->nbqk', qb, kb).astype(jnp.float32) * (D ** -0.5)
    s = jnp.where(mask, s, NEG_INF)
    m = jnp.max(s, axis=-1, keepdims=True)
    p = jnp.exp(s - m)
    denom = jnp.sum(p, axis=-1, keepdims=True)
    o = jnp.einsum('nbqk,nbkd->nbqd', (p / denom).astype(v.dtype), vb)
    lse = (m + jnp.log(denom))[..., 0]
    return o.reshape(N, Lp, D)[:, :L], lse.reshape(N, Lp)[:, :L]


def dilated_branch(q, k, v, window, dil):
    B, H, S, D = q.shape
    L = S // dil

    def strided(a):
        return a.reshape(B, H, L, dil, D).transpose(0, 1, 3, 2, 4).reshape(B * H * dil, L, D)

    o, lse = banded_attention(strided(q), strided(k), strided(v), window // (2 * dil))
    o = o.reshape(B, H, dil, L, D).transpose(0, 1, 3, 2, 4).reshape(B, H, S, D)
    lse = lse.reshape(B, H, dil, L).transpose(0, 1, 3, 2).reshape(B, H, S)
    return o, lse


def dilated_mixer(q, k, v, q_norm, k_norm, cos, sin):
    B, S, _ = q.shape

    def heads(a):
        return a.reshape(B, S, DIL_HEADS, DIL_DH).transpose(0, 2, 1, 3)

    def partial_rope(a):
        return jnp.concatenate([apply_rope(a[..., :DIL_ROT], cos, sin), a[..., DIL_ROT:]], axis=-1)

    qh = partial_rope(rms_norm(heads(q), q_norm))
    kh = partial_rope(rms_norm(heads(k), k_norm))
    vh = heads(v)
    outs = []
    lses = []
    for window, dil in DIL_PAIRS:
        o, lse = dilated_branch(qh, kh, vh, window, dil)
        outs.append(o)
        lses.append(lse)
    wts = jax.nn.softmax(jnp.stack(lses), axis=0)
    o = jnp.einsum('pbhs,pbhsd->bhsd', wts, jnp.stack(outs).astype(jnp.float32)).astype(v.dtype)
    return o.transpose(0, 2, 1, 3).reshape(B, S, DIL_W)


def hybrid_layer(x, norm_mix, w_in, ml_i_bias, ml_f_bias, ml_out_norm, mla_q_norm, mla_w_q_b,
                 mla_kv_norm, mla_w_kv_b, mla_q_head_norm, mla_k_head_norm, dil_q_norm, dil_k_norm,
                 w_out, norm_ff, w_ff1, w_ff2, cos_b, sin_b, cos_c, sin_c):
    h = rms_norm(x, norm_mix)
    z = jnp.einsum('bsd,de->bse', h, w_in)
    (ml_q, ml_k, ml_v, ml_o, ml_g, mla_cq, mla_ckv, dil_q, dil_k, dil_v) = jnp.split(
        z, np.cumsum(IN_SPLITS)[:-1].tolist(), axis=-1)
    y_a = mlstm_mixer(ml_q, ml_k, ml_v, ml_o, ml_g, ml_i_bias, ml_f_bias, ml_out_norm)
    y_b = mla_mixer(mla_cq, mla_ckv, mla_q_norm, mla_w_q_b, mla_kv_norm, mla_w_kv_b,
                    mla_q_head_norm, mla_k_head_norm, cos_b, sin_b)
    y_c = dilated_mixer(dil_q, dil_k, dil_v, dil_q_norm, dil_k_norm, cos_c, sin_c)
    y = jnp.concatenate([y_a, y_b, y_c], axis=-1)
    x = x + jnp.einsum('bse,ed->bsd', y, w_out)
    h = rms_norm(x, norm_ff)
    u = jax.nn.relu(jnp.einsum('bsd,df->bsf', h, w_ff1))
    return x + jnp.einsum('bsf,fd->bsd', u * u, w_ff2)


def setup_inputs(seed: int = 0) -> dict:
    key = jax.random.key(seed)
    ks = jax.random.split(key, 19)

    def dense(k, shape, fan_in):
        return jax.random.normal(k, shape, jnp.float32) * (fan_in ** -0.5)

    def gain(k, shape):
        return 1.0 + 0.02 * jax.random.normal(k, shape, jnp.float32)

    x = jax.random.normal(ks[0], (BATCH, SEQ, D_MODEL), jnp.float32)
    ml_i_bias = 0.1 * jax.random.normal(ks[3], (DEPTH, 2, ML_HEADS), jnp.float32)
    ml_f_bias = (jnp.linspace(3.0, 6.0, ML_HEADS, dtype=jnp.float32)[None, None, :]
                 + 0.1 * jax.random.normal(ks[4], (DEPTH, 2, ML_HEADS), jnp.float32))
    return {
        'x': x,
        'norm_mix': gain(ks[1], (DEPTH, D_MODEL)),
        'w_in': dense(ks[2], (DEPTH, D_MODEL, D_IN), D_MODEL),
        'ml_i_bias': ml_i_bias,
        'ml_f_bias': ml_f_bias,
        'ml_out_norm': gain(ks[5], (DEPTH, ML_HEADS, ML_DV)),
        'mla_q_norm': gain(ks[6], (DEPTH, MLA_Q_RANK)),
        'mla_w_q_b': dense(ks[7], (DEPTH, MLA_Q_RANK, MLA_HEADS * (MLA_NOPE + MLA_ROPE)), MLA_Q_RANK),
        'mla_kv_norm': gain(ks[8], (DEPTH, MLA_KV_RANK)),
        'mla_w_kv_b': dense(ks[9], (DEPTH, MLA_KV_RANK, MLA_HEADS * (MLA_NOPE + MLA_V)), MLA_KV_RANK),
        'mla_q_head_norm': gain(ks[10], (DEPTH, MLA_NOPE + MLA_ROPE)),
        'mla_k_head_norm': gain(ks[11], (DEPTH, MLA_NOPE + MLA_ROPE)),
        'dil_q_norm': gain(ks[12], (DEPTH, DIL_DH)),
        'dil_k_norm': gain(ks[13], (DEPTH, DIL_DH)),
        'w_out': dense(ks[14], (DEPTH, D_MIX, D_MODEL), D_MIX),
        'norm_ff': gain(ks[15], (DEPTH, D_MODEL)),
        'w_ff1': dense(ks[16], (DEPTH, D_MODEL, D_FF), D_MODEL),
        'w_ff2': dense(ks[17], (DEPTH, D_FF, D_MODEL), D_FF),
    }


def reference(x, norm_mix, w_in, ml_i_bias, ml_f_bias, ml_out_norm, mla_q_norm, mla_w_q_b,
              mla_kv_norm, mla_w_kv_b, mla_q_head_norm, mla_k_head_norm, dil_q_norm, dil_k_norm,
              w_out, norm_ff, w_ff1, w_ff2):
    seq = x.shape[1]
    cos_b, sin_b = rope_tables(seq, MLA_ROPE)
    cos_c, sin_c = rope_tables(seq, DIL_ROT)
    for l in range(DEPTH):
        x = hybrid_layer(x, norm_mix[l], w_in[l], ml_i_bias[l], ml_f_bias[l], ml_out_norm[l],
                         mla_q_norm[l], mla_w_q_b[l], mla_kv_norm[l], mla_w_kv_b[l],
                         mla_q_head_norm[l], mla_k_head_norm[l], dil_q_norm[l], dil_k_norm[l],
                         w_out[l], norm_ff[l], w_ff1[l], w_ff2[l], cos_b, sin_b, cos_c, sin_c)
    return x
```

```python
import functools
import math

import numpy as np
import jax
import jax.numpy as jnp
from jax import lax
from jax.experimental import pallas as pl
from jax.experimental.pallas import tpu as pltpu

EPS = 1e-6
NEG_INF = -1e30
ROPE_THETA = 500000.0

ML_HEADS = 4
ML_DK = 64
ML_DV = 128
ML_W = ML_HEADS * ML_DV

MLA_HEADS = 6
MLA_Q_RANK = 384
MLA_KV_RANK = 128
MLA_NOPE = 128
MLA_ROPE = 64
MLA_V = 128
MLA_W = MLA_HEADS * MLA_V
MLA_QK_PAD = 256

DIL_HEADS = 6
DIL_DH = 128
DIL_ROT = DIL_DH // 4
DIL_PAIRS = ((128, 1), (512, 4), (2048, 16))
DIL_W = DIL_HEADS * DIL_DH

LANES = 128
VMEM_LIMIT_BYTES = 56 * 1024 * 1024

_C_MLQ = 0
_C_MLK = 256
_C_MLV = 512
_C_MLO = 1024
_C_GATE = 1536
_C_CQ = 1664
_C_CKV = 2048
_C_DQ = 2304
_C_DK = 3072
_C_DV = 3840
_C_END = 4608


def _rms(x, gain, n):
    ms = jnp.sum(x * x, axis=-1, keepdims=True) * (1.0 / n)
    return x * lax.rsqrt(ms + EPS) * gain


def _rope_tile(x, cos_f, sin_f, half):
    lane = lax.broadcasted_iota(jnp.int32, x.shape, 1)
    sw = jnp.where(lane < half, pltpu.roll(x, LANES - half, 1), pltpu.roll(x, half, 1))
    return x * cos_f + sw * sin_f


def _log_sigmoid(x):
    return jnp.minimum(x, 0.0) - jnp.log1p(jnp.exp(-jnp.abs(x)))


def _inproj_kernel(x_ref, gmix_ref, w_ref, wqb_ref, wkvb_ref, gbias_ref, qlat_ref, kvlat_ref,
                   hg_ref, cosb_ref, sinb_ref, cosc_ref, sinc_ref,
                   mlq_ref, mlk_ref, mlv_ref, mlo_ref, gate_ref, q_ref, k_ref, v_ref,
                   dq_ref, dk_ref, dv_ref, *, mla_scale, dil_scale):
    x = x_ref[...]
    h = _rms(x, gmix_ref[...], x.shape[-1]).astype(jnp.bfloat16)

    def proj(c0, c1):
        return jnp.dot(h, w_ref[:, c0:c1], preferred_element_type=jnp.float32)

    mlq_ref[...] = (proj(_C_MLQ, _C_MLK) * (ML_DK ** -0.5)).astype(jnp.bfloat16)
    mlk_ref[...] = proj(_C_MLK, _C_MLV).astype(jnp.bfloat16)
    mlv_ref[...] = proj(_C_MLV, _C_MLO).astype(jnp.bfloat16)
    mlo_ref[...] = jax.nn.sigmoid(proj(_C_MLO, _C_GATE)).astype(jnp.bfloat16)
    g = proj(_C_GATE, _C_CQ) + gbias_ref[...]
    lane = lax.broadcasted_iota(jnp.int32, g.shape, 1)
    is_forget = (lane % 8) >= 4
    gate_ref[...] = jnp.where(is_forget, _log_sigmoid(g), g)

    cos_b = cosb_ref[...]
    sin_b = sinb_ref[...]
    cos_c = cosc_ref[...]
    sin_c = sinc_ref[...]
    qg_n = hg_ref[0:1, :]
    qg_r = hg_ref[1:2, :]
    kg_n = hg_ref[2:3, :]
    kg_r = hg_ref[3:4, :]
    dqg = hg_ref[4:5, :]
    dkg = hg_ref[5:6, :]

    cq = _rms(proj(_C_CQ, _C_CKV), qlat_ref[...], MLA_Q_RANK).astype(jnp.bfloat16)
    zq = jnp.dot(cq, wqb_ref[...], preferred_element_type=jnp.float32)
    zc = proj(_C_CKV, _C_DQ)
    ckv = _rms(zc[:, :MLA_KV_RANK], kvlat_ref[...], MLA_KV_RANK).astype(jnp.bfloat16)
    zkv = jnp.dot(ckv, wkvb_ref[...], preferred_element_type=jnp.float32)
    k_rope = _rope_tile(_rms(zc[:, MLA_KV_RANK:], kg_r, MLA_ROPE), cos_b, sin_b, MLA_ROPE // 2)
    k_rope = k_rope.astype(jnp.bfloat16)
    for hd in range(MLA_HEADS):
        c = hd * MLA_QK_PAD
        q_nope = _rms(zq[:, c:c + MLA_NOPE], qg_n, MLA_NOPE)
        q_rope = _rope_tile(_rms(zq[:, c + MLA_NOPE:c + MLA_QK_PAD], qg_r, MLA_ROPE),
                            cos_b, sin_b, MLA_ROPE // 2)
        q_ref[hd, :, 0:MLA_NOPE] = (q_nope * mla_scale).astype(jnp.bfloat16)
        q_ref[hd, :, MLA_NOPE:MLA_QK_PAD] = (q_rope * mla_scale).astype(jnp.bfloat16)
        k_nope = _rms(zkv[:, c:c + MLA_NOPE], kg_n, MLA_NOPE)
        k_ref[hd, :, 0:MLA_NOPE] = k_nope.astype(jnp.bfloat16)
        k_ref[hd, :, MLA_NOPE:MLA_QK_PAD] = k_rope
        v_ref[hd, :, :] = zkv[:, c + MLA_NOPE:c + MLA_QK_PAD].astype(jnp.bfloat16)

    zdq = proj(_C_DQ, _C_DK)
    zdk = proj(_C_DK, _C_DV)
    for hd in range(DIL_HEADS):
        c = hd * DIL_DH
        qh = _rope_tile(_rms(zdq[:, c:c + DIL_DH], dqg, DIL_DH), cos_c, sin_c, DIL_ROT // 2)
        kh = _rope_tile(_rms(zdk[:, c:c + DIL_DH], dkg, DIL_DH), cos_c, sin_c, DIL_ROT // 2)
        dq_ref[:, c:c + DIL_DH] = (qh * dil_scale).astype(jnp.bfloat16)
        dk_ref[:, c:c + DIL_DH] = kh.astype(jnp.bfloat16)
    dv_ref[...] = proj(_C_DV, _C_END).astype(jnp.bfloat16)


def _const_spec(shape):
    n = len(shape)
    return pl.BlockSpec(shape, lambda *_: (0,) * n, pipeline_mode=pl.Buffered(1))


def _inproj(x2, gmix, w, wqb, wkvb, gbias, qlat, kvlat, hg, cosb, sinb, cosc, sinc, *, seq, tm):
    t, d = x2.shape
    nt = t // tm
    ns = seq // tm
    tok = lambda n: pl.BlockSpec((tm, n), lambda i: (i, 0))
    pos = pl.BlockSpec((tm, LANES), lambda i: (i % ns, 0))
    headed = lambda n: pl.BlockSpec((MLA_HEADS, tm, n), lambda i: (0, i, 0))
    bf = jnp.bfloat16
    out_shape = (
        jax.ShapeDtypeStruct((t, 256), bf), jax.ShapeDtypeStruct((t, 256), bf),
        jax.ShapeDtypeStruct((t, ML_W), bf), jax.ShapeDtypeStruct((t, ML_W), bf),
        jax.ShapeDtypeStruct((t, LANES), jnp.float32),
        jax.ShapeDtypeStruct((MLA_HEADS, t, MLA_QK_PAD), bf),
        jax.ShapeDtypeStruct((MLA_HEADS, t, MLA_QK_PAD), bf),
        jax.ShapeDtypeStruct((MLA_HEADS, t, MLA_V), bf),
        jax.ShapeDtypeStruct((t, DIL_W), bf), jax.ShapeDtypeStruct((t, DIL_W), bf),
        jax.ShapeDtypeStruct((t, DIL_W), bf),
    )
    out_specs = (tok(256), tok(256), tok(ML_W), tok(ML_W), tok(LANES),
                 headed(MLA_QK_PAD), headed(MLA_QK_PAD), headed(MLA_V),
                 tok(DIL_W), tok(DIL_W), tok(DIL_W))
    in_specs = [tok(d), _const_spec(gmix.shape), _const_spec(w.shape), _const_spec(wqb.shape),
                _const_spec(wkvb.shape), _const_spec(gbias.shape), _const_spec(qlat.shape),
                _const_spec(kvlat.shape), _const_spec(hg.shape), pos, pos, pos, pos]
    kern = functools.partial(_inproj_kernel,
                             mla_scale=(MLA_NOPE + MLA_ROPE) ** -0.5, dil_scale=DIL_DH ** -0.5)
    return pl.pallas_call(
        kern, grid=(nt,), in_specs=in_specs, out_specs=out_specs, out_shape=out_shape,
        compiler_params=pltpu.CompilerParams(dimension_semantics=("arbitrary",),
                                             vmem_limit_bytes=VMEM_LIMIT_BYTES),
        name="inproj",
    )(x2, gmix, w, wqb, wkvb, gbias, qlat, kvlat, hg, cosb, sinb, cosc, sinc)


def _mlstm_kernel(*refs, reverse, finalize, chunk):
    if finalize:
        (q_ref, k_ref, v_ref, gate_ref, hprev_ref, o_ref, onorm_ref, out_ref, ct_ref, m_ref) = refs
    else:
        (q_ref, k_ref, v_ref, gate_ref, out_ref, ct_ref, m_ref) = refs
    L = chunk
    c = pl.program_id(1)

    @pl.when(c == 0)
    def _():
        ct_ref[...] = jnp.zeros_like(ct_ref)
        m_ref[...] = jnp.zeros_like(m_ref)

    row = lax.broadcasted_iota(jnp.int32, (L, L), 0)
    col = lax.broadcasted_iota(jnp.int32, (L, L), 1)
    if reverse:
        causal = col >= row
    else:
        causal = col <= row
    tri = causal.astype(jnp.float32)

    gates = gate_ref[...]
    gates_t = gates.T
    d0 = 8 if reverse else 0
    cum_col_all = jnp.dot(tri, gates, preferred_element_type=jnp.float32,
                          precision=lax.Precision.HIGHEST)
    cum_row_all = lax.dot_general(gates_t, tri, (((1,), (1,)), ((), ())),
                                  preferred_element_type=jnp.float32,
                                  precision=lax.Precision.HIGHEST)
    tot_all = jnp.sum(gates, axis=0, keepdims=True)

    lane = lax.broadcasted_iota(jnp.int32, (L, LANES), 1)
    ones_col = jnp.where(lane == 0, 1.0, 0.0).astype(jnp.bfloat16)

    for hd in range(ML_HEADS):
        gi = d0 + hd
        gf = d0 + 4 + hd
        i_col = gates[:, gi:gi + 1]
        i_row = gates_t[gi:gi + 1, :]
        cum_col = cum_col_all[:, gf:gf + 1]
        cum_row = cum_row_all[gf:gf + 1, :]
        tot = tot_all[:, gf:gf + 1]
        m_prev = m_ref[hd:hd + 1, 0:1]

        pair = hd // 2
        in_head = (lane >= (hd % 2) * ML_DK) & (lane < (hd % 2 + 1) * ML_DK)
        qp = q_ref[:, pair * LANES:(pair + 1) * LANES]
        kp = k_ref[:, pair * LANES:(pair + 1) * LANES]
        qm = jnp.where(in_head, qp, jnp.zeros_like(qp))
        km = jnp.where(in_head, kp, jnp.zeros_like(kp))
        v_h = v_ref[:, hd * ML_DV:(hd + 1) * ML_DV]
        v_aug = jnp.concatenate([v_h, ones_col], axis=1)

        d_mat = jnp.where(causal, cum_col + (i_row - cum_row), NEG_INF)
        d_inter = cum_col + m_prev
        m_t = jnp.maximum(d_inter, jnp.max(d_mat, axis=1, keepdims=True))
        w_intra = jnp.exp(d_mat - m_t)
        w_inter = jnp.exp(d_inter - m_t)
        s_raw = lax.dot_general(qm, kp, (((1,), (1,)), ((), ())), preferred_element_type=jnp.float32)
        sw = (s_raw * w_intra).astype(jnp.bfloat16)
        ct = ct_ref[hd]
        r = (jnp.dot(sw, v_aug, preferred_element_type=jnp.float32)
             + w_inter * jnp.dot(qm, ct.astype(jnp.bfloat16), preferred_element_type=jnp.float32))
        num = r[:, :ML_DV]
        den = r[:, ML_DV:ML_DV + 1]
        h_dir = num / jnp.maximum(jnp.abs(den), jnp.exp(-m_t))

        d_state = tot - cum_col + i_col
        m_new = jnp.maximum(tot + m_prev, jnp.max(d_state, axis=0, keepdims=True))
        w_s = jnp.exp(d_state - m_new)
        w_c = jnp.exp(tot + m_prev - m_new)
        vw = (v_aug.astype(jnp.float32) * w_s).astype(jnp.bfloat16)
        upd = lax.dot_general(km, vw, (((0,), (0,)), ((), ())), preferred_element_type=jnp.float32)
        ct_ref[hd] = w_c * ct + upd
        m_ref[hd:hd + 1, :] = jnp.broadcast_to(m_new, (1, LANES))

        sl = slice(hd * ML_DV, (hd + 1) * ML_DV)
        if finalize:
            h_sum = h_dir + hprev_ref[:, sl]
            y = _rms(h_sum, onorm_ref[hd:hd + 1, :], ML_DV)
            out_ref[:, sl] = (o_ref[:, sl].astype(jnp.float32) * y).astype(out_ref.dtype)
        else:
            out_ref[:, sl] = h_dir


def _mlstm(mlq, mlk, mlv, gates, hprev, mlo, onorm, *, batch, seq, chunk, reverse):
    t = mlq.shape[0]
    nc = seq // chunk
    finalize = hprev is not None
    if reverse:
        tmap = lambda b, c: (b * nc + (nc - 1 - c), 0)
    else:
        tmap = lambda b, c: (b * nc + c, 0)
    tok = lambda n: pl.BlockSpec((chunk, n), tmap)
    in_specs = [tok(256), tok(256), tok(ML_W), tok(LANES)]
    args = [mlq, mlk, mlv, gates]
    if finalize:
        in_specs += [tok(ML_W), tok(ML_W), pl.BlockSpec(onorm.shape, lambda b, c: (0, 0))]
        args += [hprev, mlo, onorm]
        out_dtype = jnp.bfloat16
    else:
        out_dtype = jnp.float32
    kern = functools.partial(_mlstm_kernel, reverse=reverse, finalize=finalize, chunk=chunk)
    return pl.pallas_call(
        kern, grid=(batch, nc), in_specs=in_specs, out_specs=tok(ML_W),
        out_shape=jax.ShapeDtypeStruct((t, ML_W), out_dtype),
        scratch_shapes=[pltpu.VMEM((ML_HEADS, LANES, 2 * LANES), jnp.float32),
                        pltpu.VMEM((8, LANES), jnp.float32)],
        compiler_params=pltpu.CompilerParams(dimension_semantics=("arbitrary", "arbitrary"),
                                             vmem_limit_bytes=VMEM_LIMIT_BYTES),
        name="mlstm_bwd" if reverse else "mlstm_fwd",
    )(*args)


def _mla_kernel(q_ref, k_ref, v_ref, o_ref, *, tk):
    q = q_ref[...]
    tq = q.shape[0]
    nk = k_ref.shape[0] // tk

    def body(j, carry):
        m_prev, l_prev, acc = carry
        start = pl.multiple_of(j * tk, tk)
        kt = k_ref[pl.ds(start, tk), :]
        vt = v_ref[pl.ds(start, tk), :]
        s = lax.dot_general(q, kt, (((1,), (1,)), ((), ())), preferred_element_type=jnp.float32)
        m_new = jnp.maximum(m_prev, jnp.max(s, axis=1, keepdims=True))
        alpha = jnp.exp(m_prev - m_new)
        p = jnp.exp(s - m_new)
        l_new = alpha * l_prev + jnp.sum(p, axis=1, keepdims=True)
        acc = alpha * acc + jnp.dot(p.astype(jnp.bfloat16), vt, preferred_element_type=jnp.float32)
        return m_new, l_new, acc

    init = (jnp.full((tq, 1), NEG_INF, jnp.float32), jnp.zeros((tq, 1), jnp.float32),
            jnp.zeros((tq, MLA_V), jnp.float32))
    _, l_fin, acc = lax.fori_loop(0, nk, body, init)
    o_ref[...] = (acc / l_fin).astype(o_ref.dtype)


def _mla_attention(q, k, v, *, batch, seq, tq, tk):
    t = q.shape[1]
    nq = seq // tq
    return pl.pallas_call(
        functools.partial(_mla_kernel, tk=tk),
        grid=(batch, MLA_HEADS, nq),
        in_specs=[pl.BlockSpec((None, tq, MLA_QK_PAD), lambda b, h, i: (h, b * nq + i, 0)),
                  pl.BlockSpec((None, seq, MLA_QK_PAD), lambda b, h, i: (h, b, 0)),
                  pl.BlockSpec((None, seq, MLA_V), lambda b, h, i: (h, b, 0))],
        out_specs=pl.BlockSpec((tq, MLA_V), lambda b, h, i: (b * nq + i, h)),
        out_shape=jax.ShapeDtypeStruct((t, MLA_W), jnp.bfloat16),
        compiler_params=pltpu.CompilerParams(
            dimension_semantics=("arbitrary", "arbitrary", "arbitrary"),
            vmem_limit_bytes=VMEM_LIMIT_BYTES),
        name="mla_attn",
    )(q, k, v)


DIL_TILE = 256
DIL_REACH = max(w // 2 for w, _ in DIL_PAIRS)
DIL_NOFF = DIL_REACH // DIL_TILE


def _dil_bias_table():
    r = np.arange(DIL_TILE)[:, None]
    c = np.arange(DIL_TILE)[None, :]
    tiles = []
    for o in range(-DIL_NOFF, DIL_NOFF + 1):
        delta = o * DIL_TILE + c - r
        mult = np.zeros_like(delta)
        for window, dil in DIL_PAIRS:
            mult += ((delta % dil) == 0) & (np.abs(delta) <= window // 2)
        tiles.append(np.where(mult > 0, np.log(np.maximum(mult, 1)), NEG_INF))
    return np.stack(tiles).astype(np.float32)


def _dil_kernel(q_ref, k_ref, v_ref, bias_ref, o_ref, *, nq):
    i = pl.program_id(2)
    q = q_ref[...]
    m = l = acc = None
    order = [0] + [s * d for d in range(1, DIL_NOFF + 1) for s in (-1, 1)]
    for o in order:
        kj = i + o
        valid = (kj >= 0) & (kj < nq)
        start = pl.multiple_of(jnp.clip(kj, 0, nq - 1) * DIL_TILE, DIL_TILE)
        kt = k_ref[pl.ds(start, DIL_TILE), :]
        vt = v_ref[pl.ds(start, DIL_TILE), :]
        bias = bias_ref[o + DIL_NOFF]
        if o != 0:
            bias = jnp.where(valid, bias, NEG_INF)
        s = lax.dot_general(q, kt, (((1,), (1,)), ((), ())), preferred_element_type=jnp.float32) + bias
        m_cur = jnp.max(s, axis=1, keepdims=True)
        if m is None:
            m = m_cur
            p = jnp.exp(s - m)
            l = jnp.sum(p, axis=1, keepdims=True)
            acc = jnp.dot(p.astype(jnp.bfloat16), vt, preferred_element_type=jnp.float32)
        else:
            m_new = jnp.maximum(m, m_cur)
            alpha = jnp.exp(m - m_new)
            p = jnp.exp(s - m_new)
            l = alpha * l + jnp.sum(p, axis=1, keepdims=True)
            acc = alpha * acc + jnp.dot(p.astype(jnp.bfloat16), vt, preferred_element_type=jnp.float32)
            m = m_new
    o_ref[...] = (acc / l).astype(o_ref.dtype)


def _dil_attention(dq, dk, dv, bias, *, batch, seq):
    t = dq.shape[0]
    nq = seq // DIL_TILE
    return pl.pallas_call(
        functools.partial(_dil_kernel, nq=nq),
        grid=(batch, DIL_HEADS, nq),
        in_specs=[pl.BlockSpec((DIL_TILE, DIL_DH), lambda b, h, i: (b * nq + i, h)),
                  pl.BlockSpec((seq, DIL_DH), lambda b, h, i: (b, h)),
                  pl.BlockSpec((seq, DIL_DH), lambda b, h, i: (b, h)),
                  pl.BlockSpec(bias.shape, lambda b, h, i: (0, 0, 0))],
        out_specs=pl.BlockSpec((DIL_TILE, DIL_DH), lambda b, h, i: (b * nq + i, h)),
        out_shape=jax.ShapeDtypeStruct((t, DIL_W), jnp.bfloat16),
        compiler_params=pltpu.CompilerParams(
            dimension_semantics=("arbitrary", "arbitrary", "arbitrary"),
            vmem_limit_bytes=VMEM_LIMIT_BYTES),
        name="dil_attn",
    )(dq, dk, dv, bias)


def _outproj_kernel(x_ref, ya_ref, yb_ref, yc_ref, wa_ref, wb_ref, wc_ref, o_ref):
    y = (jnp.dot(ya_ref[...], wa_ref[...], preferred_element_type=jnp.float32)
         + jnp.dot(yb_ref[...], wb_ref[...], preferred_element_type=jnp.float32)
         + jnp.dot(yc_ref[...], wc_ref[...], preferred_element_type=jnp.float32))
    o_ref[...] = x_ref[...] + y


def _outproj(x2, ya, yb, yc, wa, wb, wc, *, tm):
    t, d = x2.shape
    tok = lambda n: pl.BlockSpec((tm, n), lambda i: (i, 0))
    return pl.pallas_call(
        _outproj_kernel, grid=(t // tm,),
        in_specs=[tok(d), tok(ML_W), tok(MLA_W), tok(DIL_W),
                  _const_spec(wa.shape), _const_spec(wb.shape), _const_spec(wc.shape)],
        out_specs=tok(d), out_shape=jax.ShapeDtypeStruct((t, d), jnp.float32),
        compiler_params=pltpu.CompilerParams(dimension_semantics=("arbitrary",),
                                             vmem_limit_bytes=VMEM_LIMIT_BYTES),
        name="outproj",
    )(x2, ya, yb, yc, wa, wb, wc)


def _ffn_kernel(x_ref, g_ref, w1_ref, w2_ref, o_ref, h_ref):
    j = pl.program_id(1)

    @pl.when(j == 0)
    def _():
        x = x_ref[...]
        h_ref[...] = _rms(x, g_ref[...], x.shape[-1]).astype(h_ref.dtype)
        o_ref[...] = x

    u = jnp.maximum(jnp.dot(h_ref[...], w1_ref[...], preferred_element_type=jnp.float32), 0.0)
    u = (u * u).astype(jnp.bfloat16)
    o_ref[...] += jnp.dot(u, w2_ref[...], preferred_element_type=jnp.float32)


def _ffn(x2, g, w1, w2, *, tm, tf):
    t, d = x2.shape
    dff = w1.shape[1]
    return pl.pallas_call(
        _ffn_kernel, grid=(t // tm, dff // tf),
        in_specs=[pl.BlockSpec((tm, d), lambda i, j: (i, 0)),
                  pl.BlockSpec(g.shape, lambda i, j: (0, 0)),
                  pl.BlockSpec((d, tf), lambda i, j: (0, j)),
                  pl.BlockSpec((tf, d), lambda i, j: (j, 0))],
        out_specs=pl.BlockSpec((tm, d), lambda i, j: (i, 0)),
        out_shape=jax.ShapeDtypeStruct((t, d), jnp.float32),
        scratch_shapes=[pltpu.VMEM((tm, d), jnp.bfloat16)],
        compiler_params=pltpu.CompilerParams(dimension_semantics=("arbitrary", "arbitrary"),
                                             vmem_limit_bytes=VMEM_LIMIT_BYTES),
        name="ffn",
    )(x2, g, w1, w2)


def _rope_tables(seq, rot_dim, fill_cos):
    pos = jnp.arange(seq, dtype=jnp.float32)
    inv_freq = ROPE_THETA ** (-jnp.arange(0, rot_dim, 2, dtype=jnp.float32) / rot_dim)
    ang = pos[:, None] * inv_freq[None, :]
    cos, sin = jnp.cos(ang), jnp.sin(ang)
    pad = LANES - rot_dim
    cos_f = jnp.concatenate([cos, cos, jnp.full((seq, pad), fill_cos, jnp.float32)], axis=1)
    sin_f = jnp.concatenate([-sin, sin, jnp.zeros((seq, pad), jnp.float32)], axis=1)
    return cos_f, sin_f


def _pad_cols(a, n):
    return jnp.pad(a, ((0, 0), (0, n - a.shape[1])))


def _pack_w_in(w_in):
    splits = np.cumsum((256, 256, ML_W, ML_W, 4 * ML_HEADS, MLA_Q_RANK, MLA_KV_RANK + MLA_ROPE,
                        DIL_W, DIL_W, DIL_W))[:-1].tolist()
    (wq, wk, wv, wo, wg, wcq, wckv, wdq, wdk, wdv) = jnp.split(w_in, splits, axis=1)
    packed = jnp.concatenate([wq, wk, wv, wo, _pad_cols(wg, LANES), wcq, _pad_cols(wckv, 256),
                              wdq, wdk, wdv], axis=1)
    return packed.astype(jnp.bfloat16)


def _layer(x2, p, tables, *, batch, seq):
    cosb, sinb, cosc, sinc, dil_bias = tables
    w = _pack_w_in(p['w_in'])
    wqb = p['mla_w_q_b'].reshape(MLA_Q_RANK, MLA_HEADS, MLA_NOPE + MLA_ROPE)
    wqb = jnp.pad(wqb, ((0, 0), (0, 0), (0, MLA_QK_PAD - MLA_NOPE - MLA_ROPE)))
    wqb = wqb.reshape(MLA_Q_RANK, MLA_HEADS * MLA_QK_PAD).astype(jnp.bfloat16)
    wkvb = p['mla_w_kv_b'].astype(jnp.bfloat16)
    gbias = jnp.concatenate([p['ml_i_bias'][0], p['ml_f_bias'][0], p['ml_i_bias'][1], p['ml_f_bias'][1]])
    gbias = _pad_cols(gbias[None, :], LANES)
    qh, kh = p['mla_q_head_norm'], p['mla_k_head_norm']
    hg = jnp.stack([qh[:MLA_NOPE], jnp.pad(qh[MLA_NOPE:], (0, LANES - MLA_ROPE)),
                    kh[:MLA_NOPE], jnp.pad(kh[MLA_NOPE:], (0, LANES - MLA_ROPE)),
                    p['dil_q_norm'], p['dil_k_norm'],
                    jnp.zeros((LANES,), jnp.float32), jnp.zeros((LANES,), jnp.float32)])

    (mlq, mlk, mlv, mlo, gates, q, k, v, dq, dk, dv) = _inproj(
        x2, p['norm_mix'][None, :], w, wqb, wkvb, gbias, p['mla_q_norm'][None, :],
        p['mla_kv_norm'][None, :], hg, cosb, sinb, cosc, sinc, seq=seq, tm=min(512, seq))

    chunk = min(256, seq)
    h_bwd = _mlstm(mlq, mlk, mlv, gates, None, None, None, batch=batch, seq=seq, chunk=chunk, reverse=True)
    ya = _mlstm(mlq, mlk, mlv, gates, h_bwd, mlo, p['ml_out_norm'], batch=batch, seq=seq, chunk=chunk,
                reverse=False)
    yb = _mla_attention(q, k, v, batch=batch, seq=seq, tq=min(512, seq), tk=min(512, seq))
    yc = _dil_attention(dq, dk, dv, dil_bias, batch=batch, seq=seq)

    w_out = p['w_out'].astype(jnp.bfloat16)
    x2 = _outproj(x2, ya, yb, yc, w_out[:ML_W], w_out[ML_W:ML_W + MLA_W], w_out[ML_W + MLA_W:],
                  tm=min(512, seq))
    return _ffn(x2, p['norm_ff'][None, :], p['w_ff1'].astype(jnp.bfloat16), p['w_ff2'].astype(jnp.bfloat16),
                tm=min(512, seq), tf=1024)


def kernel(x, norm_mix, w_in, ml_i_bias, ml_f_bias, ml_out_norm, mla_q_norm, mla_w_q_b, mla_kv_norm,
           mla_w_kv_b, mla_q_head_norm, mla_k_head_norm, dil_q_norm, dil_k_norm, w_out, norm_ff,
           w_ff1, w_ff2):
    batch, seq, d = x.shape
    assert seq % DIL_TILE == 0
    params = dict(norm_mix=norm_mix, w_in=w_in, ml_i_bias=ml_i_bias, ml_f_bias=ml_f_bias,
                  ml_out_norm=ml_out_norm, mla_q_norm=mla_q_norm, mla_w_q_b=mla_w_q_b,
                  mla_kv_norm=mla_kv_norm, mla_w_kv_b=mla_w_kv_b, mla_q_head_norm=mla_q_head_norm,
                  mla_k_head_norm=mla_k_head_norm, dil_q_norm=dil_q_norm, dil_k_norm=dil_k_norm,
                  w_out=w_out, norm_ff=norm_ff, w_ff1=w_ff1, w_ff2=w_ff2)
    cosb, sinb = _rope_tables(seq, MLA_ROPE, 0.0)
    cosc, sinc = _rope_tables(seq, DIL_ROT, 1.0)
    tables = (cosb, sinb, cosc, sinc, jnp.asarray(_dil_bias_table()))
    x2 = x.reshape(batch * seq, d)
    for layer in range(norm_mix.shape[0]):
        x2 = _layer(x2, {name: val[layer] for name, val in params.items()}, tables, batch=batch, seq=seq)
    return x2.reshape(batch, seq, d)
```

```python
import functools
import math

import numpy as np
import jax
import jax.numpy as jnp
from jax import lax
from jax.experimental import pallas as pl
from jax.experimental.pallas import tpu as pltpu

EPS = 1e-6
NEG_INF = -1e30
ROPE_THETA = 500000.0

ML_HEADS = 4
ML_DK = 64
ML_DV = 128
ML_W = ML_HEADS * ML_DV

MLA_HEADS = 6
MLA_Q_RANK = 384
MLA_KV_RANK = 128
MLA_NOPE = 128
MLA_ROPE = 64
MLA_V = 128
MLA_W = MLA_HEADS * MLA_V
MLA_QK_PAD = 256

DIL_HEADS = 6
DIL_DH = 128
DIL_ROT = DIL_DH // 4
DIL_PAIRS = ((128, 1), (512, 4), (2048, 16))
DIL_W = DIL_HEADS * DIL_DH

LANES = 128
VMEM_LIMIT_BYTES = 56 * 1024 * 1024

_C_MLQ = 0
_C_MLK = 256
_C_MLV = 512
_C_MLO = 1024
_C_GATE = 1536
_C_CQ = 1664
_C_CKV = 2048
_C_DQ = 2304
_C_DK = 3072
_C_DV = 3840
_C_END = 4608


def _rms(x, gain, n):
    ms = jnp.sum(x * x, axis=-1, keepdims=True) * (1.0 / n)
    return x * lax.rsqrt(ms + EPS) * gain


def _rope_tile(x, cos_f, sin_f, half):
    lane = lax.broadcasted_iota(jnp.int32, x.shape, 1)
    sw = jnp.where(lane < half, pltpu.roll(x, LANES - half, 1), pltpu.roll(x, half, 1))
    return x * cos_f + sw * sin_f


def _log_sigmoid(x):
    return jnp.minimum(x, 0.0) - jnp.log1p(jnp.exp(-jnp.abs(x)))


def _inproj_kernel(x_ref, gmix_ref, w_ref, wqb_ref, wkvb_ref, gbias_ref, qlat_ref, kvlat_ref,
                   hg_ref, cosb_ref, sinb_ref, cosc_ref, sinc_ref,
                   mlq_ref, mlk_ref, mlv_ref, mlo_ref, gate_ref, q_ref, k_ref, v_ref,
                   dq_ref, dk_ref, dv_ref, *, mla_scale, dil_scale):
    x = x_ref[...]
    h = _rms(x, gmix_ref[...], x.shape[-1]).astype(jnp.bfloat16)

    def proj(c0, c1):
        return jnp.dot(h, w_ref[:, c0:c1], preferred_element_type=jnp.float32)

    mlq_ref[...] = (proj(_C_MLQ, _C_MLK) * (ML_DK ** -0.5)).astype(jnp.bfloat16)
    mlk_ref[...] = proj(_C_MLK, _C_MLV).astype(jnp.bfloat16)
    mlv_ref[...] = proj(_C_MLV, _C_MLO).astype(jnp.bfloat16)
    mlo_ref[...] = jax.nn.sigmoid(proj(_C_MLO, _C_GATE)).astype(jnp.bfloat16)
    g = proj(_C_GATE, _C_CQ) + gbias_ref[...]
    lane = lax.broadcasted_iota(jnp.int32, g.shape, 1)
    is_forget = (lane % 8) >= 4
    gate_ref[...] = jnp.where(is_forget, _log_sigmoid(g), g)

    cos_b = cosb_ref[...]
    sin_b = sinb_ref[...]
    cos_c = cosc_ref[...]
    sin_c = sinc_ref[...]
    qg_n = hg_ref[0:1, :]
    qg_r = hg_ref[1:2, :]
    kg_n = hg_ref[2:3, :]
    kg_r = hg_ref[3:4, :]
    dqg = hg_ref[4:5, :]
    dkg = hg_ref[5:6, :]

    cq = _rms(proj(_C_CQ, _C_CKV), qlat_ref[...], MLA_Q_RANK).astype(jnp.bfloat16)
    zq = jnp.dot(cq, wqb_ref[...], preferred_element_type=jnp.float32)
    zc = proj(_C_CKV, _C_DQ)
    ckv = _rms(zc[:, :MLA_KV_RANK], kvlat_ref[...], MLA_KV_RANK).astype(jnp.bfloat16)
    zkv = jnp.dot(ckv, wkvb_ref[...], preferred_element_type=jnp.float32)
    k_rope = _rope_tile(_rms(zc[:, MLA_KV_RANK:], kg_r, MLA_ROPE), cos_b, sin_b, MLA_ROPE // 2)
    k_rope = k_rope.astype(jnp.bfloat16)
    ones_col = jnp.where(lax.broadcasted_iota(jnp.int32, k_rope.shape, 1) == 0, 1.0, 0.0)
    ones_col = ones_col.astype(jnp.bfloat16)
    for hd in range(MLA_HEADS):
        c = hd * MLA_QK_PAD
        q_nope = _rms(zq[:, c:c + MLA_NOPE], qg_n, MLA_NOPE)
        q_rope = _rope_tile(_rms(zq[:, c + MLA_NOPE:c + MLA_QK_PAD], qg_r, MLA_ROPE),
                            cos_b, sin_b, MLA_ROPE // 2)
        q_ref[hd, :, 0:MLA_NOPE] = (q_nope * mla_scale).astype(jnp.bfloat16)
        q_ref[hd, :, MLA_NOPE:MLA_QK_PAD] = (q_rope * mla_scale).astype(jnp.bfloat16)
        k_nope = _rms(zkv[:, c:c + MLA_NOPE], kg_n, MLA_NOPE)
        k_ref[hd, :, 0:MLA_NOPE] = k_nope.astype(jnp.bfloat16)
        k_ref[hd, :, MLA_NOPE:MLA_QK_PAD] = k_rope
        v_ref[hd, :, 0:MLA_V] = zkv[:, c + MLA_NOPE:c + MLA_QK_PAD].astype(jnp.bfloat16)
        v_ref[hd, :, MLA_V:2 * MLA_V] = ones_col

    zdq = proj(_C_DQ, _C_DK)
    zdk = proj(_C_DK, _C_DV)
    for hd in range(DIL_HEADS):
        c = hd * DIL_DH
        qh = _rope_tile(_rms(zdq[:, c:c + DIL_DH], dqg, DIL_DH), cos_c, sin_c, DIL_ROT // 2)
        kh = _rope_tile(_rms(zdk[:, c:c + DIL_DH], dkg, DIL_DH), cos_c, sin_c, DIL_ROT // 2)
        dq_ref[:, c:c + DIL_DH] = (qh * dil_scale).astype(jnp.bfloat16)
        dk_ref[:, c:c + DIL_DH] = kh.astype(jnp.bfloat16)
    dv_ref[...] = proj(_C_DV, _C_END).astype(jnp.bfloat16)


def _const_spec(shape):
    n = len(shape)
    return pl.BlockSpec(shape, lambda *_: (0,) * n, pipeline_mode=pl.Buffered(1))


def _inproj(x2, gmix, w, wqb, wkvb, gbias, qlat, kvlat, hg, cosb, sinb, cosc, sinc, *, seq, tm):
    t, d = x2.shape
    nt = t // tm
    ns = seq // tm
    tok = lambda n: pl.BlockSpec((tm, n), lambda i: (i, 0))
    pos = pl.BlockSpec((tm, LANES), lambda i: (i % ns, 0))
    headed = lambda n: pl.BlockSpec((MLA_HEADS, tm, n), lambda i: (0, i, 0))
    bf = jnp.bfloat16
    out_shape = (
        jax.ShapeDtypeStruct((t, 256), bf), jax.ShapeDtypeStruct((t, 256), bf),
        jax.ShapeDtypeStruct((t, ML_W), bf), jax.ShapeDtypeStruct((t, ML_W), bf),
        jax.ShapeDtypeStruct((t, LANES), jnp.float32),
        jax.ShapeDtypeStruct((MLA_HEADS, t, MLA_QK_PAD), bf),
        jax.ShapeDtypeStruct((MLA_HEADS, t, MLA_QK_PAD), bf),
        jax.ShapeDtypeStruct((MLA_HEADS, t, 2 * MLA_V), bf),
        jax.ShapeDtypeStruct((t, DIL_W), bf), jax.ShapeDtypeStruct((t, DIL_W), bf),
        jax.ShapeDtypeStruct((t, DIL_W), bf),
    )
    out_specs = (tok(256), tok(256), tok(ML_W), tok(ML_W), tok(LANES),
                 headed(MLA_QK_PAD), headed(MLA_QK_PAD), headed(2 * MLA_V),
                 tok(DIL_W), tok(DIL_W), tok(DIL_W))
    in_specs = [tok(d), _const_spec(gmix.shape), _const_spec(w.shape), _const_spec(wqb.shape),
                _const_spec(wkvb.shape), _const_spec(gbias.shape), _const_spec(qlat.shape),
                _const_spec(kvlat.shape), _const_spec(hg.shape), pos, pos, pos, pos]
    kern = functools.partial(_inproj_kernel,
                             mla_scale=(MLA_NOPE + MLA_ROPE) ** -0.5 * math.log2(math.e),
                             dil_scale=DIL_DH ** -0.5)
    return pl.pallas_call(
        kern, grid=(nt,), in_specs=in_specs, out_specs=out_specs, out_shape=out_shape,
        compiler_params=pltpu.CompilerParams(dimension_semantics=("arbitrary",),
                                             vmem_limit_bytes=VMEM_LIMIT_BYTES),
        name="inproj",
    )(x2, gmix, w, wqb, wkvb, gbias, qlat, kvlat, hg, cosb, sinb, cosc, sinc)


def _mlstm_kernel(*refs, reverse, finalize, chunk):
    if finalize:
        (q_ref, k_ref, v_ref, gate_ref, hprev_ref, o_ref, onorm_ref, out_ref, ct_ref, m_ref) = refs
    else:
        (q_ref, k_ref, v_ref, gate_ref, out_ref, ct_ref, m_ref) = refs
    L = chunk
    c = pl.program_id(1)

    @pl.when(c == 0)
    def _():
        ct_ref[...] = jnp.zeros_like(ct_ref)
        m_ref[...] = jnp.zeros_like(m_ref)

    row = lax.broadcasted_iota(jnp.int32, (L, L), 0)
    col = lax.broadcasted_iota(jnp.int32, (L, L), 1)
    if reverse:
        causal = col >= row
    else:
        causal = col <= row
    tri = causal.astype(jnp.float32)

    gates = gate_ref[...]
    gates_t = gates.T
    d0 = 8 if reverse else 0
    cum_col_all = jnp.dot(tri, gates, preferred_element_type=jnp.float32,
                          precision=lax.Precision.HIGHEST)
    cum_row_all = lax.dot_general(gates_t, tri, (((1,), (1,)), ((), ())),
                                  preferred_element_type=jnp.float32,
                                  precision=lax.Precision.HIGHEST)
    tot_all = jnp.sum(gates, axis=0, keepdims=True)

    lane = lax.broadcasted_iota(jnp.int32, (L, LANES), 1)
    ones_col = jnp.where(lane == 0, 1.0, 0.0).astype(jnp.bfloat16)

    for hd in range(ML_HEADS):
        gi = d0 + hd
        gf = d0 + 4 + hd
        i_col = gates[:, gi:gi + 1]
        i_row = gates_t[gi:gi + 1, :]
        cum_col = cum_col_all[:, gf:gf + 1]
        cum_row = cum_row_all[gf:gf + 1, :]
        tot = tot_all[:, gf:gf + 1]
        m_prev = m_ref[hd:hd + 1, 0:1]

        pair = hd // 2
        in_head = (lane >= (hd % 2) * ML_DK) & (lane < (hd % 2 + 1) * ML_DK)
        qp = q_ref[:, pair * LANES:(pair + 1) * LANES]
        kp = k_ref[:, pair * LANES:(pair + 1) * LANES]
        qm = jnp.where(in_head, qp, jnp.zeros_like(qp))
        km = jnp.where(in_head, kp, jnp.zeros_like(kp))
        v_h = v_ref[:, hd * ML_DV:(hd + 1) * ML_DV]
        v_aug = jnp.concatenate([v_h, ones_col], axis=1)

        d_mat = jnp.where(causal, cum_col + (i_row - cum_row), NEG_INF)
        d_inter = cum_col + m_prev
        m_t = jnp.maximum(d_inter, jnp.max(d_mat, axis=1, keepdims=True))
        w_intra = jnp.exp(d_mat - m_t)
        w_inter = jnp.exp(d_inter - m_t)
        s_raw = lax.dot_general(qm, kp, (((1,), (1,)), ((), ())), preferred_element_type=jnp.float32)
        sw = (s_raw * w_intra).astype(jnp.bfloat16)
        ct = ct_ref[hd]
        r = (jnp.dot(sw, v_aug, preferred_element_type=jnp.float32)
             + w_inter * jnp.dot(qm, ct.astype(jnp.bfloat16), preferred_element_type=jnp.float32))
        num = r[:, :ML_DV]
        den = r[:, ML_DV:ML_DV + 1]
        h_dir = num / jnp.maximum(jnp.abs(den), jnp.exp(-m_t))

        d_state = tot - cum_col + i_col
        m_new = jnp.maximum(tot + m_prev, jnp.max(d_state, axis=0, keepdims=True))
        w_s = jnp.exp(d_state - m_new)
        w_c = jnp.exp(tot + m_prev - m_new)
        vw = (v_aug.astype(jnp.float32) * w_s).astype(jnp.bfloat16)
        upd = lax.dot_general(km, vw, (((0,), (0,)), ((), ())), preferred_element_type=jnp.float32)
        ct_ref[hd] = w_c * ct + upd
        m_ref[hd:hd + 1, :] = jnp.broadcast_to(m_new, (1, LANES))

        sl = slice(hd * ML_DV, (hd + 1) * ML_DV)
        if finalize:
            h_sum = h_dir + hprev_ref[:, sl]
            y = _rms(h_sum, onorm_ref[hd:hd + 1, :], ML_DV)
            out_ref[:, sl] = (o_ref[:, sl].astype(jnp.float32) * y).astype(out_ref.dtype)
        else:
            out_ref[:, sl] = h_dir


def _mlstm(mlq, mlk, mlv, gates, hprev, mlo, onorm, *, batch, seq, chunk, reverse):
    t = mlq.shape[0]
    nc = seq // chunk
    finalize = hprev is not None
    if reverse:
        tmap = lambda b, c: (b * nc + (nc - 1 - c), 0)
    else:
        tmap = lambda b, c: (b * nc + c, 0)
    tok = lambda n: pl.BlockSpec((chunk, n), tmap)
    in_specs = [tok(256), tok(256), tok(ML_W), tok(LANES)]
    args = [mlq, mlk, mlv, gates]
    if finalize:
        in_specs += [tok(ML_W), tok(ML_W), pl.BlockSpec(onorm.shape, lambda b, c: (0, 0))]
        args += [hprev, mlo, onorm]
        out_dtype = jnp.bfloat16
    else:
        out_dtype = jnp.float32
    kern = functools.partial(_mlstm_kernel, reverse=reverse, finalize=finalize, chunk=chunk)
    return pl.pallas_call(
        kern, grid=(batch, nc), in_specs=in_specs, out_specs=tok(ML_W),
        out_shape=jax.ShapeDtypeStruct((t, ML_W), out_dtype),
        scratch_shapes=[pltpu.VMEM((ML_HEADS, LANES, 2 * LANES), jnp.float32),
                        pltpu.VMEM((8, LANES), jnp.float32)],
        compiler_params=pltpu.CompilerParams(dimension_semantics=("arbitrary", "arbitrary"),
                                             vmem_limit_bytes=VMEM_LIMIT_BYTES),
        name="mlstm_bwd" if reverse else "mlstm_fwd",
    )(*args)


def _mla_kernel(q_ref, k_ref, v_ref, o_ref, s_scr, acc_scr, *, tk, unroll):
    q = q_ref[...]
    tq = q.shape[0]
    nk = k_ref.shape[0] // tk

    def scores(j):
        start = pl.multiple_of(j * tk, tk)
        return lax.dot_general(q, k_ref[pl.ds(start, tk), :], (((1,), (1,)), ((), ())),
                               preferred_element_type=jnp.float32)

    def consume(slot, j, m_prev):
        start = pl.multiple_of(j * tk, tk)
        vt = v_ref[pl.ds(start, tk), :]
        m_new = jnp.maximum(m_prev, jnp.max(s_scr[slot], axis=1, keepdims=True))
        alpha = jnp.exp2(m_prev - m_new)
        p = jnp.exp2(s_scr[slot] - m_new).astype(jnp.bfloat16)
        acc_scr[...] = alpha * acc_scr[...] + jnp.dot(p, vt, preferred_element_type=jnp.float32)
        return m_new

    s_scr[0] = scores(0)
    acc_scr[...] = jnp.zeros_like(acc_scr)

    def body(jj, m):
        j = unroll * jj
        for u in range(unroll):
            s_scr[(u + 1) % 2] = scores(jnp.minimum(j + u + 1, nk - 1))
            m = consume(u % 2, j + u, m)
        return m

    lax.fori_loop(0, nk // unroll, body, jnp.full((tq, 1), NEG_INF, jnp.float32))
    acc = acc_scr[...]
    o_ref[...] = (acc[:, :MLA_V] / acc[:, MLA_V:MLA_V + 1]).astype(o_ref.dtype)


def _mla_attention(q, k, v, *, batch, seq, tq, tk, unroll):
    t = q.shape[1]
    nq = seq // tq
    assert unroll % 2 == 0 and (seq // tk) % unroll == 0
    return pl.pallas_call(
        functools.partial(_mla_kernel, tk=tk, unroll=unroll),
        grid=(batch, MLA_HEADS, nq),
        in_specs=[pl.BlockSpec((None, tq, MLA_QK_PAD), lambda b, h, i: (h, b * nq + i, 0)),
                  pl.BlockSpec((None, seq, MLA_QK_PAD), lambda b, h, i: (h, b, 0)),
                  pl.BlockSpec((None, seq, 2 * MLA_V), lambda b, h, i: (h, b, 0))],
        out_specs=pl.BlockSpec((tq, MLA_V), lambda b, h, i: (b * nq + i, h)),
        out_shape=jax.ShapeDtypeStruct((t, MLA_W), jnp.bfloat16),
        scratch_shapes=[pltpu.VMEM((2, tq, tk), jnp.float32),
                        pltpu.VMEM((tq, 2 * MLA_V), jnp.float32)],
        compiler_params=pltpu.CompilerParams(
            dimension_semantics=("arbitrary", "arbitrary", "arbitrary"),
            vmem_limit_bytes=VMEM_LIMIT_BYTES),
        name="mla_attn",
    )(q, k, v)


DIL_TILE = 256
DIL_REACH = max(w // 2 for w, _ in DIL_PAIRS)
DIL_NOFF = DIL_REACH // DIL_TILE


def _dil_bias_table():
    r = np.arange(DIL_TILE)[:, None]
    c = np.arange(DIL_TILE)[None, :]
    tiles = []
    for o in range(-DIL_NOFF, DIL_NOFF + 1):
        delta = o * DIL_TILE + c - r
        mult = np.zeros_like(delta)
        for window, dil in DIL_PAIRS:
            mult += ((delta % dil) == 0) & (np.abs(delta) <= window // 2)
        tiles.append(np.where(mult > 0, np.log(np.maximum(mult, 1)), NEG_INF))
    return np.stack(tiles).astype(np.float32)


def _dil_kernel(q_ref, k_ref, v_ref, bias_ref, o_ref, *, nq):
    i = pl.program_id(2)
    q = q_ref[...]
    m = l = acc = None
    order = [0] + [s * d for d in range(1, DIL_NOFF + 1) for s in (-1, 1)]
    for o in order:
        kj = i + o
        valid = (kj >= 0) & (kj < nq)
        start = pl.multiple_of(jnp.clip(kj, 0, nq - 1) * DIL_TILE, DIL_TILE)
        kt = k_ref[pl.ds(start, DIL_TILE), :]
        vt = v_ref[pl.ds(start, DIL_TILE), :]
        bias = bias_ref[o + DIL_NOFF]
        if o != 0:
            bias = jnp.where(valid, bias, NEG_INF)
        s = lax.dot_general(q, kt, (((1,), (1,)), ((), ())), preferred_element_type=jnp.float32) + bias
        m_cur = jnp.max(s, axis=1, keepdims=True)
        if m is None:
            m = m_cur
            p = jnp.exp(s - m)
            l = jnp.sum(p, axis=1, keepdims=True)
            acc = jnp.dot(p.astype(jnp.bfloat16), vt, preferred_element_type=jnp.float32)
        else:
            m_new = jnp.maximum(m, m_cur)
            alpha = jnp.exp(m - m_new)
            p = jnp.exp(s - m_new)
            l = alpha * l + jnp.sum(p, axis=1, keepdims=True)
            acc = alpha * acc + jnp.dot(p.astype(jnp.bfloat16), vt, preferred_element_type=jnp.float32)
            m = m_new
    o_ref[...] = (acc / l).astype(o_ref.dtype)


def _dil_attention(dq, dk, dv, bias, *, batch, seq):
    t = dq.shape[0]
    nq = seq // DIL_TILE
    return pl.pallas_call(
        functools.partial(_dil_kernel, nq=nq),
        grid=(batch, DIL_HEADS, nq),
        in_specs=[pl.BlockSpec((DIL_TILE, DIL_DH), lambda b, h, i: (b * nq + i, h)),
                  pl.BlockSpec((seq, DIL_DH), lambda b, h, i: (b, h)),
                  pl.BlockSpec((seq, DIL_DH), lambda b, h, i: (b, h)),
                  pl.BlockSpec(bias.shape, lambda b, h, i: (0, 0, 0))],
        out_specs=pl.BlockSpec((DIL_TILE, DIL_DH), lambda b, h, i: (b * nq + i, h)),
        out_shape=jax.ShapeDtypeStruct((t, DIL_W), jnp.bfloat16),
        compiler_params=pltpu.CompilerParams(
            dimension_semantics=("arbitrary", "arbitrary", "arbitrary"),
            vmem_limit_bytes=VMEM_LIMIT_BYTES),
        name="dil_attn",
    )(dq, dk, dv, bias)


def _outproj_kernel(x_ref, ya_ref, yb_ref, yc_ref, wa_ref, wb_ref, wc_ref, o_ref):
    y = (jnp.dot(ya_ref[...], wa_ref[...], preferred_element_type=jnp.float32)
         + jnp.dot(yb_ref[...], wb_ref[...], preferred_element_type=jnp.float32)
         + jnp.dot(yc_ref[...], wc_ref[...], preferred_element_type=jnp.float32))
    o_ref[...] = x_ref[...] + y


def _outproj(x2, ya, yb, yc, wa, wb, wc, *, tm):
    t, d = x2.shape
    tok = lambda n: pl.BlockSpec((tm, n), lambda i: (i, 0))
    return pl.pallas_call(
        _outproj_kernel, grid=(t // tm,),
        in_specs=[tok(d), tok(ML_W), tok(MLA_W), tok(DIL_W),
                  _const_spec(wa.shape), _const_spec(wb.shape), _const_spec(wc.shape)],
        out_specs=tok(d), out_shape=jax.ShapeDtypeStruct((t, d), jnp.float32),
        compiler_params=pltpu.CompilerParams(dimension_semantics=("arbitrary",),
                                             vmem_limit_bytes=VMEM_LIMIT_BYTES),
        name="outproj",
    )(x2, ya, yb, yc, wa, wb, wc)


def _ffn_kernel(x_ref, g_ref, w1_ref, w2_ref, o_ref, h_ref):
    j = pl.program_id(1)

    @pl.when(j == 0)
    def _():
        x = x_ref[...]
        h_ref[...] = _rms(x, g_ref[...], x.shape[-1]).astype(h_ref.dtype)
        o_ref[...] = x

    u = jnp.maximum(jnp.dot(h_ref[...], w1_ref[...], preferred_element_type=jnp.float32), 0.0)
    u = (u * u).astype(jnp.bfloat16)
    o_ref[...] += jnp.dot(u, w2_ref[...], preferred_element_type=jnp.float32)


def _ffn(x2, g, w1, w2, *, tm, tf):
    t, d = x2.shape
    dff = w1.shape[1]
    return pl.pallas_call(
        _ffn_kernel, grid=(t // tm, dff // tf),
        in_specs=[pl.BlockSpec((tm, d), lambda i, j: (i, 0)),
                  pl.BlockSpec(g.shape, lambda i, j: (0, 0)),
                  pl.BlockSpec((d, tf), lambda i, j: (0, j)),
                  pl.BlockSpec((tf, d), lambda i, j: (j, 0))],
        out_specs=pl.BlockSpec((tm, d), lambda i, j: (i, 0)),
        out_shape=jax.ShapeDtypeStruct((t, d), jnp.float32),
        scratch_shapes=[pltpu.VMEM((tm, d), jnp.bfloat16)],
        compiler_params=pltpu.CompilerParams(dimension_semantics=("arbitrary", "arbitrary"),
                                             vmem_limit_bytes=VMEM_LIMIT_BYTES),
        name="ffn",
    )(x2, g, w1, w2)


def _rope_tables(seq, rot_dim, fill_cos):
    pos = jnp.arange(seq, dtype=jnp.float32)
    inv_freq = ROPE_THETA ** (-jnp.arange(0, rot_dim, 2, dtype=jnp.float32) / rot_dim)
    ang = pos[:, None] * inv_freq[None, :]
    cos, sin = jnp.cos(ang), jnp.sin(ang)
    pad = LANES - rot_dim
    cos_f = jnp.concatenate([cos, cos, jnp.full((seq, pad), fill_cos, jnp.float32)], axis=1)
    sin_f = jnp.concatenate([-sin, sin, jnp.zeros((seq, pad), jnp.float32)], axis=1)
    return cos_f, sin_f


def _pad_cols(a, n):
    return jnp.pad(a, ((0, 0), (0, n - a.shape[1])))


def _pack_w_in(w_in):
    splits = np.cumsum((256, 256, ML_W, ML_W, 4 * ML_HEADS, MLA_Q_RANK, MLA_KV_RANK + MLA_ROPE,
                        DIL_W, DIL_W, DIL_W))[:-1].tolist()
    (wq, wk, wv, wo, wg, wcq, wckv, wdq, wdk, wdv) = jnp.split(w_in, splits, axis=1)
    packed = jnp.concatenate([wq, wk, wv, wo, _pad_cols(wg, LANES), wcq, _pad_cols(wckv, 256),
                              wdq, wdk, wdv], axis=1)
    return packed.astype(jnp.bfloat16)


def _layer(x2, p, tables, *, batch, seq):
    cosb, sinb, cosc, sinc, dil_bias = tables
    w = _pack_w_in(p['w_in'])
    wqb = p['mla_w_q_b'].reshape(MLA_Q_RANK, MLA_HEADS, MLA_NOPE + MLA_ROPE)
    wqb = jnp.pad(wqb, ((0, 0), (0, 0), (0, MLA_QK_PAD - MLA_NOPE - MLA_ROPE)))
    wqb = wqb.reshape(MLA_Q_RANK, MLA_HEADS * MLA_QK_PAD).astype(jnp.bfloat16)
    wkvb = p['mla_w_kv_b'].astype(jnp.bfloat16)
    gbias = jnp.concatenate([p['ml_i_bias'][0], p['ml_f_bias'][0], p['ml_i_bias'][1], p['ml_f_bias'][1]])
    gbias = _pad_cols(gbias[None, :], LANES)
    qh, kh = p['mla_q_head_norm'], p['mla_k_head_norm']
    hg = jnp.stack([qh[:MLA_NOPE], jnp.pad(qh[MLA_NOPE:], (0, LANES - MLA_ROPE)),
                    kh[:MLA_NOPE], jnp.pad(kh[MLA_NOPE:], (0, LANES - MLA_ROPE)),
                    p['dil_q_norm'], p['dil_k_norm'],
                    jnp.zeros((LANES,), jnp.float32), jnp.zeros((LANES,), jnp.float32)])

    (mlq, mlk, mlv, mlo, gates, q, k, v, dq, dk, dv) = _inproj(
        x2, p['norm_mix'][None, :], w, wqb, wkvb, gbias, p['mla_q_norm'][None, :],
        p['mla_kv_norm'][None, :], hg, cosb, sinb, cosc, sinc, seq=seq, tm=min(512, seq))

    chunk = min(256, seq)
    h_bwd = _mlstm(mlq, mlk, mlv, gates, None, None, None, batch=batch, seq=seq, chunk=chunk, reverse=True)
    ya = _mlstm(mlq, mlk, mlv, gates, h_bwd, mlo, p['ml_out_norm'], batch=batch, seq=seq, chunk=chunk,
                reverse=False)
    yb = _mla_attention(q, k, v, batch=batch, seq=seq, tq=min(512, seq), tk=min(512, seq // 8), unroll=8)
    yc = _dil_attention(dq, dk, dv, dil_bias, batch=batch, seq=seq)

    w_out = p['w_out'].astype(jnp.bfloat16)
    x2 = _outproj(x2, ya, yb, yc, w_out[:ML_W], w_out[ML_W:ML_W + MLA_W], w_out[ML_W + MLA_W:],
                  tm=min(512, seq))
    return _ffn(x2, p['norm_ff'][None, :], p['w_ff1'].astype(jnp.bfloat16), p['w_ff2'].astype(jnp.bfloat16),
                tm=min(512, seq), tf=1024)


def kernel(x, norm_mix, w_in, ml_i_bias, ml_f_bias, ml_out_norm, mla_q_norm, mla_w_q_b, mla_kv_norm,
           mla_w_kv_b, mla_q_head_norm, mla_k_head_norm, dil_q_norm, dil_k_norm, w_out, norm_ff,
           w_ff1, w_ff2):
    batch, seq, d = x.shape
    assert seq % DIL_TILE == 0
    params = dict(norm_mix=norm_mix, w_in=w_in, ml_i_bias=ml_i_bias, ml_f_bias=ml_f_bias,
                  ml_out_norm=ml_out_norm, mla_q_norm=mla_q_norm, mla_w_q_b=mla_w_q_b,
                  mla_kv_norm=mla_kv_norm, mla_w_kv_b=mla_w_kv_b, mla_q_head_norm=mla_q_head_norm,
                  mla_k_head_norm=mla_k_head_norm, dil_q_norm=dil_q_norm, dil_k_norm=dil_k_norm,
                  w_out=w_out, norm_ff=norm_ff, w_ff1=w_ff1, w_ff2=w_ff2)
    cosb, sinb = _rope_tables(seq, MLA_ROPE, 0.0)
    cosc, sinc = _rope_tables(seq, DIL_ROT, 1.0)
    tables = (cosb, sinb, cosc, sinc, jnp.asarray(_dil_bias_table()))
    x2 = x.reshape(batch * seq, d)
    for layer in range(norm_mix.shape[0]):
        x2 = _layer(x2, {name: val[layer] for name, val in params.items()}, tables, batch=batch, seq=seq)
    return x2.reshape(batch, seq, d)
```

```python
import functools
import math

import numpy as np
import jax
import jax.numpy as jnp
from jax import lax
from jax.experimental import pallas as pl
from jax.experimental.pallas import tpu as pltpu

EPS = 1e-6
NEG_INF = -1e30
ROPE_THETA = 500000.0

ML_HEADS = 4
ML_DK = 64
ML_DV = 128
ML_W = ML_HEADS * ML_DV

MLA_HEADS = 6
MLA_Q_RANK = 384
MLA_KV_RANK = 128
MLA_NOPE = 128
MLA_ROPE = 64
MLA_V = 128
MLA_W = MLA_HEADS * MLA_V
MLA_QK_PAD = 256

DIL_HEADS = 6
DIL_DH = 128
DIL_ROT = DIL_DH // 4
DIL_PAIRS = ((128, 1), (512, 4), (2048, 16))
DIL_W = DIL_HEADS * DIL_DH

LANES = 128
VMEM_LIMIT_BYTES = 60 * 1024 * 1024

_C_MLQ = 0
_C_MLK = 256
_C_MLV = 512
_C_MLO = 1024
_C_GATE = 1536
_C_CQ = 1664
_C_CKV = 2048
_C_DQ = 2304
_C_DK = 3072
_C_DV = 3840
_C_END = 4608


def _rms(x, gain, n):
    ms = jnp.sum(x * x, axis=-1, keepdims=True) * (1.0 / n)
    return x * lax.rsqrt(ms + EPS) * gain


def _rope_tile(x, cos_f, sin_f, half):
    lane = lax.broadcasted_iota(jnp.int32, x.shape, 1)
    sw = jnp.where(lane < half, pltpu.roll(x, LANES - half, 1), pltpu.roll(x, half, 1))
    return x * cos_f + sw * sin_f


def _log_sigmoid(x):
    return jnp.minimum(x, 0.0) - jnp.log1p(jnp.exp(-jnp.abs(x)))


def _inproj_kernel(x_ref, gmix_ref, w_ref, wqb_ref, wkvb_ref, gbias_ref, qlat_ref, kvlat_ref,
                   hg_ref, cosb_ref, sinb_ref, cosc_ref, sinc_ref,
                   mlq_ref, mlk_ref, mlv_ref, mlo_ref, gate_ref, q_ref, k_ref, v_ref,
                   dq_ref, dk_ref, dv_ref, *, mla_scale, dil_scale):
    x = x_ref[...]
    h = _rms(x, gmix_ref[...], x.shape[-1]).astype(jnp.bfloat16)

    def proj(c0, c1):
        return jnp.dot(h, w_ref[:, c0:c1], preferred_element_type=jnp.float32)

    cos_b = cosb_ref[...]
    sin_b = sinb_ref[...]
    cos_c = cosc_ref[...]
    sin_c = sinc_ref[...]
    qg_n = hg_ref[0:1, :]
    qg_r = hg_ref[1:2, :]
    kg_n = hg_ref[2:3, :]
    kg_r = hg_ref[3:4, :]
    dqg = hg_ref[4:5, :]
    dkg = hg_ref[5:6, :]

    cq = _rms(proj(_C_CQ, _C_CKV), qlat_ref[...], MLA_Q_RANK).astype(jnp.bfloat16)
    zc = proj(_C_CKV, _C_DQ)
    ckv = _rms(zc[:, :MLA_KV_RANK], kvlat_ref[...], MLA_KV_RANK).astype(jnp.bfloat16)
    zdq = proj(_C_DQ, _C_DK)
    zdk = proj(_C_DK, _C_DV)
    zq = jnp.dot(cq, wqb_ref[...], preferred_element_type=jnp.float32)
    zkv = jnp.dot(ckv, wkvb_ref[...], preferred_element_type=jnp.float32)

    k_rope = _rope_tile(_rms(zc[:, MLA_KV_RANK:], kg_r, MLA_ROPE), cos_b, sin_b, MLA_ROPE // 2)
    k_rope = k_rope.astype(jnp.bfloat16)
    ones_col = jnp.where(lax.broadcasted_iota(jnp.int32, k_rope.shape, 1) == 0, 1.0, 0.0)
    ones_col = ones_col.astype(jnp.bfloat16)
    for hd in range(MLA_HEADS):
        c = hd * MLA_QK_PAD
        q_nope = _rms(zq[:, c:c + MLA_NOPE], qg_n, MLA_NOPE)
        q_rope = _rope_tile(_rms(zq[:, c + MLA_NOPE:c + MLA_QK_PAD], qg_r, MLA_ROPE),
                            cos_b, sin_b, MLA_ROPE // 2)
        q_ref[hd, :, 0:MLA_NOPE] = (q_nope * mla_scale).astype(jnp.bfloat16)
        q_ref[hd, :, MLA_NOPE:MLA_QK_PAD] = (q_rope * mla_scale).astype(jnp.bfloat16)
        k_nope = _rms(zkv[:, c:c + MLA_NOPE], kg_n, MLA_NOPE)
        k_ref[hd, :, 0:MLA_NOPE] = k_nope.astype(jnp.bfloat16)
        k_ref[hd, :, MLA_NOPE:MLA_QK_PAD] = k_rope
        v_ref[hd, :, 0:MLA_V] = zkv[:, c + MLA_NOPE:c + MLA_QK_PAD].astype(jnp.bfloat16)
        v_ref[hd, :, MLA_V:2 * MLA_V] = ones_col

    mlo_ref[...] = jax.nn.sigmoid(proj(_C_MLO, _C_GATE)).astype(jnp.bfloat16)
    g = proj(_C_GATE, _C_CQ) + gbias_ref[...]
    lane = lax.broadcasted_iota(jnp.int32, g.shape, 1)
    is_forget = (lane % 8) >= 4
    gate_ref[...] = jnp.where(is_forget, _log_sigmoid(g), g)

    for hd in range(DIL_HEADS):
        c = hd * DIL_DH
        qh = _rope_tile(_rms(zdq[:, c:c + DIL_DH], dqg, DIL_DH), cos_c, sin_c, DIL_ROT // 2)
        kh = _rope_tile(_rms(zdk[:, c:c + DIL_DH], dkg, DIL_DH), cos_c, sin_c, DIL_ROT // 2)
        dq_ref[:, c:c + DIL_DH] = (qh * dil_scale).astype(jnp.bfloat16)
        dk_ref[:, c:c + DIL_DH] = kh.astype(jnp.bfloat16)
    zdv = proj(_C_DV, _C_END).astype(jnp.bfloat16)
    for hd in range(DIL_HEADS):
        dv_ref[:, 2 * hd * DIL_DH:(2 * hd + 1) * DIL_DH] = zdv[:, hd * DIL_DH:(hd + 1) * DIL_DH]
        dv_ref[:, (2 * hd + 1) * DIL_DH:(2 * hd + 2) * DIL_DH] = ones_col

    mlq_ref[...] = (proj(_C_MLQ, _C_MLK) * (ML_DK ** -0.5)).astype(jnp.bfloat16)
    mlk_ref[...] = proj(_C_MLK, _C_MLV).astype(jnp.bfloat16)
    mlv_ref[...] = proj(_C_MLV, _C_MLO).astype(jnp.bfloat16)


def _const_spec(shape):
    n = len(shape)
    return pl.BlockSpec(shape, lambda *_: (0,) * n, pipeline_mode=pl.Buffered(1))


def _inproj(x2, gmix, w, wqb, wkvb, gbias, qlat, kvlat, hg, cosb, sinb, cosc, sinc, *, seq, tm):
    t, d = x2.shape
    nt = t // tm
    ns = seq // tm
    tok = lambda n: pl.BlockSpec((tm, n), lambda i: (i, 0))
    pos = pl.BlockSpec((tm, LANES), lambda i: (i % ns, 0))
    headed = lambda n: pl.BlockSpec((MLA_HEADS, tm, n), lambda i: (0, i, 0))
    bf = jnp.bfloat16
    out_shape = (
        jax.ShapeDtypeStruct((t, 256), bf), jax.ShapeDtypeStruct((t, 256), bf),
        jax.ShapeDtypeStruct((t, ML_W), bf), jax.ShapeDtypeStruct((t, ML_W), bf),
        jax.ShapeDtypeStruct((t, LANES), jnp.float32),
        jax.ShapeDtypeStruct((MLA_HEADS, t, MLA_QK_PAD), bf),
        jax.ShapeDtypeStruct((MLA_HEADS, t, MLA_QK_PAD), bf),
        jax.ShapeDtypeStruct((MLA_HEADS, t, 2 * MLA_V), bf),
        jax.ShapeDtypeStruct((t, DIL_W), bf), jax.ShapeDtypeStruct((t, DIL_W), bf),
        jax.ShapeDtypeStruct((t, 2 * DIL_W), bf),
    )
    out_specs = (tok(256), tok(256), tok(ML_W), tok(ML_W), tok(LANES),
                 headed(MLA_QK_PAD), headed(MLA_QK_PAD), headed(2 * MLA_V),
                 tok(DIL_W), tok(DIL_W), tok(2 * DIL_W))
    in_specs = [tok(d), _const_spec(gmix.shape), _const_spec(w.shape), _const_spec(wqb.shape),
                _const_spec(wkvb.shape), _const_spec(gbias.shape), _const_spec(qlat.shape),
                _const_spec(kvlat.shape), _const_spec(hg.shape), pos, pos, pos, pos]
    kern = functools.partial(_inproj_kernel,
                             mla_scale=(MLA_NOPE + MLA_ROPE) ** -0.5 * math.log2(math.e),
                             dil_scale=DIL_DH ** -0.5 * math.log2(math.e))
    return pl.pallas_call(
        kern, grid=(nt,), in_specs=in_specs, out_specs=out_specs, out_shape=out_shape,
        compiler_params=pltpu.CompilerParams(dimension_semantics=("arbitrary",),
                                             vmem_limit_bytes=VMEM_LIMIT_BYTES),
        name="inproj",
    )(x2, gmix, w, wqb, wkvb, gbias, qlat, kvlat, hg, cosb, sinb, cosc, sinc)


def _mlstm_kernel(*refs, reverse, finalize, chunk):
    if finalize:
        (q_ref, k_ref, v_ref, gate_ref, hprev_ref, o_ref, onorm_ref, out_ref, ct_ref, m_ref) = refs
    else:
        (q_ref, k_ref, v_ref, gate_ref, out_ref, ct_ref, m_ref) = refs
    L = chunk
    c = pl.program_id(1)

    @pl.when(c == 0)
    def _():
        ct_ref[...] = jnp.zeros_like(ct_ref)
        m_ref[...] = jnp.zeros_like(m_ref)

    row = lax.broadcasted_iota(jnp.int32, (L, L), 0)
    col = lax.broadcasted_iota(jnp.int32, (L, L), 1)
    if reverse:
        causal = col >= row
    else:
        causal = col <= row
    tri = causal.astype(jnp.float32)

    gates = gate_ref[...]
    gates_t = gates.T
    d0 = 8 if reverse else 0
    cum_col_all = jnp.dot(tri, gates, preferred_element_type=jnp.float32,
                          precision=lax.Precision.HIGHEST)
    cum_row_all = lax.dot_general(gates_t, tri, (((1,), (1,)), ((), ())),
                                  preferred_element_type=jnp.float32,
                                  precision=lax.Precision.HIGHEST)
    tot_all = jnp.sum(gates, axis=0, keepdims=True)

    lane = lax.broadcasted_iota(jnp.int32, (L, LANES), 1)
    ones_col = jnp.where(lane == 0, 1.0, 0.0).astype(jnp.bfloat16)

    for hd in range(ML_HEADS):
        gi = d0 + hd
        gf = d0 + 4 + hd
        i_col = gates[:, gi:gi + 1]
        i_row = gates_t[gi:gi + 1, :]
        cum_col = cum_col_all[:, gf:gf + 1]
        cum_row = cum_row_all[gf:gf + 1, :]
        tot = tot_all[:, gf:gf + 1]
        m_prev = m_ref[hd:hd + 1, 0:1]

        pair = hd // 2
        in_head = (lane >= (hd % 2) * ML_DK) & (lane < (hd % 2 + 1) * ML_DK)
        qp = q_ref[:, pair * LANES:(pair + 1) * LANES]
        kp = k_ref[:, pair * LANES:(pair + 1) * LANES]
        qm = jnp.where(in_head, qp, jnp.zeros_like(qp))
        km = jnp.where(in_head, kp, jnp.zeros_like(kp))
        v_h = v_ref[:, hd * ML_DV:(hd + 1) * ML_DV]
        v_aug = jnp.concatenate([v_h, ones_col], axis=1)

        d_mat = jnp.where(causal, cum_col + (i_row - cum_row), NEG_INF)
        d_inter = cum_col + m_prev
        m_t = jnp.maximum(d_inter, jnp.max(d_mat, axis=1, keepdims=True))
        w_intra = jnp.exp(d_mat - m_t)
        w_inter = jnp.exp(d_inter - m_t)
        s_raw = lax.dot_general(qm, kp, (((1,), (1,)), ((), ())), preferred_element_type=jnp.float32)
        sw = (s_raw * w_intra).astype(jnp.bfloat16)
        ct = ct_ref[hd]
        r = (jnp.dot(sw, v_aug, preferred_element_type=jnp.float32)
             + w_inter * jnp.dot(qm, ct.astype(jnp.bfloat16), preferred_element_type=jnp.float32))
        num = r[:, :ML_DV]
        den = r[:, ML_DV:ML_DV + 1]
        h_dir = num / jnp.maximum(jnp.abs(den), jnp.exp(-m_t))

        d_state = tot - cum_col + i_col
        m_new = jnp.maximum(tot + m_prev, jnp.max(d_state, axis=0, keepdims=True))
        w_s = jnp.exp(d_state - m_new)
        w_c = jnp.exp(tot + m_prev - m_new)
        vw = (v_aug.astype(jnp.float32) * w_s).astype(jnp.bfloat16)
        upd = lax.dot_general(km, vw, (((0,), (0,)), ((), ())), preferred_element_type=jnp.float32)
        ct_ref[hd] = w_c * ct + upd
        m_ref[hd:hd + 1, :] = jnp.broadcast_to(m_new, (1, LANES))

        sl = slice(hd * ML_DV, (hd + 1) * ML_DV)
        if finalize:
            h_sum = h_dir + hprev_ref[:, sl]
            y = _rms(h_sum, onorm_ref[hd:hd + 1, :], ML_DV)
            out_ref[:, sl] = (o_ref[:, sl].astype(jnp.float32) * y).astype(out_ref.dtype)
        else:
            out_ref[:, sl] = h_dir


def _mlstm(mlq, mlk, mlv, gates, hprev, mlo, onorm, *, batch, seq, chunk, reverse):
    t = mlq.shape[0]
    nc = seq // chunk
    finalize = hprev is not None
    if reverse:
        tmap = lambda b, c: (b * nc + (nc - 1 - c), 0)
    else:
        tmap = lambda b, c: (b * nc + c, 0)
    tok = lambda n: pl.BlockSpec((chunk, n), tmap)
    in_specs = [tok(256), tok(256), tok(ML_W), tok(LANES)]
    args = [mlq, mlk, mlv, gates]
    if finalize:
        in_specs += [tok(ML_W), tok(ML_W), pl.BlockSpec(onorm.shape, lambda b, c: (0, 0))]
        args += [hprev, mlo, onorm]
        out_dtype = jnp.bfloat16
    else:
        out_dtype = jnp.float32
    kern = functools.partial(_mlstm_kernel, reverse=reverse, finalize=finalize, chunk=chunk)
    return pl.pallas_call(
        kern, grid=(batch, nc), in_specs=in_specs, out_specs=tok(ML_W),
        out_shape=jax.ShapeDtypeStruct((t, ML_W), out_dtype),
        scratch_shapes=[pltpu.VMEM((ML_HEADS, LANES, 2 * LANES), jnp.float32),
                        pltpu.VMEM((8, LANES), jnp.float32)],
        compiler_params=pltpu.CompilerParams(dimension_semantics=("arbitrary", "arbitrary"),
                                             vmem_limit_bytes=VMEM_LIMIT_BYTES),
        name="mlstm_bwd" if reverse else "mlstm_fwd",
    )(*args)


def _mla_kernel(q_ref, k_ref, v_ref, o_ref, s_scr, acc_scr, *, tk, unroll):
    q = q_ref[...]
    tq = q.shape[0]
    nk = k_ref.shape[0] // tk

    def scores(j):
        start = pl.multiple_of(j * tk, tk)
        return lax.dot_general(q, k_ref[pl.ds(start, tk), :], (((1,), (1,)), ((), ())),
                               preferred_element_type=jnp.float32)

    def consume(slot, j, m_prev):
        start = pl.multiple_of(j * tk, tk)
        vt = v_ref[pl.ds(start, tk), :]
        m_new = jnp.maximum(m_prev, jnp.max(s_scr[slot], axis=1, keepdims=True))
        alpha = jnp.exp2(m_prev - m_new)
        p = jnp.exp2(s_scr[slot] - m_new).astype(jnp.bfloat16)
        acc_scr[...] = alpha * acc_scr[...] + jnp.dot(p, vt, preferred_element_type=jnp.float32)
        return m_new

    s_scr[0] = scores(0)
    acc_scr[...] = jnp.zeros_like(acc_scr)

    def body(jj, m):
        j = unroll * jj
        for u in range(unroll):
            s_scr[(u + 1) % 2] = scores(jnp.minimum(j + u + 1, nk - 1))
            m = consume(u % 2, j + u, m)
        return m

    lax.fori_loop(0, nk // unroll, body, jnp.full((tq, 1), NEG_INF, jnp.float32))
    acc = acc_scr[...]
    o_ref[...] = (acc[:, :MLA_V] / acc[:, MLA_V:MLA_V + 1]).astype(o_ref.dtype)


def _mla_attention(q, k, v, *, batch, seq, tq, tk, unroll):
    t = q.shape[1]
    nq = seq // tq
    assert unroll % 2 == 0 and (seq // tk) % unroll == 0
    return pl.pallas_call(
        functools.partial(_mla_kernel, tk=tk, unroll=unroll),
        grid=(batch, MLA_HEADS, nq),
        in_specs=[pl.BlockSpec((None, tq, MLA_QK_PAD), lambda b, h, i: (h, b * nq + i, 0)),
                  pl.BlockSpec((None, seq, MLA_QK_PAD), lambda b, h, i: (h, b, 0)),
                  pl.BlockSpec((None, seq, 2 * MLA_V), lambda b, h, i: (h, b, 0))],
        out_specs=pl.BlockSpec((tq, MLA_V), lambda b, h, i: (b * nq + i, h)),
        out_shape=jax.ShapeDtypeStruct((t, MLA_W), jnp.bfloat16),
        scratch_shapes=[pltpu.VMEM((2, tq, tk), jnp.float32),
                        pltpu.VMEM((tq, 2 * MLA_V), jnp.float32)],
        compiler_params=pltpu.CompilerParams(
            dimension_semantics=("arbitrary", "arbitrary", "arbitrary"),
            vmem_limit_bytes=VMEM_LIMIT_BYTES),
        name="mla_attn",
    )(q, k, v)


DIL_TILE = 256
DIL_REACH = max(w // 2 for w, _ in DIL_PAIRS)
DIL_NOFF = DIL_REACH // DIL_TILE
DIL_NWIN = 2 * DIL_NOFF + 1
DIL_SUB = 2


def _dil_bias_table():
    r = np.arange(DIL_TILE)[:, None]
    c = np.arange(DIL_TILE)[None, :]
    tiles = []
    for o in range(-2 * DIL_NOFF, 2 * DIL_NOFF + 1):
        delta = o * DIL_TILE + c - r
        mult = np.zeros_like(delta)
        for window, dil in DIL_PAIRS:
            mult += ((delta % dil) == 0) & (np.abs(delta) <= window // 2)
        tiles.append(np.where(mult > 0, np.log2(np.maximum(mult, 1)), NEG_INF))
    return np.stack(tiles).astype(np.float32)


def _dil_kernel(q_ref, k_ref, v_ref, bias_ref, o_ref, s_scr, *, nq, nwin):
    i = pl.program_id(2)
    nt = (((1,), (1,)), ((), ()))
    subs = range(DIL_SUB)
    qi = [i * DIL_SUB + sub for sub in subs]
    rows = [slice(sub * DIL_TILE, (sub + 1) * DIL_TILE) for sub in subs]
    q = [q_ref[r, :] for r in rows]
    w0 = [jnp.clip(t - DIL_NOFF, 0, nq - nwin) for t in qi]
    m_elem = [None] * DIL_SUB
    for idx in range(nwin):
        for sub in subs:
            start = pl.multiple_of((w0[sub] + idx) * DIL_TILE, DIL_TILE)
            bias = bias_ref[w0[sub] + idx - qi[sub] + 2 * DIL_NOFF]
            s = lax.dot_general(q[sub], k_ref[pl.ds(start, DIL_TILE), :], nt,
                                preferred_element_type=jnp.float32) + bias
            s_scr[sub, :, idx * DIL_TILE:(idx + 1) * DIL_TILE] = s
            m_elem[sub] = s if idx == 0 else jnp.maximum(m_elem[sub], s)
    for sub in subs:
        m = jnp.max(m_elem[sub], axis=1, keepdims=True)
        p = jnp.exp2(s_scr[sub] - m).astype(jnp.bfloat16)
        wstart = pl.multiple_of(w0[sub] * DIL_TILE, DIL_TILE)
        acc = jnp.dot(p, v_ref[pl.ds(wstart, nwin * DIL_TILE), :], preferred_element_type=jnp.float32)
        o_ref[rows[sub], :] = (acc[:, :DIL_DH] / acc[:, DIL_DH:DIL_DH + 1]).astype(o_ref.dtype)


def _dil_attention(dq, dk, dv, bias, *, batch, seq):
    t = dq.shape[0]
    nq = seq // DIL_TILE
    nwin = min(DIL_NWIN, nq)
    tq = DIL_SUB * DIL_TILE
    nstep = seq // tq
    return pl.pallas_call(
        functools.partial(_dil_kernel, nq=nq, nwin=nwin),
        grid=(batch, DIL_HEADS, nstep),
        in_specs=[pl.BlockSpec((tq, DIL_DH), lambda b, h, i: (b * nstep + i, h)),
                  pl.BlockSpec((seq, DIL_DH), lambda b, h, i: (b, h)),
                  pl.BlockSpec((seq, 2 * DIL_DH), lambda b, h, i: (b, h)),
                  _const_spec(bias.shape)],
        out_specs=pl.BlockSpec((tq, DIL_DH), lambda b, h, i: (b * nstep + i, h)),
        out_shape=jax.ShapeDtypeStruct((t, DIL_W), jnp.bfloat16),
        scratch_shapes=[pltpu.VMEM((DIL_SUB, DIL_TILE, nwin * DIL_TILE), jnp.float32)],
        compiler_params=pltpu.CompilerParams(
            dimension_semantics=("arbitrary", "arbitrary", "arbitrary"),
            vmem_limit_bytes=VMEM_LIMIT_BYTES),
        name="dil_attn",
    )(dq, dk, dv, bias)


def _outproj_kernel(x_ref, ya_ref, yb_ref, yc_ref, wa_ref, wb_ref, wc_ref, o_ref):
    y = (jnp.dot(ya_ref[...], wa_ref[...], preferred_element_type=jnp.float32)
         + jnp.dot(yb_ref[...], wb_ref[...], preferred_element_type=jnp.float32)
         + jnp.dot(yc_ref[...], wc_ref[...], preferred_element_type=jnp.float32))
    o_ref[...] = x_ref[...] + y


def _outproj(x2, ya, yb, yc, wa, wb, wc, *, tm):
    t, d = x2.shape
    tok = lambda n: pl.BlockSpec((tm, n), lambda i: (i, 0))
    return pl.pallas_call(
        _outproj_kernel, grid=(t // tm,),
        in_specs=[tok(d), tok(ML_W), tok(MLA_W), tok(DIL_W),
                  _const_spec(wa.shape), _const_spec(wb.shape), _const_spec(wc.shape)],
        out_specs=tok(d), out_shape=jax.ShapeDtypeStruct((t, d), jnp.float32),
        compiler_params=pltpu.CompilerParams(dimension_semantics=("arbitrary",),
                                             vmem_limit_bytes=VMEM_LIMIT_BYTES),
        name="outproj",
    )(x2, ya, yb, yc, wa, wb, wc)


def _ffn_kernel(x_ref, g_ref, w1_ref, w2_ref, o_ref, h_ref):
    j = pl.program_id(1)

    @pl.when(j == 0)
    def _():
        x = x_ref[...]
        h_ref[...] = _rms(x, g_ref[...], x.shape[-1]).astype(h_ref.dtype)
        o_ref[...] = x

    u = jnp.maximum(jnp.dot(h_ref[...], w1_ref[...], preferred_element_type=jnp.float32), 0.0)
    u = (u * u).astype(jnp.bfloat16)
    o_ref[...] += jnp.dot(u, w2_ref[...], preferred_element_type=jnp.float32)


def _ffn(x2, g, w1, w2, *, tm, tf):
    t, d = x2.shape
    dff = w1.shape[1]
    return pl.pallas_call(
        _ffn_kernel, grid=(t // tm, dff // tf),
        in_specs=[pl.BlockSpec((tm, d), lambda i, j: (i, 0)),
                  pl.BlockSpec(g.shape, lambda i, j: (0, 0)),
                  pl.BlockSpec((d, tf), lambda i, j: (0, j)),
                  pl.BlockSpec((tf, d), lambda i, j: (j, 0))],
        out_specs=pl.BlockSpec((tm, d), lambda i, j: (i, 0)),
        out_shape=jax.ShapeDtypeStruct((t, d), jnp.float32),
        scratch_shapes=[pltpu.VMEM((tm, d), jnp.bfloat16)],
        compiler_params=pltpu.CompilerParams(dimension_semantics=("arbitrary", "arbitrary"),
                                             vmem_limit_bytes=VMEM_LIMIT_BYTES),
        name="ffn",
    )(x2, g, w1, w2)


def _rope_tables(seq, rot_dim, fill_cos):
    pos = jnp.arange(seq, dtype=jnp.float32)
    inv_freq = ROPE_THETA ** (-jnp.arange(0, rot_dim, 2, dtype=jnp.float32) / rot_dim)
    ang = pos[:, None] * inv_freq[None, :]
    cos, sin = jnp.cos(ang), jnp.sin(ang)
    pad = LANES - rot_dim
    cos_f = jnp.concatenate([cos, cos, jnp.full((seq, pad), fill_cos, jnp.float32)], axis=1)
    sin_f = jnp.concatenate([-sin, sin, jnp.zeros((seq, pad), jnp.float32)], axis=1)
    return cos_f, sin_f


def _pad_cols(a, n):
    return jnp.pad(a, ((0, 0), (0, n - a.shape[1])))


def _pack_w_in(w_in):
    splits = np.cumsum((256, 256, ML_W, ML_W, 4 * ML_HEADS, MLA_Q_RANK, MLA_KV_RANK + MLA_ROPE,
                        DIL_W, DIL_W, DIL_W))[:-1].tolist()
    (wq, wk, wv, wo, wg, wcq, wckv, wdq, wdk, wdv) = jnp.split(w_in, splits, axis=1)
    packed = jnp.concatenate([wq, wk, wv, wo, _pad_cols(wg, LANES), wcq, _pad_cols(wckv, 256),
                              wdq, wdk, wdv], axis=1)
    return packed.astype(jnp.bfloat16)


def _layer(x2, p, tables, *, batch, seq):
    cosb, sinb, cosc, sinc, dil_bias = tables
    w = _pack_w_in(p['w_in'])
    wqb = p['mla_w_q_b'].reshape(MLA_Q_RANK, MLA_HEADS, MLA_NOPE + MLA_ROPE)
    wqb = jnp.pad(wqb, ((0, 0), (0, 0), (0, MLA_QK_PAD - MLA_NOPE - MLA_ROPE)))
    wqb = wqb.reshape(MLA_Q_RANK, MLA_HEADS * MLA_QK_PAD).astype(jnp.bfloat16)
    wkvb = p['mla_w_kv_b'].astype(jnp.bfloat16)
    gbias = jnp.concatenate([p['ml_i_bias'][0], p['ml_f_bias'][0], p['ml_i_bias'][1], p['ml_f_bias'][1]])
    gbias = _pad_cols(gbias[None, :], LANES)
    qh, kh = p['mla_q_head_norm'], p['mla_k_head_norm']
    hg = jnp.stack([qh[:MLA_NOPE], jnp.pad(qh[MLA_NOPE:], (0, LANES - MLA_ROPE)),
                    kh[:MLA_NOPE], jnp.pad(kh[MLA_NOPE:], (0, LANES - MLA_ROPE)),
                    p['dil_q_norm'], p['dil_k_norm'],
                    jnp.zeros((LANES,), jnp.float32), jnp.zeros((LANES,), jnp.float32)])

    (mlq, mlk, mlv, mlo, gates, q, k, v, dq, dk, dv) = _inproj(
        x2, p['norm_mix'][None, :], w, wqb, wkvb, gbias, p['mla_q_norm'][None, :],
        p['mla_kv_norm'][None, :], hg, cosb, sinb, cosc, sinc, seq=seq, tm=min(512, seq))

    chunk = min(256, seq)
    h_bwd = _mlstm(mlq, mlk, mlv, gates, None, None, None, batch=batch, seq=seq, chunk=chunk, reverse=True)
    ya = _mlstm(mlq, mlk, mlv, gates, h_bwd, mlo, p['ml_out_norm'], batch=batch, seq=seq, chunk=chunk,
                reverse=False)
    yb = _mla_attention(q, k, v, batch=batch, seq=seq, tq=min(512, seq), tk=min(512, seq // 8), unroll=8)
    yc = _dil_attention(dq, dk, dv, dil_bias, batch=batch, seq=seq)

    w_out = p['w_out'].astype(jnp.bfloat16)
    x2 = _outproj(x2, ya, yb, yc, w_out[:ML_W], w_out[ML_W:ML_W + MLA_W], w_out[ML_W + MLA_W:],
                  tm=min(512, seq))
    return _ffn(x2, p['norm_ff'][None, :], p['w_ff1'].astype(jnp.bfloat16), p['w_ff2'].astype(jnp.bfloat16),
                tm=min(512, seq), tf=1024)


def kernel(x, norm_mix, w_in, ml_i_bias, ml_f_bias, ml_out_norm, mla_q_norm, mla_w_q_b, mla_kv_norm,
           mla_w_kv_b, mla_q_head_norm, mla_k_head_norm, dil_q_norm, dil_k_norm, w_out, norm_ff,
           w_ff1, w_ff2):
    batch, seq, d = x.shape
    assert seq % (DIL_SUB * DIL_TILE) == 0
    params = dict(norm_mix=norm_mix, w_in=w_in, ml_i_bias=ml_i_bias, ml_f_bias=ml_f_bias,
                  ml_out_norm=ml_out_norm, mla_q_norm=mla_q_norm, mla_w_q_b=mla_w_q_b,
                  mla_kv_norm=mla_kv_norm, mla_w_kv_b=mla_w_kv_b, mla_q_head_norm=mla_q_head_norm,
                  mla_k_head_norm=mla_k_head_norm, dil_q_norm=dil_q_norm, dil_k_norm=dil_k_norm,
                  w_out=w_out, norm_ff=norm_ff, w_ff1=w_ff1, w_ff2=w_ff2)
    cosb, sinb = _rope_tables(seq, MLA_ROPE, 0.0)
    cosc, sinc = _rope_tables(seq, DIL_ROT, 1.0)
    tables = (cosb, sinb, cosc, sinc, jnp.asarray(_dil_bias_table()))
    x2 = x.reshape(batch * seq, d)
    for layer in range(norm_mix.shape[0]):
        x2 = _layer(x2, {name: val[layer] for name, val in params.items()}, tables, batch=batch, seq=seq)
    return x2.reshape(batch, seq, d)
```

```python
import functools
import math

import numpy as np
import jax
import jax.numpy as jnp
from jax import lax
from jax.experimental import pallas as pl
from jax.experimental.pallas import tpu as pltpu

EPS = 1e-6
NEG_INF = -1e30
ROPE_THETA = 500000.0

ML_HEADS = 4
ML_DK = 64
ML_DV = 128
ML_W = ML_HEADS * ML_DV

MLA_HEADS = 6
MLA_Q_RANK = 384
MLA_KV_RANK = 128
MLA_NOPE = 128
MLA_ROPE = 64
MLA_V = 128
MLA_W = MLA_HEADS * MLA_V
MLA_QK_PAD = 256

DIL_HEADS = 6
DIL_DH = 128
DIL_ROT = DIL_DH // 4
DIL_PAIRS = ((128, 1), (512, 4), (2048, 16))
DIL_W = DIL_HEADS * DIL_DH

LANES = 128
VMEM_LIMIT_BYTES = 60 * 1024 * 1024

_C_MLQ = 0
_C_MLK = 256
_C_MLV = 512
_C_MLO = 1024
_C_GATE = 1536
_C_CQ = 1664
_C_CKV = 2048
_C_DQ = 2304
_C_DK = 3072
_C_DV = 3840
_C_END = 4608


def _rms(x, gain, n):
    ms = jnp.sum(x * x, axis=-1, keepdims=True) * (1.0 / n)
    return x * lax.rsqrt(ms + EPS) * gain


def _rope_tile(x, cos_f, sin_f, half):
    lane = lax.broadcasted_iota(jnp.int32, x.shape, 1)
    sw = jnp.where(lane < half, pltpu.roll(x, LANES - half, 1), pltpu.roll(x, half, 1))
    return x * cos_f + sw * sin_f


def _log_sigmoid(x):
    return jnp.minimum(x, 0.0) - jnp.log1p(jnp.exp(-jnp.abs(x)))


def _inproj_kernel(x_ref, gmix_ref, w_ref, wqb_ref, wkvb_ref, gbias_ref, qlat_ref, kvlat_ref,
                   hg_ref, cosb_ref, sinb_ref, cosc_ref, sinc_ref,
                   mlq_ref, mlkt_ref, mlv_ref, mlo_ref, gate_ref, gatet_ref, q_ref, k_ref, v_ref,
                   dq_ref, dk_ref, dv_ref, *, mla_scale, dil_scale):
    x = x_ref[...]
    h = _rms(x, gmix_ref[...], x.shape[-1]).astype(jnp.bfloat16)

    def proj(c0, c1):
        return jnp.dot(h, w_ref[:, c0:c1], preferred_element_type=jnp.float32)

    cos_b = cosb_ref[...]
    sin_b = sinb_ref[...]
    cos_c = cosc_ref[...]
    sin_c = sinc_ref[...]
    qg_n = hg_ref[0:1, :]
    qg_r = hg_ref[1:2, :]
    kg_n = hg_ref[2:3, :]
    kg_r = hg_ref[3:4, :]
    dqg = hg_ref[4:5, :]
    dkg = hg_ref[5:6, :]

    cq = _rms(proj(_C_CQ, _C_CKV), qlat_ref[...], MLA_Q_RANK).astype(jnp.bfloat16)
    zc = proj(_C_CKV, _C_DQ)
    ckv = _rms(zc[:, :MLA_KV_RANK], kvlat_ref[...], MLA_KV_RANK).astype(jnp.bfloat16)
    zdq = proj(_C_DQ, _C_DK)
    zdk = proj(_C_DK, _C_DV)
    zq = jnp.dot(cq, wqb_ref[...], preferred_element_type=jnp.float32)
    zkv = jnp.dot(ckv, wkvb_ref[...], preferred_element_type=jnp.float32)

    k_rope = _rope_tile(_rms(zc[:, MLA_KV_RANK:], kg_r, MLA_ROPE), cos_b, sin_b, MLA_ROPE // 2)
    k_rope = k_rope.astype(jnp.bfloat16)
    ones_col = jnp.where(lax.broadcasted_iota(jnp.int32, k_rope.shape, 1) == 0, 1.0, 0.0)
    ones_col = ones_col.astype(jnp.bfloat16)
    for hd in range(MLA_HEADS):
        c = hd * MLA_QK_PAD
        q_nope = _rms(zq[:, c:c + MLA_NOPE], qg_n, MLA_NOPE)
        q_rope = _rope_tile(_rms(zq[:, c + MLA_NOPE:c + MLA_QK_PAD], qg_r, MLA_ROPE),
                            cos_b, sin_b, MLA_ROPE // 2)
        q_ref[hd, :, 0:MLA_NOPE] = (q_nope * mla_scale).astype(jnp.bfloat16)
        q_ref[hd, :, MLA_NOPE:MLA_QK_PAD] = (q_rope * mla_scale).astype(jnp.bfloat16)
        k_nope = _rms(zkv[:, c:c + MLA_NOPE], kg_n, MLA_NOPE)
        k_ref[hd, :, 0:MLA_NOPE] = k_nope.astype(jnp.bfloat16)
        k_ref[hd, :, MLA_NOPE:MLA_QK_PAD] = k_rope
        v_ref[hd, :, 0:MLA_V] = zkv[:, c + MLA_NOPE:c + MLA_QK_PAD].astype(jnp.bfloat16)
        v_ref[hd, :, MLA_V:2 * MLA_V] = ones_col

    mlo_ref[...] = jax.nn.sigmoid(proj(_C_MLO, _C_GATE)).astype(jnp.bfloat16)
    g = proj(_C_GATE, _C_CQ) + gbias_ref[...]
    lane = lax.broadcasted_iota(jnp.int32, g.shape, 1)
    is_forget = (lane % 8) >= 4
    g = jnp.where(is_forget, _log_sigmoid(g), g)
    gate_ref[...] = g
    gatet_ref[...] = g.T

    for hd in range(DIL_HEADS):
        c = hd * DIL_DH
        qh = _rope_tile(_rms(zdq[:, c:c + DIL_DH], dqg, DIL_DH), cos_c, sin_c, DIL_ROT // 2)
        kh = _rope_tile(_rms(zdk[:, c:c + DIL_DH], dkg, DIL_DH), cos_c, sin_c, DIL_ROT // 2)
        dq_ref[:, c:c + DIL_DH] = (qh * dil_scale).astype(jnp.bfloat16)
        dk_ref[:, c:c + DIL_DH] = kh.astype(jnp.bfloat16)
    zdv = proj(_C_DV, _C_END).astype(jnp.bfloat16)
    for hd in range(DIL_HEADS):
        dv_ref[:, 2 * hd * DIL_DH:(2 * hd + 1) * DIL_DH] = zdv[:, hd * DIL_DH:(hd + 1) * DIL_DH]
        dv_ref[:, (2 * hd + 1) * DIL_DH:(2 * hd + 2) * DIL_DH] = ones_col

    mlq_ref[...] = (proj(_C_MLQ, _C_MLK) * (ML_DK ** -0.5)).astype(jnp.bfloat16)
    mlkt_ref[...] = proj(_C_MLK, _C_MLV).T.astype(jnp.bfloat16)
    mlv_ref[...] = proj(_C_MLV, _C_MLO).astype(jnp.bfloat16)


def _const_spec(shape):
    n = len(shape)
    return pl.BlockSpec(shape, lambda *_: (0,) * n, pipeline_mode=pl.Buffered(1))


def _inproj(x2, gmix, w, wqb, wkvb, gbias, qlat, kvlat, hg, cosb, sinb, cosc, sinc, *, seq, tm):
    t, d = x2.shape
    nt = t // tm
    ns = seq // tm
    tok = lambda n: pl.BlockSpec((tm, n), lambda i: (i, 0))
    pos = pl.BlockSpec((tm, LANES), lambda i: (i % ns, 0))
    headed = lambda n: pl.BlockSpec((MLA_HEADS, tm, n), lambda i: (0, i, 0))
    bf = jnp.bfloat16
    out_shape = (
        jax.ShapeDtypeStruct((t, 256), bf), jax.ShapeDtypeStruct((256, t), bf),
        jax.ShapeDtypeStruct((t, ML_W), bf), jax.ShapeDtypeStruct((t, ML_W), bf),
        jax.ShapeDtypeStruct((t, LANES), jnp.float32), jax.ShapeDtypeStruct((LANES, t), jnp.float32),
        jax.ShapeDtypeStruct((MLA_HEADS, t, MLA_QK_PAD), bf),
        jax.ShapeDtypeStruct((MLA_HEADS, t, MLA_QK_PAD), bf),
        jax.ShapeDtypeStruct((MLA_HEADS, t, 2 * MLA_V), bf),
        jax.ShapeDtypeStruct((t, DIL_W), bf), jax.ShapeDtypeStruct((t, DIL_W), bf),
        jax.ShapeDtypeStruct((t, 2 * DIL_W), bf),
    )
    tok_t = lambda n: pl.BlockSpec((n, tm), lambda i: (0, i))
    out_specs = (tok(256), tok_t(256), tok(ML_W), tok(ML_W), tok(LANES), tok_t(LANES),
                 headed(MLA_QK_PAD), headed(MLA_QK_PAD), headed(2 * MLA_V),
                 tok(DIL_W), tok(DIL_W), tok(2 * DIL_W))
    in_specs = [tok(d), _const_spec(gmix.shape), _const_spec(w.shape), _const_spec(wqb.shape),
                _const_spec(wkvb.shape), _const_spec(gbias.shape), _const_spec(qlat.shape),
                _const_spec(kvlat.shape), _const_spec(hg.shape), pos, pos, pos, pos]
    kern = functools.partial(_inproj_kernel,
                             mla_scale=(MLA_NOPE + MLA_ROPE) ** -0.5 * math.log2(math.e),
                             dil_scale=DIL_DH ** -0.5 * math.log2(math.e))
    return pl.pallas_call(
        kern, grid=(nt,), in_specs=in_specs, out_specs=out_specs, out_shape=out_shape,
        compiler_params=pltpu.CompilerParams(dimension_semantics=("arbitrary",),
                                             vmem_limit_bytes=VMEM_LIMIT_BYTES),
        name="inproj",
    )(x2, gmix, w, wqb, wkvb, gbias, qlat, kvlat, hg, cosb, sinb, cosc, sinc)


def _mlstm_kernel(*refs, reverse, finalize, chunk):
    if finalize:
        (q_ref, kt_ref, v_ref, gate_ref, gatet_ref, hprev_ref, o_ref, onorm_ref,
         out_ref, ct_ref, m_ref) = refs
    else:
        (q_ref, kt_ref, v_ref, gate_ref, gatet_ref, out_ref, ct_ref, m_ref) = refs
    L = chunk
    c = pl.program_id(1)

    @pl.when(c == 0)
    def _():
        ct_ref[...] = jnp.zeros_like(ct_ref)
        m_ref[...] = jnp.zeros_like(m_ref)

    row = lax.broadcasted_iota(jnp.int32, (L, L), 0)
    col = lax.broadcasted_iota(jnp.int32, (L, L), 1)
    causal = (col >= row) if reverse else (col <= row)

    d0 = 8 if reverse else 0
    gates = gate_ref[...]
    slab = gatet_ref[d0:d0 + 8, :]
    def split3(x):
        hi = x.astype(jnp.bfloat16)
        rest = x - hi.astype(jnp.float32)
        mid = rest.astype(jnp.bfloat16)
        return hi, mid, (rest - mid.astype(jnp.float32)).astype(jnp.bfloat16)

    vis = causal.astype(jnp.bfloat16)
    vis_t = ((row >= col) if reverse else (row <= col)).astype(jnp.bfloat16)
    cum_t = sum(jnp.dot(vis, part, preferred_element_type=jnp.float32)
                for part in split3(gates))
    cum = sum(jnp.dot(part, vis_t, preferred_element_type=jnp.float32)
              for part in split3(slab))

    def rep(col):
        return jnp.broadcast_to(col, (L, LANES))

    def wide(xb, n):
        return xb if n == LANES else jnp.concatenate([xb] * (n // LANES), axis=1)

    lane = lax.broadcasted_iota(jnp.int32, (L, LANES), 1)
    ones_col = jnp.where(lane == 0, 1.0, 0.0).astype(jnp.bfloat16)
    zero_half = jnp.zeros((ML_DK, L), jnp.bfloat16)

    for hd in range(ML_HEADS):
        i_b = rep(gates[:, d0 + hd:d0 + hd + 1])
        i_row = slab[hd:hd + 1, :]
        cum_b = rep(cum_t[:, d0 + 4 + hd:d0 + 5 + hd])
        cum_row = cum[4 + hd:5 + hd, :]
        tot = cum_row[:, 0:1] if reverse else cum_row[:, L - 1:L]
        m_prev = m_ref[hd:hd + 1, 0:1]

        pair, half = hd // 2, hd % 2
        in_head = (lane >= half * ML_DK) & (lane < (half + 1) * ML_DK)
        qp = q_ref[:, pair * LANES:(pair + 1) * LANES]
        qm = jnp.where(in_head, qp, jnp.zeros_like(qp))
        kt_pair = kt_ref[pair * LANES:(pair + 1) * LANES, :]
        kt_h = kt_ref[pair * LANES + half * ML_DK:pair * LANES + (half + 1) * ML_DK, :]
        kmt = jnp.concatenate([kt_h, zero_half] if half == 0 else [zero_half, kt_h], axis=0)
        v_h = v_ref[:, hd * ML_DV:(hd + 1) * ML_DV]
        v_aug = jnp.concatenate([v_h, ones_col], axis=1)

        d_mat = jnp.where(causal, wide(cum_b, L) + (i_row - cum_row), NEG_INF)
        d_inter = cum_b + m_prev
        m_t = jnp.maximum(d_inter, rep(jnp.max(d_mat, axis=1, keepdims=True)))
        w_intra = jnp.exp(d_mat - wide(m_t, L))
        w_inter = jnp.exp(d_inter - m_t)
        s_raw = jnp.dot(qm, kt_pair, preferred_element_type=jnp.float32)
        sw = (s_raw * w_intra).astype(jnp.bfloat16)
        ct = ct_ref[hd]
        r = (jnp.dot(sw, v_aug, preferred_element_type=jnp.float32)
             + wide(w_inter, 2 * LANES) * jnp.dot(qm, ct.astype(jnp.bfloat16),
                                                  preferred_element_type=jnp.float32))
        num = r[:, :ML_DV]
        den = rep(r[:, ML_DV:ML_DV + 1])
        h_dir = num / jnp.maximum(jnp.abs(den), jnp.exp(-m_t))

        d_state = tot - cum_b + i_b
        m_new = jnp.maximum(tot + m_prev, jnp.max(d_state, axis=0, keepdims=True)[:, 0:1])
        w_s = jnp.exp(d_state - m_new)
        w_c = jnp.exp(tot + m_prev - m_new)
        vw = (v_aug.astype(jnp.float32) * wide(w_s, 2 * LANES)).astype(jnp.bfloat16)
        ct_ref[hd] = w_c * ct + jnp.dot(kmt, vw, preferred_element_type=jnp.float32)
        m_ref[hd:hd + 1, :] = jnp.broadcast_to(m_new, (1, LANES))

        sl = slice(hd * ML_DV, (hd + 1) * ML_DV)
        if finalize:
            h_sum = h_dir + hprev_ref[:, sl]
            y = _rms(h_sum, onorm_ref[hd:hd + 1, :], ML_DV)
            out_ref[:, sl] = (o_ref[:, sl].astype(jnp.float32) * y).astype(out_ref.dtype)
        else:
            out_ref[:, sl] = h_dir


def _mlstm(mlq, mlkt, mlv, gates, gates_t, hprev, mlo, onorm, *, batch, seq, chunk, reverse):
    t = mlq.shape[0]
    nc = seq // chunk
    finalize = hprev is not None
    if reverse:
        blk = lambda b, c: b * nc + (nc - 1 - c)
    else:
        blk = lambda b, c: b * nc + c
    tok = lambda n: pl.BlockSpec((chunk, n), lambda b, c: (blk(b, c), 0))
    tok_t = lambda n: pl.BlockSpec((n, chunk), lambda b, c: (0, blk(b, c)))
    in_specs = [tok(256), tok_t(256), tok(ML_W), tok(LANES), tok_t(LANES)]
    args = [mlq, mlkt, mlv, gates, gates_t]
    if finalize:
        in_specs += [tok(ML_W), tok(ML_W), pl.BlockSpec(onorm.shape, lambda b, c: (0, 0))]
        args += [hprev, mlo, onorm]
        out_dtype = jnp.bfloat16
    else:
        out_dtype = jnp.float32
    kern = functools.partial(_mlstm_kernel, reverse=reverse, finalize=finalize, chunk=chunk)
    return pl.pallas_call(
        kern, grid=(batch, nc), in_specs=in_specs, out_specs=tok(ML_W),
        out_shape=jax.ShapeDtypeStruct((t, ML_W), out_dtype),
        scratch_shapes=[pltpu.VMEM((ML_HEADS, LANES, 2 * LANES), jnp.float32),
                        pltpu.VMEM((8, LANES), jnp.float32)],
        compiler_params=pltpu.CompilerParams(dimension_semantics=("arbitrary", "arbitrary"),
                                             vmem_limit_bytes=VMEM_LIMIT_BYTES),
        name="mlstm_bwd" if reverse else "mlstm_fwd",
    )(*args)


def _mla_kernel(q_ref, k_ref, v_ref, o_ref, s_scr, acc_scr, *, tk, unroll):
    q = q_ref[...]
    tq = q.shape[0]
    nk = k_ref.shape[0] // tk

    def scores(j):
        start = pl.multiple_of(j * tk, tk)
        return lax.dot_general(q, k_ref[pl.ds(start, tk), :], (((1,), (1,)), ((), ())),
                               preferred_element_type=jnp.float32)

    def consume(slot, j, m_prev):
        start = pl.multiple_of(j * tk, tk)
        vt = v_ref[pl.ds(start, tk), :]
        m_new = jnp.maximum(m_prev, jnp.max(s_scr[slot], axis=1, keepdims=True))
        alpha = jnp.exp2(m_prev - m_new)
        p = jnp.exp2(s_scr[slot] - m_new).astype(jnp.bfloat16)
        acc_scr[...] = alpha * acc_scr[...] + jnp.dot(p, vt, preferred_element_type=jnp.float32)
        return m_new

    s_scr[0] = scores(0)
    acc_scr[...] = jnp.zeros_like(acc_scr)

    def body(jj, m):
        j = unroll * jj
        for u in range(unroll):
            s_scr[(u + 1) % 2] = scores(jnp.minimum(j + u + 1, nk - 1))
            m = consume(u % 2, j + u, m)
        return m

    lax.fori_loop(0, nk // unroll, body, jnp.full((tq, 1), NEG_INF, jnp.float32))
    acc = acc_scr[...]
    o_ref[...] = (acc[:, :MLA_V] / acc[:, MLA_V:MLA_V + 1]).astype(o_ref.dtype)


def _mla_attention(q, k, v, *, batch, seq, tq, tk, unroll):
    t = q.shape[1]
    nq = seq // tq
    assert unroll % 2 == 0 and (seq // tk) % unroll == 0
    return pl.pallas_call(
        functools.partial(_mla_kernel, tk=tk, unroll=unroll),
        grid=(batch, MLA_HEADS, nq),
        in_specs=[pl.BlockSpec((None, tq, MLA_QK_PAD), lambda b, h, i: (h, b * nq + i, 0)),
                  pl.BlockSpec((None, seq, MLA_QK_PAD), lambda b, h, i: (h, b, 0)),
                  pl.BlockSpec((None, seq, 2 * MLA_V), lambda b, h, i: (h, b, 0))],
        out_specs=pl.BlockSpec((tq, MLA_V), lambda b, h, i: (b * nq + i, h)),
        out_shape=jax.ShapeDtypeStruct((t, MLA_W), jnp.bfloat16),
        scratch_shapes=[pltpu.VMEM((2, tq, tk), jnp.float32),
                        pltpu.VMEM((tq, 2 * MLA_V), jnp.float32)],
        compiler_params=pltpu.CompilerParams(
            dimension_semantics=("arbitrary", "arbitrary", "arbitrary"),
            vmem_limit_bytes=VMEM_LIMIT_BYTES),
        name="mla_attn",
    )(q, k, v)


DIL_TILE = 256
DIL_REACH = max(w // 2 for w, _ in DIL_PAIRS)
DIL_NOFF = DIL_REACH // DIL_TILE
DIL_NWIN = 2 * DIL_NOFF + 1
DIL_SUB = 2


def _dil_bias_table():
    r = np.arange(DIL_TILE)[:, None]
    c = np.arange(DIL_TILE)[None, :]
    tiles = []
    for o in range(-2 * DIL_NOFF, 2 * DIL_NOFF + 1):
        delta = o * DIL_TILE + c - r
        mult = np.zeros_like(delta)
        for window, dil in DIL_PAIRS:
            mult += ((delta % dil) == 0) & (np.abs(delta) <= window // 2)
        tiles.append(np.where(mult > 0, np.log2(np.maximum(mult, 1)), NEG_INF))
    return np.stack(tiles).astype(np.float32)


def _dil_kernel(q_ref, k_ref, v_ref, bias_ref, o_ref, s_scr, *, nq, nwin):
    i = pl.program_id(2)
    nt = (((1,), (1,)), ((), ()))
    subs = range(DIL_SUB)
    qi = [i * DIL_SUB + sub for sub in subs]
    rows = [slice(sub * DIL_TILE, (sub + 1) * DIL_TILE) for sub in subs]
    q = [q_ref[r, :] for r in rows]
    w0 = [jnp.clip(t - DIL_NOFF, 0, nq - nwin) for t in qi]
    m_elem = [None] * DIL_SUB
    for idx in range(nwin):
        for sub in subs:
            start = pl.multiple_of((w0[sub] + idx) * DIL_TILE, DIL_TILE)
            bias = bias_ref[w0[sub] + idx - qi[sub] + 2 * DIL_NOFF]
            s = lax.dot_general(q[sub], k_ref[pl.ds(start, DIL_TILE), :], nt,
                                preferred_element_type=jnp.float32) + bias
            s_scr[sub, :, idx * DIL_TILE:(idx + 1) * DIL_TILE] = s
            m_elem[sub] = s if idx == 0 else jnp.maximum(m_elem[sub], s)
    for sub in subs:
        m = jnp.max(m_elem[sub], axis=1, keepdims=True)
        p = jnp.exp2(s_scr[sub] - m).astype(jnp.bfloat16)
        wstart = pl.multiple_of(w0[sub] * DIL_TILE, DIL_TILE)
        acc = jnp.dot(p, v_ref[pl.ds(wstart, nwin * DIL_TILE), :], preferred_element_type=jnp.float32)
        o_ref[rows[sub], :] = (acc[:, :DIL_DH] / acc[:, DIL_DH:DIL_DH + 1]).astype(o_ref.dtype)


def _dil_attention(dq, dk, dv, bias, *, batch, seq):
    t = dq.shape[0]
    nq = seq // DIL_TILE
    nwin = min(DIL_NWIN, nq)
    tq = DIL_SUB * DIL_TILE
    nstep = seq // tq
    return pl.pallas_call(
        functools.partial(_dil_kernel, nq=nq, nwin=nwin),
        grid=(batch, DIL_HEADS, nstep),
        in_specs=[pl.BlockSpec((tq, DIL_DH), lambda b, h, i: (b * nstep + i, h)),
                  pl.BlockSpec((seq, DIL_DH), lambda b, h, i: (b, h)),
                  pl.BlockSpec((seq, 2 * DIL_DH), lambda b, h, i: (b, h)),
                  _const_spec(bias.shape)],
        out_specs=pl.BlockSpec((tq, DIL_DH), lambda b, h, i: (b * nstep + i, h)),
        out_shape=jax.ShapeDtypeStruct((t, DIL_W), jnp.bfloat16),
        scratch_shapes=[pltpu.VMEM((DIL_SUB, DIL_TILE, nwin * DIL_TILE), jnp.float32)],
        compiler_params=pltpu.CompilerParams(
            dimension_semantics=("arbitrary", "arbitrary", "arbitrary"),
            vmem_limit_bytes=VMEM_LIMIT_BYTES),
        name="dil_attn",
    )(dq, dk, dv, bias)


def _outproj_kernel(x_ref, ya_ref, yb_ref, yc_ref, w_ref, o_ref):
    b0, b1 = ML_W, ML_W + MLA_W
    y = (jnp.dot(ya_ref[...], w_ref[0:b0, :], preferred_element_type=jnp.float32)
         + jnp.dot(yb_ref[...], w_ref[b0:b1, :], preferred_element_type=jnp.float32)
         + jnp.dot(yc_ref[...], w_ref[b1:, :], preferred_element_type=jnp.float32))
    o_ref[...] = x_ref[...] + y


def _outproj(x2, ya, yb, yc, w, *, tm):
    t, d = x2.shape
    tok = lambda n: pl.BlockSpec((tm, n), lambda i: (i, 0))
    return pl.pallas_call(
        _outproj_kernel, grid=(t // tm,),
        in_specs=[tok(d), tok(ML_W), tok(MLA_W), tok(DIL_W), _const_spec(w.shape)],
        out_specs=tok(d), out_shape=jax.ShapeDtypeStruct((t, d), jnp.float32),
        compiler_params=pltpu.CompilerParams(dimension_semantics=("arbitrary",),
                                             vmem_limit_bytes=VMEM_LIMIT_BYTES),
        name="outproj",
    )(x2, ya, yb, yc, w)


def _ffn_kernel(x_ref, g_ref, w1_ref, w2_ref, o_ref, h_ref):
    j = pl.program_id(1)

    @pl.when(j == 0)
    def _():
        x = x_ref[...]
        h_ref[...] = _rms(x, g_ref[...], x.shape[-1]).astype(h_ref.dtype)
        o_ref[...] = x

    u = jnp.maximum(jnp.dot(h_ref[...], w1_ref[...], preferred_element_type=jnp.float32), 0.0)
    u = (u * u).astype(jnp.bfloat16)
    o_ref[...] += jnp.dot(u, w2_ref[...], preferred_element_type=jnp.float32)


def _ffn(x2, g, w1, w2, *, tm, tf):
    t, d = x2.shape
    dff = w1.shape[1]
    return pl.pallas_call(
        _ffn_kernel, grid=(t // tm, dff // tf),
        in_specs=[pl.BlockSpec((tm, d), lambda i, j: (i, 0)),
                  pl.BlockSpec(g.shape, lambda i, j: (0, 0)),
                  pl.BlockSpec((d, tf), lambda i, j: (0, j)),
                  pl.BlockSpec((tf, d), lambda i, j: (j, 0))],
        out_specs=pl.BlockSpec((tm, d), lambda i, j: (i, 0)),
        out_shape=jax.ShapeDtypeStruct((t, d), jnp.float32),
        scratch_shapes=[pltpu.VMEM((tm, d), jnp.bfloat16)],
        compiler_params=pltpu.CompilerParams(dimension_semantics=("arbitrary", "arbitrary"),
                                             vmem_limit_bytes=VMEM_LIMIT_BYTES),
        name="ffn",
    )(x2, g, w1, w2)


def _rope_tables(seq, rot_dim, fill_cos):
    pos = jnp.arange(seq, dtype=jnp.float32)
    inv_freq = ROPE_THETA ** (-jnp.arange(0, rot_dim, 2, dtype=jnp.float32) / rot_dim)
    ang = pos[:, None] * inv_freq[None, :]
    cos, sin = jnp.cos(ang), jnp.sin(ang)
    pad = LANES - rot_dim
    cos_f = jnp.concatenate([cos, cos, jnp.full((seq, pad), fill_cos, jnp.float32)], axis=1)
    sin_f = jnp.concatenate([-sin, sin, jnp.zeros((seq, pad), jnp.float32)], axis=1)
    return cos_f, sin_f


def _pad_cols(a, n):
    return jnp.pad(a, ((0, 0), (0, n - a.shape[1])))


def _pack_w_in(w_in):
    splits = np.cumsum((256, 256, ML_W, ML_W, 4 * ML_HEADS, MLA_Q_RANK, MLA_KV_RANK + MLA_ROPE,
                        DIL_W, DIL_W, DIL_W))[:-1].tolist()
    (wq, wk, wv, wo, wg, wcq, wckv, wdq, wdk, wdv) = jnp.split(w_in, splits, axis=1)
    packed = jnp.concatenate([wq, wk, wv, wo, _pad_cols(wg, LANES), wcq, _pad_cols(wckv, 256),
                              wdq, wdk, wdv], axis=1)
    return packed.astype(jnp.bfloat16)


def _layer(x2, p, tables, *, batch, seq):
    cosb, sinb, cosc, sinc, dil_bias = tables
    w = _pack_w_in(p['w_in'])
    wqb = p['mla_w_q_b'].reshape(MLA_Q_RANK, MLA_HEADS, MLA_NOPE + MLA_ROPE)
    wqb = jnp.pad(wqb, ((0, 0), (0, 0), (0, MLA_QK_PAD - MLA_NOPE - MLA_ROPE)))
    wqb = wqb.reshape(MLA_Q_RANK, MLA_HEADS * MLA_QK_PAD).astype(jnp.bfloat16)
    wkvb = p['mla_w_kv_b'].astype(jnp.bfloat16)
    gbias = jnp.concatenate([p['ml_i_bias'][0], p['ml_f_bias'][0], p['ml_i_bias'][1], p['ml_f_bias'][1]])
    gbias = _pad_cols(gbias[None, :], LANES)
    qh, kh = p['mla_q_head_norm'], p['mla_k_head_norm']
    hg = jnp.stack([qh[:MLA_NOPE], jnp.pad(qh[MLA_NOPE:], (0, LANES - MLA_ROPE)),
                    kh[:MLA_NOPE], jnp.pad(kh[MLA_NOPE:], (0, LANES - MLA_ROPE)),
                    p['dil_q_norm'], p['dil_k_norm'],
                    jnp.zeros((LANES,), jnp.float32), jnp.zeros((LANES,), jnp.float32)])

    (mlq, mlkt, mlv, mlo, gates, gates_t, q, k, v, dq, dk, dv) = _inproj(
        x2, p['norm_mix'][None, :], w, wqb, wkvb, gbias, p['mla_q_norm'][None, :],
        p['mla_kv_norm'][None, :], hg, cosb, sinb, cosc, sinc, seq=seq, tm=min(512, seq))

    chunk = min(256, seq)
    h_bwd = _mlstm(mlq, mlkt, mlv, gates, gates_t, None, None, None, batch=batch, seq=seq, chunk=chunk,
                   reverse=True)
    ya = _mlstm(mlq, mlkt, mlv, gates, gates_t, h_bwd, mlo, p['ml_out_norm'], batch=batch, seq=seq,
                chunk=chunk, reverse=False)
    yb = _mla_attention(q, k, v, batch=batch, seq=seq, tq=min(512, seq), tk=min(512, seq // 16), unroll=16)
    yc = _dil_attention(dq, dk, dv, dil_bias, batch=batch, seq=seq)

    x2 = _outproj(x2, ya, yb, yc, p['w_out'].astype(jnp.bfloat16), tm=min(512, seq))
    return _ffn(x2, p['norm_ff'][None, :], p['w_ff1'].astype(jnp.bfloat16), p['w_ff2'].astype(jnp.bfloat16),
                tm=min(512, seq), tf=1024)


def kernel(x, norm_mix, w_in, ml_i_bias, ml_f_bias, ml_out_norm, mla_q_norm, mla_w_q_b, mla_kv_norm,
           mla_w_kv_b, mla_q_head_norm, mla_k_head_norm, dil_q_norm, dil_k_norm, w_out, norm_ff,
           w_ff1, w_ff2):
    batch, seq, d = x.shape
    assert seq % (DIL_SUB * DIL_TILE) == 0
    params = dict(norm_mix=norm_mix, w_in=w_in, ml_i_bias=ml_i_bias, ml_f_bias=ml_f_bias,
                  ml_out_norm=ml_out_norm, mla_q_norm=mla_q_norm, mla_w_q_b=mla_w_q_b,
                  mla_kv_norm=mla_kv_norm, mla_w_kv_b=mla_w_kv_b, mla_q_head_norm=mla_q_head_norm,
                  mla_k_head_norm=mla_k_head_norm, dil_q_norm=dil_q_norm, dil_k_norm=dil_k_norm,
                  w_out=w_out, norm_ff=norm_ff, w_ff1=w_ff1, w_ff2=w_ff2)
    cosb, sinb = _rope_tables(seq, MLA_ROPE, 0.0)
    cosc, sinc = _rope_tables(seq, DIL_ROT, 1.0)
    tables = (cosb, sinb, cosc, sinc, jnp.asarray(_dil_bias_table()))
    x2 = x.reshape(batch * seq, d)
    for layer in range(norm_mix.shape[0]):
        x2 = _layer(x2, {name: val[layer] for name, val in params.items()}, tables, batch=batch, seq=seq)
    return x2.reshape(batch, seq, d)
```

```python
import functools
import math

import numpy as np
import jax
import jax.numpy as jnp
from jax import lax
from jax.experimental import pallas as pl
from jax.experimental.pallas import tpu as pltpu

EPS = 1e-6
NEG_INF = -1e30
ROPE_THETA = 500000.0

ML_HEADS = 4
ML_DK = 64
ML_DV = 128
ML_W = ML_HEADS * ML_DV

MLA_HEADS = 6
MLA_Q_RANK = 384
MLA_KV_RANK = 128
MLA_NOPE = 128
MLA_ROPE = 64
MLA_V = 128
MLA_W = MLA_HEADS * MLA_V
MLA_QK_PAD = 256
MLA_VT_ROWS = 144
MLA_TILE = 512

DIL_HEADS = 6
DIL_DH = 128
DIL_ROT = DIL_DH // 4
DIL_PAIRS = ((128, 1), (512, 4), (2048, 16))
DIL_W = DIL_HEADS * DIL_DH

LANES = 128
VMEM_LIMIT_BYTES = 60 * 1024 * 1024

_C_MLQ = 0
_C_MLK = 256
_C_MLV = 512
_C_MLO = 1024
_C_GATE = 1536
_C_CQ = 1664
_C_CKV = 2048
_C_DQ = 2304
_C_DK = 3072
_C_DV = 3840
_C_END = 4608


def _rms(x, gain, n):
    ms = jnp.sum(x * x, axis=-1, keepdims=True) * (1.0 / n)
    return x * lax.rsqrt(ms + EPS) * gain


def _rope_tile(x, cos_f, sin_f, half):
    lane = lax.broadcasted_iota(jnp.int32, x.shape, 1)
    sw = jnp.where(lane < half, pltpu.roll(x, LANES - half, 1), pltpu.roll(x, half, 1))
    return x * cos_f + sw * sin_f


def _log_sigmoid(x):
    return jnp.minimum(x, 0.0) - jnp.log1p(jnp.exp(-jnp.abs(x)))


def _inproj_kernel(x_ref, gmix_ref, w_ref, wqb_ref, wkvb_ref, gbias_ref, qlat_ref, kvlat_ref,
                   hg_ref, cosb_ref, sinb_ref, cosc_ref, sinc_ref,
                   mlq_ref, mlkt_ref, mlv_ref, mlo_ref, gate_ref, gatet_ref, qt_ref, k_ref, vt_ref,
                   dq_ref, dk_ref, dv_ref, *, mla_scale, dil_scale):
    x = x_ref[...]
    h = _rms(x, gmix_ref[...], x.shape[-1]).astype(jnp.bfloat16)

    def proj(c0, c1):
        return jnp.dot(h, w_ref[:, c0:c1], preferred_element_type=jnp.float32)

    cos_b = cosb_ref[...]
    sin_b = sinb_ref[...]
    cos_c = cosc_ref[...]
    sin_c = sinc_ref[...]
    qg_n = hg_ref[0:1, :]
    qg_r = hg_ref[1:2, :]
    kg_n = hg_ref[2:3, :]
    kg_r = hg_ref[3:4, :]
    dqg = hg_ref[4:5, :]
    dkg = hg_ref[5:6, :]

    cq = _rms(proj(_C_CQ, _C_CKV), qlat_ref[...], MLA_Q_RANK).astype(jnp.bfloat16)
    zc = proj(_C_CKV, _C_DQ)
    ckv = _rms(zc[:, :MLA_KV_RANK], kvlat_ref[...], MLA_KV_RANK).astype(jnp.bfloat16)
    zdq = proj(_C_DQ, _C_DK)
    zdk = proj(_C_DK, _C_DV)
    zq = jnp.dot(cq, wqb_ref[...], preferred_element_type=jnp.float32)
    zkv = jnp.dot(ckv, wkvb_ref[...], preferred_element_type=jnp.float32)

    k_rope = _rope_tile(_rms(zc[:, MLA_KV_RANK:], kg_r, MLA_ROPE), cos_b, sin_b, MLA_ROPE // 2)
    k_rope = k_rope.astype(jnp.bfloat16)
    ones_col = jnp.where(lax.broadcasted_iota(jnp.int32, k_rope.shape, 1) == 0, 1.0, 0.0)
    ones_col = ones_col.astype(jnp.bfloat16)
    pad_rows = MLA_VT_ROWS - MLA_V
    ones_row = jnp.where(lax.broadcasted_iota(jnp.int32, (pad_rows, x.shape[0]), 0) == 0, 1.0, 0.0)
    ones_row = ones_row.astype(jnp.bfloat16)
    for hd in range(MLA_HEADS):
        c = hd * MLA_QK_PAD
        q_nope = _rms(zq[:, c:c + MLA_NOPE], qg_n, MLA_NOPE)
        q_rope = _rope_tile(_rms(zq[:, c + MLA_NOPE:c + MLA_QK_PAD], qg_r, MLA_ROPE),
                            cos_b, sin_b, MLA_ROPE // 2)
        qt_ref[hd, 0, 0:MLA_NOPE, :] = (q_nope * mla_scale).T.astype(jnp.bfloat16)
        qt_ref[hd, 0, MLA_NOPE:MLA_QK_PAD, :] = (q_rope * mla_scale).T.astype(jnp.bfloat16)
        k_nope = _rms(zkv[:, c:c + MLA_NOPE], kg_n, MLA_NOPE)
        k_ref[hd, :, 0:MLA_NOPE] = k_nope.astype(jnp.bfloat16)
        k_ref[hd, :, MLA_NOPE:MLA_QK_PAD] = k_rope
        vt_ref[hd, 0, 0:MLA_V, :] = zkv[:, c + MLA_NOPE:c + MLA_QK_PAD].T.astype(jnp.bfloat16)
        vt_ref[hd, 0, MLA_V:MLA_VT_ROWS, :] = ones_row

    mlkt_ref[...] = proj(_C_MLK, _C_MLV).T.astype(jnp.bfloat16)
    mlo_ref[...] = jax.nn.sigmoid(proj(_C_MLO, _C_GATE)).astype(jnp.bfloat16)
    g = proj(_C_GATE, _C_CQ) + gbias_ref[...]
    lane = lax.broadcasted_iota(jnp.int32, g.shape, 1)
    is_forget = (lane % 8) >= 4
    g = jnp.where(is_forget, _log_sigmoid(g), g)
    gate_ref[...] = g
    gatet_ref[...] = g.T

    for hd in range(DIL_HEADS):
        c = hd * DIL_DH
        qh = _rope_tile(_rms(zdq[:, c:c + DIL_DH], dqg, DIL_DH), cos_c, sin_c, DIL_ROT // 2)
        kh = _rope_tile(_rms(zdk[:, c:c + DIL_DH], dkg, DIL_DH), cos_c, sin_c, DIL_ROT // 2)
        dq_ref[:, c:c + DIL_DH] = (qh * dil_scale).astype(jnp.bfloat16)
        dk_ref[:, c:c + DIL_DH] = kh.astype(jnp.bfloat16)
    zdv = proj(_C_DV, _C_END).astype(jnp.bfloat16)
    for hd in range(DIL_HEADS):
        dv_ref[:, 2 * hd * DIL_DH:(2 * hd + 1) * DIL_DH] = zdv[:, hd * DIL_DH:(hd + 1) * DIL_DH]
        dv_ref[:, (2 * hd + 1) * DIL_DH:(2 * hd + 2) * DIL_DH] = ones_col

    mlq_ref[...] = (proj(_C_MLQ, _C_MLK) * (ML_DK ** -0.5)).astype(jnp.bfloat16)
    mlv_ref[...] = proj(_C_MLV, _C_MLO).astype(jnp.bfloat16)


def _const_spec(shape):
    n = len(shape)
    return pl.BlockSpec(shape, lambda *_: (0,) * n, pipeline_mode=pl.Buffered(1))


def _inproj(x2, gmix, w, wqb, wkvb, gbias, qlat, kvlat, hg, cosb, sinb, cosc, sinc, *, seq, tm):
    t, d = x2.shape
    nt = t // tm
    ns = seq // tm
    tok = lambda n: pl.BlockSpec((tm, n), lambda i: (i, 0))
    pos = pl.BlockSpec((tm, LANES), lambda i: (i % ns, 0))
    headed = lambda n: pl.BlockSpec((MLA_HEADS, tm, n), lambda i: (0, i, 0))
    headed_t = lambda n: pl.BlockSpec((MLA_HEADS, 1, n, tm), lambda i: (0, i, 0, 0))
    bf = jnp.bfloat16
    out_shape = (
        jax.ShapeDtypeStruct((t, 256), bf), jax.ShapeDtypeStruct((256, t), bf),
        jax.ShapeDtypeStruct((t, ML_W), bf), jax.ShapeDtypeStruct((t, ML_W), bf),
        jax.ShapeDtypeStruct((t, LANES), jnp.float32), jax.ShapeDtypeStruct((LANES, t), jnp.float32),
        jax.ShapeDtypeStruct((MLA_HEADS, nt, MLA_QK_PAD, tm), bf),
        jax.ShapeDtypeStruct((MLA_HEADS, t, MLA_QK_PAD), bf),
        jax.ShapeDtypeStruct((MLA_HEADS, nt, MLA_VT_ROWS, tm), bf),
        jax.ShapeDtypeStruct((t, DIL_W), bf), jax.ShapeDtypeStruct((t, DIL_W), bf),
        jax.ShapeDtypeStruct((t, 2 * DIL_W), bf),
    )
    tok_t = lambda n: pl.BlockSpec((n, tm), lambda i: (0, i))
    out_specs = (tok(256), tok_t(256), tok(ML_W), tok(ML_W), tok(LANES), tok_t(LANES),
                 headed_t(MLA_QK_PAD), headed(MLA_QK_PAD), headed_t(MLA_VT_ROWS),
                 tok(DIL_W), tok(DIL_W), tok(2 * DIL_W))
    in_specs = [tok(d), _const_spec(gmix.shape), _const_spec(w.shape), _const_spec(wqb.shape),
                _const_spec(wkvb.shape), _const_spec(gbias.shape), _const_spec(qlat.shape),
                _const_spec(kvlat.shape), _const_spec(hg.shape), pos, pos, pos, pos]
    kern = functools.partial(_inproj_kernel,
                             mla_scale=(MLA_NOPE + MLA_ROPE) ** -0.5 * math.log2(math.e),
                             dil_scale=DIL_DH ** -0.5 * math.log2(math.e))
    return pl.pallas_call(
        kern, grid=(nt,), in_specs=in_specs, out_specs=out_specs, out_shape=out_shape,
        compiler_params=pltpu.CompilerParams(dimension_semantics=("arbitrary",),
                                             vmem_limit_bytes=VMEM_LIMIT_BYTES),
        name="inproj",
    )(x2, gmix, w, wqb, wkvb, gbias, qlat, kvlat, hg, cosb, sinb, cosc, sinc)


def _mlstm_kernel(*refs, reverse, finalize, chunk):
    if finalize:
        (q_ref, kt_ref, v_ref, gate_ref, gatet_ref, hprev_ref, o_ref, onorm_ref,
         out_ref, ct_ref, m_ref) = refs
    else:
        (q_ref, kt_ref, v_ref, gate_ref, gatet_ref, out_ref, ct_ref, m_ref) = refs
    L = chunk
    c = pl.program_id(1)

    @pl.when(c == 0)
    def _():
        ct_ref[...] = jnp.zeros_like(ct_ref)
        m_ref[...] = jnp.zeros_like(m_ref)

    row = lax.broadcasted_iota(jnp.int32, (L, L), 0)
    col = lax.broadcasted_iota(jnp.int32, (L, L), 1)
    causal = (col >= row) if reverse else (col <= row)

    d0 = 8 if reverse else 0
    gates = gate_ref[...]
    slab = gatet_ref[d0:d0 + 8, :]
    def split3(x):
        hi = x.astype(jnp.bfloat16)
        rest = x - hi.astype(jnp.float32)
        mid = rest.astype(jnp.bfloat16)
        return hi, mid, (rest - mid.astype(jnp.float32)).astype(jnp.bfloat16)

    vis = causal.astype(jnp.bfloat16)
    vis_t = ((row >= col) if reverse else (row <= col)).astype(jnp.bfloat16)
    cum_t = sum(jnp.dot(vis, part, preferred_element_type=jnp.float32)
                for part in split3(gates))
    cum = sum(jnp.dot(part, vis_t, preferred_element_type=jnp.float32)
              for part in split3(slab))

    def rep(col):
        return jnp.broadcast_to(col, (L, LANES))

    def wide(xb, n):
        return xb if n == LANES else jnp.concatenate([xb] * (n // LANES), axis=1)

    lane = lax.broadcasted_iota(jnp.int32, (L, LANES), 1)
    ones_col = jnp.where(lane == 0, 1.0, 0.0).astype(jnp.bfloat16)
    zero_half = jnp.zeros((ML_DK, L), jnp.bfloat16)

    for hd in range(ML_HEADS):
        i_b = rep(gates[:, d0 + hd:d0 + hd + 1])
        i_row = slab[hd:hd + 1, :]
        cum_b = rep(cum_t[:, d0 + 4 + hd:d0 + 5 + hd])
        cum_row = cum[4 + hd:5 + hd, :]
        tot = cum_row[:, 0:1] if reverse else cum_row[:, L - 1:L]
        m_prev = m_ref[hd:hd + 1, 0:1]

        pair, half = hd // 2, hd % 2
        in_head = (lane >= half * ML_DK) & (lane < (half + 1) * ML_DK)
        qp = q_ref[:, pair * LANES:(pair + 1) * LANES]
        qm = jnp.where(in_head, qp, jnp.zeros_like(qp))
        kt_pair = kt_ref[pair * LANES:(pair + 1) * LANES, :]
        kt_h = kt_ref[pair * LANES + half * ML_DK:pair * LANES + (half + 1) * ML_DK, :]
        kmt = jnp.concatenate([kt_h, zero_half] if half == 0 else [zero_half, kt_h], axis=0)
        v_h = v_ref[:, hd * ML_DV:(hd + 1) * ML_DV]
        v_aug = jnp.concatenate([v_h, ones_col], axis=1)

        d_mat = jnp.where(causal, wide(cum_b, L) + (i_row - cum_row), NEG_INF)
        d_inter = cum_b + m_prev
        m_t = jnp.maximum(d_inter, rep(jnp.max(d_mat, axis=1, keepdims=True)))
        w_intra = jnp.exp(d_mat - wide(m_t, L))
        w_inter = jnp.exp(d_inter - m_t)
        s_raw = jnp.dot(qm, kt_pair, preferred_element_type=jnp.float32)
        sw = (s_raw * w_intra).astype(jnp.bfloat16)
        ct = ct_ref[hd]
        r = (jnp.dot(sw, v_aug, preferred_element_type=jnp.float32)
             + wide(w_inter, 2 * LANES) * jnp.dot(qm, ct.astype(jnp.bfloat16),
                                                  preferred_element_type=jnp.float32))
        num = r[:, :ML_DV]
        den = rep(r[:, ML_DV:ML_DV + 1])
        h_dir = num / jnp.maximum(jnp.abs(den), jnp.exp(-m_t))

        d_state = tot - cum_b + i_b
        m_new = jnp.maximum(tot + m_prev, jnp.max(d_state, axis=0, keepdims=True)[:, 0:1])
        w_s = jnp.exp(d_state - m_new)
        w_c = jnp.exp(tot + m_prev - m_new)
        vw = (v_aug.astype(jnp.float32) * wide(w_s, 2 * LANES)).astype(jnp.bfloat16)
        ct_ref[hd] = w_c * ct + jnp.dot(kmt, vw, preferred_element_type=jnp.float32)
        m_ref[hd:hd + 1, :] = jnp.broadcast_to(m_new, (1, LANES))

        sl = slice(hd * ML_DV, (hd + 1) * ML_DV)
        if finalize:
            h_sum = h_dir + hprev_ref[:, sl]
            y = _rms(h_sum, onorm_ref[hd:hd + 1, :], ML_DV)
            out_ref[:, sl] = (o_ref[:, sl].astype(jnp.float32) * y).astype(out_ref.dtype)
        else:
            out_ref[:, sl] = h_dir


def _mlstm(mlq, mlkt, mlv, gates, gates_t, hprev, mlo, onorm, *, batch, seq, chunk, reverse):
    t = mlq.shape[0]
    nc = seq // chunk
    finalize = hprev is not None
    if reverse:
        blk = lambda b, c: b * nc + (nc - 1 - c)
    else:
        blk = lambda b, c: b * nc + c
    tok = lambda n: pl.BlockSpec((chunk, n), lambda b, c: (blk(b, c), 0))
    tok_t = lambda n: pl.BlockSpec((n, chunk), lambda b, c: (0, blk(b, c)))
    in_specs = [tok(256), tok_t(256), tok(ML_W), tok(LANES), tok_t(LANES)]
    args = [mlq, mlkt, mlv, gates, gates_t]
    if finalize:
        in_specs += [tok(ML_W), tok(ML_W), pl.BlockSpec(onorm.shape, lambda b, c: (0, 0))]
        args += [hprev, mlo, onorm]
        out_dtype = jnp.bfloat16
    else:
        out_dtype = jnp.float32
    kern = functools.partial(_mlstm_kernel, reverse=reverse, finalize=finalize, chunk=chunk)
    return pl.pallas_call(
        kern, grid=(batch, nc), in_specs=in_specs, out_specs=tok(ML_W),
        out_shape=jax.ShapeDtypeStruct((t, ML_W), out_dtype),
        scratch_shapes=[pltpu.VMEM((ML_HEADS, LANES, 2 * LANES), jnp.float32),
                        pltpu.VMEM((8, LANES), jnp.float32)],
        compiler_params=pltpu.CompilerParams(dimension_semantics=("arbitrary", "arbitrary"),
                                             vmem_limit_bytes=VMEM_LIMIT_BYTES),
        name="mlstm_bwd" if reverse else "mlstm_fwd",
    )(*args)


def _mla_kernel(qt_ref, k_ref, vt_ref, o_ref, s_scr, acc_scr, *, unroll):
    qt = qt_ref[...]
    tq = qt.shape[1]
    tk = vt_ref.shape[2]
    nk = vt_ref.shape[0]

    def scores(j):
        start = pl.multiple_of(j * tk, tk)
        return jnp.dot(k_ref[pl.ds(start, tk), :], qt, preferred_element_type=jnp.float32)

    def consume(slot, j, m_prev):
        m_new = jnp.maximum(m_prev, jnp.max(s_scr[slot], axis=0, keepdims=True))
        alpha = jnp.exp2(m_prev - m_new)
        p = jnp.exp2(s_scr[slot] - m_new).astype(jnp.bfloat16)
        acc_scr[...] = alpha * acc_scr[...] + jnp.dot(vt_ref[j], p, preferred_element_type=jnp.float32)
        return m_new

    s_scr[0] = scores(0)
    acc_scr[...] = jnp.zeros_like(acc_scr)

    def body(jj, m):
        j = unroll * jj
        for u in range(unroll):
            s_scr[(u + 1) % 2] = scores(jnp.minimum(j + u + 1, nk - 1))
            m = consume(u % 2, j + u, m)
        return m

    lax.fori_loop(0, nk // unroll, body, jnp.full((1, tq), NEG_INF, jnp.float32))
    acc = acc_scr[...]
    o_ref[...] = (acc[:MLA_V] / acc[MLA_V:MLA_V + 1]).T.astype(o_ref.dtype)


def _mla_attention(qt, k, vt, *, batch, seq, unroll):
    tile = qt.shape[-1]
    t = k.shape[1]
    nq = seq // tile
    assert unroll % 2 == 0 and nq % unroll == 0
    return pl.pallas_call(
        functools.partial(_mla_kernel, unroll=unroll),
        grid=(batch, MLA_HEADS, nq),
        in_specs=[pl.BlockSpec((None, None, MLA_QK_PAD, tile), lambda b, h, i: (h, b * nq + i, 0, 0)),
                  pl.BlockSpec((None, seq, MLA_QK_PAD), lambda b, h, i: (h, b, 0)),
                  pl.BlockSpec((None, nq, MLA_VT_ROWS, tile), lambda b, h, i: (h, b, 0, 0))],
        out_specs=pl.BlockSpec((tile, MLA_V), lambda b, h, i: (b * nq + i, h)),
        out_shape=jax.ShapeDtypeStruct((t, MLA_W), jnp.bfloat16),
        scratch_shapes=[pltpu.VMEM((2, tile, tile), jnp.float32),
                        pltpu.VMEM((MLA_VT_ROWS, tile), jnp.float32)],
        compiler_params=pltpu.CompilerParams(
            dimension_semantics=("arbitrary", "arbitrary", "arbitrary"),
            vmem_limit_bytes=VMEM_LIMIT_BYTES),
        name="mla_attn",
    )(qt, k, vt)


DIL_TILE = 256
DIL_REACH = max(w // 2 for w, _ in DIL_PAIRS)
DIL_NOFF = DIL_REACH // DIL_TILE
DIL_NWIN = 2 * DIL_NOFF + 1
DIL_SUB = 2


def _dil_bias_table():
    r = np.arange(DIL_TILE)[:, None]
    c = np.arange(DIL_TILE)[None, :]
    tiles = []
    for o in range(-2 * DIL_NOFF, 2 * DIL_NOFF + 1):
        delta = o * DIL_TILE + c - r
        mult = np.zeros_like(delta)
        for window, dil in DIL_PAIRS:
            mult += ((delta % dil) == 0) & (np.abs(delta) <= window // 2)
        tiles.append(np.where(mult > 0, np.log2(np.maximum(mult, 1)), NEG_INF))
    return np.stack(tiles).astype(np.float32)


def _dil_kernel(q_ref, k_ref, v_ref, bias_ref, o_ref, s_scr, *, nq, nwin):
    i = pl.program_id(2)
    nt = (((1,), (1,)), ((), ()))
    subs = range(DIL_SUB)
    qi = [i * DIL_SUB + sub for sub in subs]
    rows = [slice(sub * DIL_TILE, (sub + 1) * DIL_TILE) for sub in subs]
    q = [q_ref[r, :] for r in rows]
    w0 = [jnp.clip(t - DIL_NOFF, 0, nq - nwin) for t in qi]
    m_elem = [None] * DIL_SUB
    for idx in range(nwin):
        for sub in subs:
            start = pl.multiple_of((w0[sub] + idx) * DIL_TILE, DIL_TILE)
            bias = bias_ref[w0[sub] + idx - qi[sub] + 2 * DIL_NOFF]
            s = lax.dot_general(q[sub], k_ref[pl.ds(start, DIL_TILE), :], nt,
                                preferred_element_type=jnp.float32) + bias
            s_scr[sub, :, idx * DIL_TILE:(idx + 1) * DIL_TILE] = s
            m_elem[sub] = s if idx == 0 else jnp.maximum(m_elem[sub], s)
    for sub in subs:
        m = jnp.max(m_elem[sub], axis=1, keepdims=True)
        p = jnp.exp2(s_scr[sub] - m).astype(jnp.bfloat16)
        wstart = pl.multiple_of(w0[sub] * DIL_TILE, DIL_TILE)
        acc = jnp.dot(p, v_ref[pl.ds(wstart, nwin * DIL_TILE), :], preferred_element_type=jnp.float32)
        o_ref[rows[sub], :] = (acc[:, :DIL_DH] / acc[:, DIL_DH:DIL_DH + 1]).astype(o_ref.dtype)


def _dil_attention(dq, dk, dv, bias, *, batch, seq):
    t = dq.shape[0]
    nq = seq // DIL_TILE
    nwin = min(DIL_NWIN, nq)
    tq = DIL_SUB * DIL_TILE
    nstep = seq // tq
    return pl.pallas_call(
        functools.partial(_dil_kernel, nq=nq, nwin=nwin),
        grid=(batch, DIL_HEADS, nstep),
        in_specs=[pl.BlockSpec((tq, DIL_DH), lambda b, h, i: (b * nstep + i, h)),
                  pl.BlockSpec((seq, DIL_DH), lambda b, h, i: (b, h)),
                  pl.BlockSpec((seq, 2 * DIL_DH), lambda b, h, i: (b, h)),
                  _const_spec(bias.shape)],
        out_specs=pl.BlockSpec((tq, DIL_DH), lambda b, h, i: (b * nstep + i, h)),
        out_shape=jax.ShapeDtypeStruct((t, DIL_W), jnp.bfloat16),
        scratch_shapes=[pltpu.VMEM((DIL_SUB, DIL_TILE, nwin * DIL_TILE), jnp.float32)],
        compiler_params=pltpu.CompilerParams(
            dimension_semantics=("arbitrary", "arbitrary", "arbitrary"),
            vmem_limit_bytes=VMEM_LIMIT_BYTES),
        name="dil_attn",
    )(dq, dk, dv, bias)


def _outproj_kernel(x_ref, ya_ref, yb_ref, yc_ref, w_ref, o_ref):
    b0, b1 = ML_W, ML_W + MLA_W
    y = (jnp.dot(ya_ref[...], w_ref[0:b0, :], preferred_element_type=jnp.float32)
         + jnp.dot(yb_ref[...], w_ref[b0:b1, :], preferred_element_type=jnp.float32)
         + jnp.dot(yc_ref[...], w_ref[b1:, :], preferred_element_type=jnp.float32))
    o_ref[...] = x_ref[...] + y


def _outproj(x2, ya, yb, yc, w, *, tm):
    t, d = x2.shape
    tok = lambda n: pl.BlockSpec((tm, n), lambda i: (i, 0))
    return pl.pallas_call(
        _outproj_kernel, grid=(t // tm,),
        in_specs=[tok(d), tok(ML_W), tok(MLA_W), tok(DIL_W), _const_spec(w.shape)],
        out_specs=tok(d), out_shape=jax.ShapeDtypeStruct((t, d), jnp.float32),
        compiler_params=pltpu.CompilerParams(dimension_semantics=("arbitrary",),
                                             vmem_limit_bytes=VMEM_LIMIT_BYTES),
        name="outproj",
    )(x2, ya, yb, yc, w)


def _ffn_kernel(x_ref, g_ref, w1_ref, w2_ref, o_ref, h_ref):
    j = pl.program_id(1)

    @pl.when(j == 0)
    def _():
        x = x_ref[...]
        h_ref[...] = _rms(x, g_ref[...], x.shape[-1]).astype(h_ref.dtype)
        o_ref[...] = x

    u = jnp.maximum(jnp.dot(h_ref[...], w1_ref[...], preferred_element_type=jnp.float32), 0.0)
    u = (u * u).astype(jnp.bfloat16)
    o_ref[...] += jnp.dot(u, w2_ref[...], preferred_element_type=jnp.float32)


def _ffn(x2, g, w1, w2, *, tm, tf):
    t, d = x2.shape
    dff = w1.shape[1]
    return pl.pallas_call(
        _ffn_kernel, grid=(t // tm, dff // tf),
        in_specs=[pl.BlockSpec((tm, d), lambda i, j: (i, 0)),
                  pl.BlockSpec(g.shape, lambda i, j: (0, 0)),
                  pl.BlockSpec((d, tf), lambda i, j: (0, j)),
                  pl.BlockSpec((tf, d), lambda i, j: (j, 0))],
        out_specs=pl.BlockSpec((tm, d), lambda i, j: (i, 0)),
        out_shape=jax.ShapeDtypeStruct((t, d), jnp.float32),
        scratch_shapes=[pltpu.VMEM((tm, d), jnp.bfloat16)],
        compiler_params=pltpu.CompilerParams(dimension_semantics=("arbitrary", "arbitrary"),
                                             vmem_limit_bytes=VMEM_LIMIT_BYTES),
        name="ffn",
    )(x2, g, w1, w2)


def _rope_tables(seq, rot_dim, fill_cos):
    pos = jnp.arange(seq, dtype=jnp.float32)
    inv_freq = ROPE_THETA ** (-jnp.arange(0, rot_dim, 2, dtype=jnp.float32) / rot_dim)
    ang = pos[:, None] * inv_freq[None, :]
    cos, sin = jnp.cos(ang), jnp.sin(ang)
    pad = LANES - rot_dim
    cos_f = jnp.concatenate([cos, cos, jnp.full((seq, pad), fill_cos, jnp.float32)], axis=1)
    sin_f = jnp.concatenate([-sin, sin, jnp.zeros((seq, pad), jnp.float32)], axis=1)
    return cos_f, sin_f


def _pad_cols(a, n):
    return jnp.pad(a, ((0, 0), (0, n - a.shape[1])))


def _pack_w_in(w_in):
    splits = np.cumsum((256, 256, ML_W, ML_W, 4 * ML_HEADS, MLA_Q_RANK, MLA_KV_RANK + MLA_ROPE,
                        DIL_W, DIL_W, DIL_W))[:-1].tolist()
    (wq, wk, wv, wo, wg, wcq, wckv, wdq, wdk, wdv) = jnp.split(w_in, splits, axis=1)
    packed = jnp.concatenate([wq, wk, wv, wo, _pad_cols(wg, LANES), wcq, _pad_cols(wckv, 256),
                              wdq, wdk, wdv], axis=1)
    return packed.astype(jnp.bfloat16)


def _layer(x2, p, tables, *, batch, seq):
    cosb, sinb, cosc, sinc, dil_bias = tables
    w = _pack_w_in(p['w_in'])
    wqb = p['mla_w_q_b'].reshape(MLA_Q_RANK, MLA_HEADS, MLA_NOPE + MLA_ROPE)
    wqb = jnp.pad(wqb, ((0, 0), (0, 0), (0, MLA_QK_PAD - MLA_NOPE - MLA_ROPE)))
    wqb = wqb.reshape(MLA_Q_RANK, MLA_HEADS * MLA_QK_PAD).astype(jnp.bfloat16)
    wkvb = p['mla_w_kv_b'].astype(jnp.bfloat16)
    gbias = jnp.concatenate([p['ml_i_bias'][0], p['ml_f_bias'][0], p['ml_i_bias'][1], p['ml_f_bias'][1]])
    gbias = _pad_cols(gbias[None, :], LANES)
    qh, kh = p['mla_q_head_norm'], p['mla_k_head_norm']
    hg = jnp.stack([qh[:MLA_NOPE], jnp.pad(qh[MLA_NOPE:], (0, LANES - MLA_ROPE)),
                    kh[:MLA_NOPE], jnp.pad(kh[MLA_NOPE:], (0, LANES - MLA_ROPE)),
                    p['dil_q_norm'], p['dil_k_norm'],
                    jnp.zeros((LANES,), jnp.float32), jnp.zeros((LANES,), jnp.float32)])

    (mlq, mlkt, mlv, mlo, gates, gates_t, qt, k, vt, dq, dk, dv) = _inproj(
        x2, p['norm_mix'][None, :], w, wqb, wkvb, gbias, p['mla_q_norm'][None, :],
        p['mla_kv_norm'][None, :], hg, cosb, sinb, cosc, sinc, seq=seq, tm=MLA_TILE)

    chunk = min(256, seq)
    h_bwd = _mlstm(mlq, mlkt, mlv, gates, gates_t, None, None, None, batch=batch, seq=seq, chunk=chunk,
                   reverse=True)
    ya = _mlstm(mlq, mlkt, mlv, gates, gates_t, h_bwd, mlo, p['ml_out_norm'], batch=batch, seq=seq,
                chunk=chunk, reverse=False)
    yb = _mla_attention(qt, k, vt, batch=batch, seq=seq, unroll=min(16, seq // MLA_TILE))
    yc = _dil_attention(dq, dk, dv, dil_bias, batch=batch, seq=seq)

    x2 = _outproj(x2, ya, yb, yc, p['w_out'].astype(jnp.bfloat16), tm=min(512, seq))
    return _ffn(x2, p['norm_ff'][None, :], p['w_ff1'].astype(jnp.bfloat16), p['w_ff2'].astype(jnp.bfloat16),
                tm=min(512, seq), tf=1024)


def kernel(x, norm_mix, w_in, ml_i_bias, ml_f_bias, ml_out_norm, mla_q_norm, mla_w_q_b, mla_kv_norm,
           mla_w_kv_b, mla_q_head_norm, mla_k_head_norm, dil_q_norm, dil_k_norm, w_out, norm_ff,
           w_ff1, w_ff2):
    batch, seq, d = x.shape
    assert seq % (DIL_SUB * DIL_TILE) == 0
    params = dict(norm_mix=norm_mix, w_in=w_in, ml_i_bias=ml_i_bias, ml_f_bias=ml_f_bias,
                  ml_out_norm=ml_out_norm, mla_q_norm=mla_q_norm, mla_w_q_b=mla_w_q_b,
                  mla_kv_norm=mla_kv_norm, mla_w_kv_b=mla_w_kv_b, mla_q_head_norm=mla_q_head_norm,
                  mla_k_head_norm=mla_k_head_norm, dil_q_norm=dil_q_norm, dil_k_norm=dil_k_norm,
                  w_out=w_out, norm_ff=norm_ff, w_ff1=w_ff1, w_ff2=w_ff2)
    cosb, sinb = _rope_tables(seq, MLA_ROPE, 0.0)
    cosc, sinc = _rope_tables(seq, DIL_ROT, 1.0)
    tables = (cosb, sinb, cosc, sinc, jnp.asarray(_dil_bias_table()))
    x2 = x.reshape(batch * seq, d)
    for layer in range(norm_mix.shape[0]):
        x2 = _layer(x2, {name: val[layer] for name, val in params.items()}, tables, batch=batch, seq=seq)
    return x2.reshape(batch, seq, d)
```

```python
import functools
import math

import numpy as np
import jax
import jax.numpy as jnp
from jax import lax
from jax.experimental import pallas as pl
from jax.experimental.pallas import tpu as pltpu

EPS = 1e-6
NEG_INF = -1e30
ROPE_THETA = 500000.0

ML_HEADS = 4
ML_DK = 64
ML_DV = 128
ML_W = ML_HEADS * ML_DV

MLA_HEADS = 6
MLA_Q_RANK = 384
MLA_KV_RANK = 128
MLA_NOPE = 128
MLA_ROPE = 64
MLA_V = 128
MLA_W = MLA_HEADS * MLA_V
MLA_QK_PAD = 256
MLA_VT_ROWS = 144
MLA_TILE = 512

DIL_HEADS = 6
DIL_DH = 128
DIL_ROT = DIL_DH // 4
DIL_PAIRS = ((128, 1), (512, 4), (2048, 16))
DIL_W = DIL_HEADS * DIL_DH

LANES = 128
VMEM_LIMIT_BYTES = 60 * 1024 * 1024

_C_MLQ = 0
_C_MLK = 256
_C_MLV = 512
_C_MLO = 1024
_C_GATE = 1536
_C_CQ = 1664
_C_CKV = 2048
_C_DQ = 2304
_C_DK = 3072
_C_DV = 3840
_C_END = 4608


def _rms(x, gain, n):
    ms = jnp.sum(x * x, axis=-1, keepdims=True) * (1.0 / n)
    return x * lax.rsqrt(ms + EPS) * gain


def _rope_tile(x, cos_f, sin_f, half):
    lane = lax.broadcasted_iota(jnp.int32, x.shape, 1)
    sw = jnp.where(lane < half, pltpu.roll(x, LANES - half, 1), pltpu.roll(x, half, 1))
    return x * cos_f + sw * sin_f


def _log_sigmoid(x):
    return jnp.minimum(x, 0.0) - jnp.log1p(jnp.exp(-jnp.abs(x)))


def _inproj_kernel(x_ref, gmix_ref, w_ref, wqb_ref, wkvb_ref, gbias_ref, qlat_ref, kvlat_ref,
                   hg_ref, cosb_ref, sinb_ref, cosc_ref, sinc_ref,
                   mlq_ref, mlkt_ref, mlv_ref, mlo_ref, gate_ref, gatet_ref, qt_ref, k_ref, vt_ref,
                   dqt_ref, dk_ref, dvt_ref, *, mla_scale, dil_scale):
    x = x_ref[...]
    h = _rms(x, gmix_ref[...], x.shape[-1]).astype(jnp.bfloat16)

    def proj(c0, c1):
        return jnp.dot(h, w_ref[:, c0:c1], preferred_element_type=jnp.float32)

    cos_b = cosb_ref[...]
    sin_b = sinb_ref[...]
    cos_c = cosc_ref[...]
    sin_c = sinc_ref[...]
    qg_n = hg_ref[0:1, :]
    qg_r = hg_ref[1:2, :]
    kg_n = hg_ref[2:3, :]
    kg_r = hg_ref[3:4, :]
    dqg = hg_ref[4:5, :]
    dkg = hg_ref[5:6, :]

    cq = _rms(proj(_C_CQ, _C_CKV), qlat_ref[...], MLA_Q_RANK).astype(jnp.bfloat16)
    zc = proj(_C_CKV, _C_DQ)
    ckv = _rms(zc[:, :MLA_KV_RANK], kvlat_ref[...], MLA_KV_RANK).astype(jnp.bfloat16)
    zdq = proj(_C_DQ, _C_DK)
    zdk = proj(_C_DK, _C_DV)
    zq = jnp.dot(cq, wqb_ref[...], preferred_element_type=jnp.float32)
    zkv = jnp.dot(ckv, wkvb_ref[...], preferred_element_type=jnp.float32)

    k_rope = _rope_tile(_rms(zc[:, MLA_KV_RANK:], kg_r, MLA_ROPE), cos_b, sin_b, MLA_ROPE // 2)
    k_rope = k_rope.astype(jnp.bfloat16)
    pad_rows = MLA_VT_ROWS - MLA_V

    def ones_row_tile(n):
        first = lax.broadcasted_iota(jnp.int32, (pad_rows, n), 0) == 0
        return jnp.where(first, 1.0, 0.0).astype(jnp.bfloat16)

    ones_row = ones_row_tile(x.shape[0])
    ones_row_dil = ones_row_tile(DIL_TILE)
    for hd in range(MLA_HEADS):
        c = hd * MLA_QK_PAD
        q_nope = _rms(zq[:, c:c + MLA_NOPE], qg_n, MLA_NOPE)
        q_rope = _rope_tile(_rms(zq[:, c + MLA_NOPE:c + MLA_QK_PAD], qg_r, MLA_ROPE),
                            cos_b, sin_b, MLA_ROPE // 2)
        qt_ref[hd, 0, 0:MLA_NOPE, :] = (q_nope * mla_scale).T.astype(jnp.bfloat16)
        qt_ref[hd, 0, MLA_NOPE:MLA_QK_PAD, :] = (q_rope * mla_scale).T.astype(jnp.bfloat16)
        k_nope = _rms(zkv[:, c:c + MLA_NOPE], kg_n, MLA_NOPE)
        k_ref[hd, :, 0:MLA_NOPE] = k_nope.astype(jnp.bfloat16)
        k_ref[hd, :, MLA_NOPE:MLA_QK_PAD] = k_rope
        vt_ref[hd, 0, 0:MLA_V, :] = zkv[:, c + MLA_NOPE:c + MLA_QK_PAD].T.astype(jnp.bfloat16)
        vt_ref[hd, 0, MLA_V:MLA_VT_ROWS, :] = ones_row

    mlkt_ref[...] = proj(_C_MLK, _C_MLV).T.astype(jnp.bfloat16)
    mlo_ref[...] = jax.nn.sigmoid(proj(_C_MLO, _C_GATE)).astype(jnp.bfloat16)
    g = proj(_C_GATE, _C_CQ) + gbias_ref[...]
    lane = lax.broadcasted_iota(jnp.int32, g.shape, 1)
    is_forget = (lane % 8) >= 4
    g = jnp.where(is_forget, _log_sigmoid(g), g)
    gate_ref[...] = g
    gatet_ref[...] = g.T

    for hd in range(DIL_HEADS):
        c = hd * DIL_DH
        qh = _rope_tile(_rms(zdq[:, c:c + DIL_DH], dqg, DIL_DH), cos_c, sin_c, DIL_ROT // 2)
        kh = _rope_tile(_rms(zdk[:, c:c + DIL_DH], dkg, DIL_DH), cos_c, sin_c, DIL_ROT // 2)
        qh = qh * dil_scale
        for sb in range(x.shape[0] // DIL_TILE):
            dqt_ref[hd, sb] = qh[sb * DIL_TILE:(sb + 1) * DIL_TILE].T.astype(jnp.bfloat16)
        dk_ref[:, c:c + DIL_DH] = kh.astype(jnp.bfloat16)
    zdv = proj(_C_DV, _C_END)
    for hd in range(DIL_HEADS):
        c = hd * DIL_DH
        for sb in range(x.shape[0] // DIL_TILE):
            rows = slice(sb * DIL_TILE, (sb + 1) * DIL_TILE)
            dvt_ref[hd, sb, 0:DIL_DH, :] = zdv[rows, c:c + DIL_DH].T.astype(jnp.bfloat16)
            dvt_ref[hd, sb, DIL_DH:MLA_VT_ROWS, :] = ones_row_dil

    mlq_ref[...] = (proj(_C_MLQ, _C_MLK) * (ML_DK ** -0.5)).astype(jnp.bfloat16)
    mlv_ref[...] = proj(_C_MLV, _C_MLO).astype(jnp.bfloat16)


def _const_spec(shape):
    n = len(shape)
    return pl.BlockSpec(shape, lambda *_: (0,) * n, pipeline_mode=pl.Buffered(1))


def _inproj(x2, gmix, w, wqb, wkvb, gbias, qlat, kvlat, hg, cosb, sinb, cosc, sinc, *, seq, tm):
    t, d = x2.shape
    nt = t // tm
    ns = seq // tm
    tok = lambda n: pl.BlockSpec((tm, n), lambda i: (i, 0))
    pos = pl.BlockSpec((tm, LANES), lambda i: (i % ns, 0))
    headed = lambda n: pl.BlockSpec((MLA_HEADS, tm, n), lambda i: (0, i, 0))
    headed_t = lambda n: pl.BlockSpec((MLA_HEADS, 1, n, tm), lambda i: (0, i, 0, 0))
    bf = jnp.bfloat16
    out_shape = (
        jax.ShapeDtypeStruct((t, 256), bf), jax.ShapeDtypeStruct((256, t), bf),
        jax.ShapeDtypeStruct((t, ML_W), bf), jax.ShapeDtypeStruct((t, ML_W), bf),
        jax.ShapeDtypeStruct((t, LANES), jnp.float32), jax.ShapeDtypeStruct((LANES, t), jnp.float32),
        jax.ShapeDtypeStruct((MLA_HEADS, nt, MLA_QK_PAD, tm), bf),
        jax.ShapeDtypeStruct((MLA_HEADS, t, MLA_QK_PAD), bf),
        jax.ShapeDtypeStruct((MLA_HEADS, nt, MLA_VT_ROWS, tm), bf),
        jax.ShapeDtypeStruct((DIL_HEADS, t // DIL_TILE, DIL_DH, DIL_TILE), bf),
        jax.ShapeDtypeStruct((t, DIL_W), bf),
        jax.ShapeDtypeStruct((DIL_HEADS, t // DIL_TILE, MLA_VT_ROWS, DIL_TILE), bf),
    )
    dil_t = lambda n: pl.BlockSpec((DIL_HEADS, tm // DIL_TILE, n, DIL_TILE), lambda i: (0, i, 0, 0))
    tok_t = lambda n: pl.BlockSpec((n, tm), lambda i: (0, i))
    out_specs = (tok(256), tok_t(256), tok(ML_W), tok(ML_W), tok(LANES), tok_t(LANES),
                 headed_t(MLA_QK_PAD), headed(MLA_QK_PAD), headed_t(MLA_VT_ROWS),
                 dil_t(DIL_DH), tok(DIL_W), dil_t(MLA_VT_ROWS))
    in_specs = [tok(d), _const_spec(gmix.shape), _const_spec(w.shape), _const_spec(wqb.shape),
                _const_spec(wkvb.shape), _const_spec(gbias.shape), _const_spec(qlat.shape),
                _const_spec(kvlat.shape), _const_spec(hg.shape), pos, pos, pos, pos]
    kern = functools.partial(_inproj_kernel,
                             mla_scale=(MLA_NOPE + MLA_ROPE) ** -0.5 * math.log2(math.e),
                             dil_scale=DIL_DH ** -0.5 * math.log2(math.e))
    return pl.pallas_call(
        kern, grid=(nt,), in_specs=in_specs, out_specs=out_specs, out_shape=out_shape,
        compiler_params=pltpu.CompilerParams(dimension_semantics=("arbitrary",),
                                             vmem_limit_bytes=VMEM_LIMIT_BYTES),
        name="inproj",
    )(x2, gmix, w, wqb, wkvb, gbias, qlat, kvlat, hg, cosb, sinb, cosc, sinc)


def _mlstm_kernel(*refs, reverse, finalize, chunk):
    if finalize:
        (q_ref, kt_ref, v_ref, gate_ref, gatet_ref, hprev_ref, o_ref, onorm_ref,
         out_ref, ct_ref, m_ref) = refs
    else:
        (q_ref, kt_ref, v_ref, gate_ref, gatet_ref, out_ref, ct_ref, m_ref) = refs
    L = chunk
    c = pl.program_id(1)

    @pl.when(c == 0)
    def _():
        ct_ref[...] = jnp.zeros_like(ct_ref)
        m_ref[...] = jnp.zeros_like(m_ref)

    row = lax.broadcasted_iota(jnp.int32, (L, L), 0)
    col = lax.broadcasted_iota(jnp.int32, (L, L), 1)
    causal = (col >= row) if reverse else (col <= row)

    d0 = 8 if reverse else 0
    gates = gate_ref[...]
    slab = gatet_ref[d0:d0 + 8, :]
    def split3(x):
        hi = x.astype(jnp.bfloat16)
        rest = x - hi.astype(jnp.float32)
        mid = rest.astype(jnp.bfloat16)
        return hi, mid, (rest - mid.astype(jnp.float32)).astype(jnp.bfloat16)

    vis = causal.astype(jnp.bfloat16)
    vis_t = ((row >= col) if reverse else (row <= col)).astype(jnp.bfloat16)
    cum_t = sum(jnp.dot(vis, part, preferred_element_type=jnp.float32)
                for part in split3(gates))
    cum = sum(jnp.dot(part, vis_t, preferred_element_type=jnp.float32)
              for part in split3(slab))

    def rep(col):
        return jnp.broadcast_to(col, (L, LANES))

    def wide(xb, n):
        return xb if n == LANES else jnp.concatenate([xb] * (n // LANES), axis=1)

    lane = lax.broadcasted_iota(jnp.int32, (L, LANES), 1)
    ones_col = jnp.where(lane == 0, 1.0, 0.0).astype(jnp.bfloat16)
    zero_half = jnp.zeros((ML_DK, L), jnp.bfloat16)

    for hd in range(ML_HEADS):
        i_b = rep(gates[:, d0 + hd:d0 + hd + 1])
        i_row = slab[hd:hd + 1, :]
        cum_b = rep(cum_t[:, d0 + 4 + hd:d0 + 5 + hd])
        cum_row = cum[4 + hd:5 + hd, :]
        tot = cum_row[:, 0:1] if reverse else cum_row[:, L - 1:L]
        m_prev = m_ref[hd:hd + 1, 0:1]

        pair, half = hd // 2, hd % 2
        in_head = (lane >= half * ML_DK) & (lane < (half + 1) * ML_DK)
        qp = q_ref[:, pair * LANES:(pair + 1) * LANES]
        qm = jnp.where(in_head, qp, jnp.zeros_like(qp))
        kt_pair = kt_ref[pair * LANES:(pair + 1) * LANES, :]
        kt_h = kt_ref[pair * LANES + half * ML_DK:pair * LANES + (half + 1) * ML_DK, :]
        kmt = jnp.concatenate([kt_h, zero_half] if half == 0 else [zero_half, kt_h], axis=0)
        v_h = v_ref[:, hd * ML_DV:(hd + 1) * ML_DV]
        v_aug = jnp.concatenate([v_h, ones_col], axis=1)

        d_mat = jnp.where(causal, wide(cum_b, L) + (i_row - cum_row), NEG_INF)
        d_inter = cum_b + m_prev
        m_t = jnp.maximum(d_inter, rep(jnp.max(d_mat, axis=1, keepdims=True)))
        w_intra = jnp.exp(d_mat - wide(m_t, L))
        w_inter = jnp.exp(d_inter - m_t)
        s_raw = jnp.dot(qm, kt_pair, preferred_element_type=jnp.float32)
        sw = (s_raw * w_intra).astype(jnp.bfloat16)
        ct = ct_ref[hd]
        r = (jnp.dot(sw, v_aug, preferred_element_type=jnp.float32)
             + wide(w_inter, 2 * LANES) * jnp.dot(qm, ct.astype(jnp.bfloat16),
                                                  preferred_element_type=jnp.float32))
        num = r[:, :ML_DV]
        den = rep(r[:, ML_DV:ML_DV + 1])
        h_dir = num / jnp.maximum(jnp.abs(den), jnp.exp(-m_t))

        d_state = tot - cum_b + i_b
        m_new = jnp.maximum(tot + m_prev, jnp.max(d_state, axis=0, keepdims=True)[:, 0:1])
        w_s = jnp.exp(d_state - m_new)
        w_c = jnp.exp(tot + m_prev - m_new)
        vw = (v_aug.astype(jnp.float32) * wide(w_s, 2 * LANES)).astype(jnp.bfloat16)
        ct_ref[hd] = w_c * ct + jnp.dot(kmt, vw, preferred_element_type=jnp.float32)
        m_ref[hd:hd + 1, :] = jnp.broadcast_to(m_new, (1, LANES))

        sl = slice(hd * ML_DV, (hd + 1) * ML_DV)
        if finalize:
            h_sum = h_dir + hprev_ref[:, sl]
            y = _rms(h_sum, onorm_ref[hd:hd + 1, :], ML_DV)
            out_ref[:, sl] = (o_ref[:, sl].astype(jnp.float32) * y).astype(out_ref.dtype)
        else:
            out_ref[:, sl] = h_dir


def _mlstm(mlq, mlkt, mlv, gates, gates_t, hprev, mlo, onorm, *, batch, seq, chunk, reverse):
    t = mlq.shape[0]
    nc = seq // chunk
    finalize = hprev is not None
    if reverse:
        blk = lambda b, c: b * nc + (nc - 1 - c)
    else:
        blk = lambda b, c: b * nc + c
    tok = lambda n: pl.BlockSpec((chunk, n), lambda b, c: (blk(b, c), 0))
    tok_t = lambda n: pl.BlockSpec((n, chunk), lambda b, c: (0, blk(b, c)))
    in_specs = [tok(256), tok_t(256), tok(ML_W), tok(LANES), tok_t(LANES)]
    args = [mlq, mlkt, mlv, gates, gates_t]
    if finalize:
        in_specs += [tok(ML_W), tok(ML_W), pl.BlockSpec(onorm.shape, lambda b, c: (0, 0))]
        args += [hprev, mlo, onorm]
        out_dtype = jnp.bfloat16
    else:
        out_dtype = jnp.float32
    kern = functools.partial(_mlstm_kernel, reverse=reverse, finalize=finalize, chunk=chunk)
    return pl.pallas_call(
        kern, grid=(batch, nc), in_specs=in_specs, out_specs=tok(ML_W),
        out_shape=jax.ShapeDtypeStruct((t, ML_W), out_dtype),
        scratch_shapes=[pltpu.VMEM((ML_HEADS, LANES, 2 * LANES), jnp.float32),
                        pltpu.VMEM((8, LANES), jnp.float32)],
        compiler_params=pltpu.CompilerParams(dimension_semantics=("arbitrary", "arbitrary"),
                                             vmem_limit_bytes=VMEM_LIMIT_BYTES),
        name="mlstm_bwd" if reverse else "mlstm_fwd",
    )(*args)


def _mla_kernel(qt_ref, k_ref, vt_ref, o_ref, s_scr, acc_scr, *, unroll):
    qt = qt_ref[...]
    tq = qt.shape[1]
    tk = vt_ref.shape[2]
    nk = vt_ref.shape[0]

    def scores(j):
        start = pl.multiple_of(j * tk, tk)
        return jnp.dot(k_ref[pl.ds(start, tk), :], qt, preferred_element_type=jnp.float32)

    def consume(slot, j, m_prev):
        m_new = jnp.maximum(m_prev, jnp.max(s_scr[slot], axis=0, keepdims=True))
        alpha = jnp.exp2(m_prev - m_new)
        p = jnp.exp2(s_scr[slot] - m_new).astype(jnp.bfloat16)
        acc_scr[...] = alpha * acc_scr[...] + jnp.dot(vt_ref[j], p, preferred_element_type=jnp.float32)
        return m_new

    s_scr[0] = scores(0)
    acc_scr[...] = jnp.zeros_like(acc_scr)

    def body(jj, m):
        j = unroll * jj
        for u in range(unroll):
            s_scr[(u + 1) % 2] = scores(jnp.minimum(j + u + 1, nk - 1))
            m = consume(u % 2, j + u, m)
        return m

    lax.fori_loop(0, nk // unroll, body, jnp.full((1, tq), NEG_INF, jnp.float32))
    acc = acc_scr[...]
    o_ref[...] = (acc[:MLA_V] / acc[MLA_V:MLA_V + 1]).T.astype(o_ref.dtype)


def _mla_attention(qt, k, vt, *, batch, seq, unroll):
    tile = qt.shape[-1]
    t = k.shape[1]
    nq = seq // tile
    assert unroll % 2 == 0 and nq % unroll == 0
    return pl.pallas_call(
        functools.partial(_mla_kernel, unroll=unroll),
        grid=(batch, MLA_HEADS, nq),
        in_specs=[pl.BlockSpec((None, None, MLA_QK_PAD, tile), lambda b, h, i: (h, b * nq + i, 0, 0)),
                  pl.BlockSpec((None, seq, MLA_QK_PAD), lambda b, h, i: (h, b, 0)),
                  pl.BlockSpec((None, nq, MLA_VT_ROWS, tile), lambda b, h, i: (h, b, 0, 0))],
        out_specs=pl.BlockSpec((tile, MLA_V), lambda b, h, i: (b * nq + i, h)),
        out_shape=jax.ShapeDtypeStruct((t, MLA_W), jnp.bfloat16),
        scratch_shapes=[pltpu.VMEM((2, tile, tile), jnp.float32),
                        pltpu.VMEM((MLA_VT_ROWS, tile), jnp.float32)],
        compiler_params=pltpu.CompilerParams(
            dimension_semantics=("arbitrary", "arbitrary", "arbitrary"),
            vmem_limit_bytes=VMEM_LIMIT_BYTES),
        name="mla_attn",
    )(qt, k, vt)


DIL_TILE = 256
DIL_REACH = max(w // 2 for w, _ in DIL_PAIRS)
DIL_NOFF = DIL_REACH // DIL_TILE
DIL_NWIN = 2 * DIL_NOFF + 1
DIL_SUB = 4


def _dil_bias_table():
    r = np.arange(DIL_TILE)[:, None]
    c = np.arange(DIL_TILE)[None, :]
    tiles = []
    for o in range(-2 * DIL_NOFF, 2 * DIL_NOFF + 1):
        delta = o * DIL_TILE + r - c
        mult = np.zeros_like(delta)
        for window, dil in DIL_PAIRS:
            mult += ((delta % dil) == 0) & (np.abs(delta) <= window // 2)
        tiles.append(np.where(mult > 0, np.log2(np.maximum(mult, 1)), NEG_INF))
    return np.stack(tiles).astype(np.float32)


def _dil_kernel(qt_ref, k_ref, vt_ref, bias_ref, o_ref, s_scr, *, nq, nwin):
    i = pl.program_id(2)
    subs = range(DIL_SUB)
    qi = [i * DIL_SUB + sub for sub in subs]
    w0 = [jnp.clip(t - DIL_NOFF, 0, nq - nwin) for t in qi]

    def score_pass(sub):
        m = None
        for idx in range(nwin):
            start = pl.multiple_of((w0[sub] + idx) * DIL_TILE, DIL_TILE)
            bias = bias_ref[w0[sub] + idx - qi[sub] + 2 * DIL_NOFF]
            s = jnp.dot(k_ref[pl.ds(start, DIL_TILE), :], qt_ref[sub],
                        preferred_element_type=jnp.float32) + bias
            s_scr[sub, idx * DIL_TILE:(idx + 1) * DIL_TILE, :] = s
            m_cur = jnp.max(s, axis=0, keepdims=True)
            m = m_cur if idx == 0 else jnp.maximum(m, m_cur)
        return m

    def value_pass(sub, m):
        acc = None
        for idx in range(nwin):
            p = jnp.exp2(s_scr[sub, idx * DIL_TILE:(idx + 1) * DIL_TILE, :] - m)
            part = jnp.dot(vt_ref[w0[sub] + idx], p.astype(jnp.bfloat16),
                           preferred_element_type=jnp.float32)
            acc = part if acc is None else acc + part
        out_t = acc[:DIL_DH] / acc[DIL_DH:DIL_DH + 1]
        o_ref[sub * DIL_TILE:(sub + 1) * DIL_TILE, :] = out_t.T.astype(o_ref.dtype)

    m_prev = score_pass(0)
    for sub in range(1, DIL_SUB):
        m_next = score_pass(sub)
        value_pass(sub - 1, m_prev)
        m_prev = m_next
    value_pass(DIL_SUB - 1, m_prev)


def _dil_attention(dqt, dk, dvt, bias, *, batch, seq):
    t = dk.shape[0]
    nq = seq // DIL_TILE
    nwin = min(DIL_NWIN, nq)
    tq = DIL_SUB * DIL_TILE
    nstep = seq // tq
    return pl.pallas_call(
        functools.partial(_dil_kernel, nq=nq, nwin=nwin),
        grid=(batch, DIL_HEADS, nstep),
        in_specs=[pl.BlockSpec((None, DIL_SUB, DIL_DH, DIL_TILE), lambda b, h, i: (h, b * nstep + i, 0, 0)),
                  pl.BlockSpec((seq, DIL_DH), lambda b, h, i: (b, h)),
                  pl.BlockSpec((None, nq, MLA_VT_ROWS, DIL_TILE), lambda b, h, i: (h, b, 0, 0)),
                  _const_spec(bias.shape)],
        out_specs=pl.BlockSpec((tq, DIL_DH), lambda b, h, i: (b * nstep + i, h)),
        out_shape=jax.ShapeDtypeStruct((t, DIL_W), jnp.bfloat16),
        scratch_shapes=[pltpu.VMEM((DIL_SUB, nwin * DIL_TILE, DIL_TILE), jnp.float32)],
        compiler_params=pltpu.CompilerParams(
            dimension_semantics=("arbitrary", "arbitrary", "arbitrary"),
            vmem_limit_bytes=VMEM_LIMIT_BYTES),
        name="dil_attn",
    )(dqt, dk, dvt, bias)


def _outproj_kernel(x_ref, ya_ref, yb_ref, yc_ref, w_ref, o_ref):
    b0, b1 = ML_W, ML_W + MLA_W
    y = (jnp.dot(ya_ref[...], w_ref[0:b0, :], preferred_element_type=jnp.float32)
         + jnp.dot(yb_ref[...], w_ref[b0:b1, :], preferred_element_type=jnp.float32)
         + jnp.dot(yc_ref[...], w_ref[b1:, :], preferred_element_type=jnp.float32))
    o_ref[...] = x_ref[...] + y


def _outproj(x2, ya, yb, yc, w, *, tm):
    t, d = x2.shape
    tok = lambda n: pl.BlockSpec((tm, n), lambda i: (i, 0))
    return pl.pallas_call(
        _outproj_kernel, grid=(t // tm,),
        in_specs=[tok(d), tok(ML_W), tok(MLA_W), tok(DIL_W), _const_spec(w.shape)],
        out_specs=tok(d), out_shape=jax.ShapeDtypeStruct((t, d), jnp.float32),
        compiler_params=pltpu.CompilerParams(dimension_semantics=("arbitrary",),
                                             vmem_limit_bytes=VMEM_LIMIT_BYTES),
        name="outproj",
    )(x2, ya, yb, yc, w)


def _ffn_kernel(x_ref, g_ref, w1_ref, w2_ref, o_ref, h_ref):
    j = pl.program_id(1)

    @pl.when(j == 0)
    def _():
        x = x_ref[...]
        h_ref[...] = _rms(x, g_ref[...], x.shape[-1]).astype(h_ref.dtype)
        o_ref[...] = x

    u = jnp.maximum(jnp.dot(h_ref[...], w1_ref[...], preferred_element_type=jnp.float32), 0.0)
    u = (u * u).astype(jnp.bfloat16)
    o_ref[...] += jnp.dot(u, w2_ref[...], preferred_element_type=jnp.float32)


def _ffn(x2, g, w1, w2, *, tm, tf):
    t, d = x2.shape
    dff = w1.shape[1]
    return pl.pallas_call(
        _ffn_kernel, grid=(t // tm, dff // tf),
        in_specs=[pl.BlockSpec((tm, d), lambda i, j: (i, 0)),
                  pl.BlockSpec(g.shape, lambda i, j: (0, 0)),
                  pl.BlockSpec((d, tf), lambda i, j: (0, j)),
                  pl.BlockSpec((tf, d), lambda i, j: (j, 0))],
        out_specs=pl.BlockSpec((tm, d), lambda i, j: (i, 0)),
        out_shape=jax.ShapeDtypeStruct((t, d), jnp.float32),
        scratch_shapes=[pltpu.VMEM((tm, d), jnp.bfloat16)],
        compiler_params=pltpu.CompilerParams(dimension_semantics=("arbitrary", "arbitrary"),
                                             vmem_limit_bytes=VMEM_LIMIT_BYTES),
        name="ffn",
    )(x2, g, w1, w2)


def _rope_tables(seq, rot_dim, fill_cos):
    pos = jnp.arange(seq, dtype=jnp.float32)
    inv_freq = ROPE_THETA ** (-jnp.arange(0, rot_dim, 2, dtype=jnp.float32) / rot_dim)
    ang = pos[:, None] * inv_freq[None, :]
    cos, sin = jnp.cos(ang), jnp.sin(ang)
    pad = LANES - rot_dim
    cos_f = jnp.concatenate([cos, cos, jnp.full((seq, pad), fill_cos, jnp.float32)], axis=1)
    sin_f = jnp.concatenate([-sin, sin, jnp.zeros((seq, pad), jnp.float32)], axis=1)
    return cos_f, sin_f


def _pad_cols(a, n):
    return jnp.pad(a, ((0, 0), (0, n - a.shape[1])))


def _pack_w_in(w_in):
    splits = np.cumsum((256, 256, ML_W, ML_W, 4 * ML_HEADS, MLA_Q_RANK, MLA_KV_RANK + MLA_ROPE,
                        DIL_W, DIL_W, DIL_W))[:-1].tolist()
    (wq, wk, wv, wo, wg, wcq, wckv, wdq, wdk, wdv) = jnp.split(w_in, splits, axis=1)
    packed = jnp.concatenate([wq, wk, wv, wo, _pad_cols(wg, LANES), wcq, _pad_cols(wckv, 256),
                              wdq, wdk, wdv], axis=1)
    return packed.astype(jnp.bfloat16)


def _layer(x2, p, tables, *, batch, seq):
    cosb, sinb, cosc, sinc, dil_bias = tables
    w = _pack_w_in(p['w_in'])
    wqb = p['mla_w_q_b'].reshape(MLA_Q_RANK, MLA_HEADS, MLA_NOPE + MLA_ROPE)
    wqb = jnp.pad(wqb, ((0, 0), (0, 0), (0, MLA_QK_PAD - MLA_NOPE - MLA_ROPE)))
    wqb = wqb.reshape(MLA_Q_RANK, MLA_HEADS * MLA_QK_PAD).astype(jnp.bfloat16)
    wkvb = p['mla_w_kv_b'].astype(jnp.bfloat16)
    gbias = jnp.concatenate([p['ml_i_bias'][0], p['ml_f_bias'][0], p['ml_i_bias'][1], p['ml_f_bias'][1]])
    gbias = _pad_cols(gbias[None, :], LANES)
    qh, kh = p['mla_q_head_norm'], p['mla_k_head_norm']
    hg = jnp.stack([qh[:MLA_NOPE], jnp.pad(qh[MLA_NOPE:], (0, LANES - MLA_ROPE)),
                    kh[:MLA_NOPE], jnp.pad(kh[MLA_NOPE:], (0, LANES - MLA_ROPE)),
                    p['dil_q_norm'], p['dil_k_norm'],
                    jnp.zeros((LANES,), jnp.float32), jnp.zeros((LANES,), jnp.float32)])

    (mlq, mlkt, mlv, mlo, gates, gates_t, qt, k, vt, dqt, dk, dvt) = _inproj(
        x2, p['norm_mix'][None, :], w, wqb, wkvb, gbias, p['mla_q_norm'][None, :],
        p['mla_kv_norm'][None, :], hg, cosb, sinb, cosc, sinc, seq=seq, tm=MLA_TILE)

    chunk = min(256, seq)
    h_bwd = _mlstm(mlq, mlkt, mlv, gates, gates_t, None, None, None, batch=batch, seq=seq, chunk=chunk,
                   reverse=True)
    ya = _mlstm(mlq, mlkt, mlv, gates, gates_t, h_bwd, mlo, p['ml_out_norm'], batch=batch, seq=seq,
                chunk=chunk, reverse=False)
    yb = _mla_attention(qt, k, vt, batch=batch, seq=seq, unroll=min(16, seq // MLA_TILE))
    yc = _dil_attention(dqt, dk, dvt, dil_bias, batch=batch, seq=seq)

    x2 = _outproj(x2, ya, yb, yc, p['w_out'].astype(jnp.bfloat16), tm=min(512, seq))
    return _ffn(x2, p['norm_ff'][None, :], p['w_ff1'].astype(jnp.bfloat16), p['w_ff2'].astype(jnp.bfloat16),
                tm=min(512, seq), tf=1024)


def kernel(x, norm_mix, w_in, ml_i_bias, ml_f_bias, ml_out_norm, mla_q_norm, mla_w_q_b, mla_kv_norm,
           mla_w_kv_b, mla_q_head_norm, mla_k_head_norm, dil_q_norm, dil_k_norm, w_out, norm_ff,
           w_ff1, w_ff2):
    batch, seq, d = x.shape
    assert seq % (DIL_SUB * DIL_TILE) == 0
    params = dict(norm_mix=norm_mix, w_in=w_in, ml_i_bias=ml_i_bias, ml_f_bias=ml_f_bias,
                  ml_out_norm=ml_out_norm, mla_q_norm=mla_q_norm, mla_w_q_b=mla_w_q_b,
                  mla_kv_norm=mla_kv_norm, mla_w_kv_b=mla_w_kv_b, mla_q_head_norm=mla_q_head_norm,
                  mla_k_head_norm=mla_k_head_norm, dil_q_norm=dil_q_norm, dil_k_norm=dil_k_norm,
                  w_out=w_out, norm_ff=norm_ff, w_ff1=w_ff1, w_ff2=w_ff2)
    cosb, sinb = _rope_tables(seq, MLA_ROPE, 0.0)
    cosc, sinc = _rope_tables(seq, DIL_ROT, 1.0)
    tables = (cosb, sinb, cosc, sinc, jnp.asarray(_dil_bias_table()))
    x2 = x.reshape(batch * seq, d)
    for layer in range(norm_mix.shape[0]):
        x2 = _layer(x2, {name: val[layer] for name, val in params.items()}, tables, batch=batch, seq=seq)
    return x2.reshape(batch, seq, d)
```

```python
import functools
import math

import numpy as np
import jax
import jax.numpy as jnp
from jax import lax
from jax.experimental import pallas as pl
from jax.experimental.pallas import tpu as pltpu

EPS = 1e-6
NEG_INF = -1e30
ROPE_THETA = 500000.0

ML_HEADS = 4
ML_DK = 64
ML_DV = 128
ML_W = ML_HEADS * ML_DV

MLA_HEADS = 6
MLA_Q_RANK = 384
MLA_KV_RANK = 128
MLA_NOPE = 128
MLA_ROPE = 64
MLA_V = 128
MLA_W = MLA_HEADS * MLA_V
MLA_QK_PAD = 256
MLA_VT_ROWS = 144
MLA_TILE = 512

DIL_HEADS = 6
DIL_DH = 128
DIL_ROT = DIL_DH // 4
DIL_PAIRS = ((128, 1), (512, 4), (2048, 16))
DIL_W = DIL_HEADS * DIL_DH

LANES = 128
VMEM_LIMIT_BYTES = 60 * 1024 * 1024

_C_MLQ = 0
_C_MLK = 256
_C_MLV = 512
_C_MLO = 1024
_C_GATE = 1536
_C_CQ = 1664
_C_CKV = 2048
_C_DQ = 2304
_C_DK = 3072
_C_DV = 3840
_C_END = 4608


def _rms(x, gain, n):
    ms = jnp.sum(x * x, axis=-1, keepdims=True) * (1.0 / n)
    return x * lax.rsqrt(ms + EPS) * gain


def _rope_tile(x, cos_f, sin_f, half):
    lane = lax.broadcasted_iota(jnp.int32, x.shape, 1)
    sw = jnp.where(lane < half, pltpu.roll(x, LANES - half, 1), pltpu.roll(x, half, 1))
    return x * cos_f + sw * sin_f


def _log_sigmoid(x):
    return jnp.minimum(x, 0.0) - jnp.log1p(jnp.exp(-jnp.abs(x)))


def _inproj_kernel(x_ref, gmix_ref, w_ref, wqb_ref, wkvb_ref, gbias_ref, qlat_ref, kvlat_ref,
                   hg_ref, cosb_ref, sinb_ref, cosc_ref, sinc_ref,
                   mlq_ref, mlkt_ref, mlv_ref, mlo_ref, gate_ref, gatet_ref, qt_ref, k_ref, vt_ref,
                   dqt_ref, dk_ref, dvt_ref, *, mla_scale, dil_scale):
    x = x_ref[...]
    h = _rms(x, gmix_ref[...], x.shape[-1]).astype(jnp.bfloat16)

    def proj(c0, c1):
        return jnp.dot(h, w_ref[:, c0:c1], preferred_element_type=jnp.float32)

    cos_b = cosb_ref[...]
    sin_b = sinb_ref[...]
    cos_c = cosc_ref[...]
    sin_c = sinc_ref[...]
    qg_n = hg_ref[0:1, :]
    qg_r = hg_ref[1:2, :]
    kg_n = hg_ref[2:3, :]
    kg_r = hg_ref[3:4, :]
    dqg = hg_ref[4:5, :]
    dkg = hg_ref[5:6, :]

    cq = _rms(proj(_C_CQ, _C_CKV), qlat_ref[...], MLA_Q_RANK).astype(jnp.bfloat16)
    zc = proj(_C_CKV, _C_DQ)
    ckv = _rms(zc[:, :MLA_KV_RANK], kvlat_ref[...], MLA_KV_RANK).astype(jnp.bfloat16)
    zdq = proj(_C_DQ, _C_DK)
    zdk = proj(_C_DK, _C_DV)
    zq = jnp.dot(cq, wqb_ref[...], preferred_element_type=jnp.float32)
    zkv = jnp.dot(ckv, wkvb_ref[...], preferred_element_type=jnp.float32)

    k_rope = _rope_tile(_rms(zc[:, MLA_KV_RANK:], kg_r, MLA_ROPE), cos_b, sin_b, MLA_ROPE // 2)
    k_rope = k_rope.astype(jnp.bfloat16)
    pad_rows = MLA_VT_ROWS - MLA_V

    def ones_row_tile(n):
        first = lax.broadcasted_iota(jnp.int32, (pad_rows, n), 0) == 0
        return jnp.where(first, 1.0, 0.0).astype(jnp.bfloat16)

    ones_row = ones_row_tile(x.shape[0])
    ones_row_dil = ones_row_tile(DIL_TILE)
    for hd in range(MLA_HEADS):
        c = hd * MLA_QK_PAD
        q_nope = _rms(zq[:, c:c + MLA_NOPE], qg_n, MLA_NOPE)
        q_rope = _rope_tile(_rms(zq[:, c + MLA_NOPE:c + MLA_QK_PAD], qg_r, MLA_ROPE),
                            cos_b, sin_b, MLA_ROPE // 2)
        qt_ref[hd, 0, 0:MLA_NOPE, :] = (q_nope * mla_scale).T.astype(jnp.bfloat16)
        qt_ref[hd, 0, MLA_NOPE:MLA_QK_PAD, :] = (q_rope * mla_scale).T.astype(jnp.bfloat16)
        k_nope = _rms(zkv[:, c:c + MLA_NOPE], kg_n, MLA_NOPE)
        k_ref[hd, :, 0:MLA_NOPE] = k_nope.astype(jnp.bfloat16)
        k_ref[hd, :, MLA_NOPE:MLA_QK_PAD] = k_rope
        vt_ref[hd, 0, 0:MLA_V, :] = zkv[:, c + MLA_NOPE:c + MLA_QK_PAD].T.astype(jnp.bfloat16)
        vt_ref[hd, 0, MLA_V:MLA_VT_ROWS, :] = ones_row

    mlkt_ref[...] = proj(_C_MLK, _C_MLV).T.astype(jnp.bfloat16)
    mlo_ref[...] = jax.nn.sigmoid(proj(_C_MLO, _C_GATE)).astype(jnp.bfloat16)
    g = proj(_C_GATE, _C_CQ) + gbias_ref[...]
    lane = lax.broadcasted_iota(jnp.int32, g.shape, 1)
    is_forget = (lane % 8) >= 4
    g = jnp.where(is_forget, _log_sigmoid(g), g)
    gate_ref[...] = g
    gatet_ref[...] = g.T

    for hd in range(DIL_HEADS):
        c = hd * DIL_DH
        qh = _rope_tile(_rms(zdq[:, c:c + DIL_DH], dqg, DIL_DH), cos_c, sin_c, DIL_ROT // 2)
        kh = _rope_tile(_rms(zdk[:, c:c + DIL_DH], dkg, DIL_DH), cos_c, sin_c, DIL_ROT // 2)
        qh = qh * dil_scale
        for sb in range(x.shape[0] // DIL_TILE):
            dqt_ref[hd, sb] = qh[sb * DIL_TILE:(sb + 1) * DIL_TILE].T.astype(jnp.bfloat16)
        dk_ref[:, c:c + DIL_DH] = kh.astype(jnp.bfloat16)
    zdv = proj(_C_DV, _C_END)
    for hd in range(DIL_HEADS):
        c = hd * DIL_DH
        for sb in range(x.shape[0] // DIL_TILE):
            rows = slice(sb * DIL_TILE, (sb + 1) * DIL_TILE)
            dvt_ref[hd, sb, 0:DIL_DH, :] = zdv[rows, c:c + DIL_DH].T.astype(jnp.bfloat16)
            dvt_ref[hd, sb, DIL_DH:MLA_VT_ROWS, :] = ones_row_dil

    mlq_ref[...] = (proj(_C_MLQ, _C_MLK) * (ML_DK ** -0.5)).astype(jnp.bfloat16)
    mlv_ref[...] = proj(_C_MLV, _C_MLO).astype(jnp.bfloat16)


def _const_spec(shape):
    n = len(shape)
    return pl.BlockSpec(shape, lambda *_: (0,) * n, pipeline_mode=pl.Buffered(1))


def _inproj(x2, gmix, w, wqb, wkvb, gbias, qlat, kvlat, hg, cosb, sinb, cosc, sinc, *, seq, tm):
    t, d = x2.shape
    nt = t // tm
    ns = seq // tm
    tok = lambda n: pl.BlockSpec((tm, n), lambda i: (i, 0))
    pos = pl.BlockSpec((tm, LANES), lambda i: (i % ns, 0))
    headed = lambda n: pl.BlockSpec((MLA_HEADS, tm, n), lambda i: (0, i, 0))
    headed_t = lambda n: pl.BlockSpec((MLA_HEADS, 1, n, tm), lambda i: (0, i, 0, 0))
    bf = jnp.bfloat16
    out_shape = (
        jax.ShapeDtypeStruct((t, 256), bf), jax.ShapeDtypeStruct((256, t), bf),
        jax.ShapeDtypeStruct((t, ML_W), bf), jax.ShapeDtypeStruct((t, ML_W), bf),
        jax.ShapeDtypeStruct((t, LANES), jnp.float32), jax.ShapeDtypeStruct((LANES, t), jnp.float32),
        jax.ShapeDtypeStruct((MLA_HEADS, nt, MLA_QK_PAD, tm), bf),
        jax.ShapeDtypeStruct((MLA_HEADS, t, MLA_QK_PAD), bf),
        jax.ShapeDtypeStruct((MLA_HEADS, nt, MLA_VT_ROWS, tm), bf),
        jax.ShapeDtypeStruct((DIL_HEADS, t // DIL_TILE, DIL_DH, DIL_TILE), bf),
        jax.ShapeDtypeStruct((t, DIL_W), bf),
        jax.ShapeDtypeStruct((DIL_HEADS, t // DIL_TILE, MLA_VT_ROWS, DIL_TILE), bf),
    )
    dil_t = lambda n: pl.BlockSpec((DIL_HEADS, tm // DIL_TILE, n, DIL_TILE), lambda i: (0, i, 0, 0))
    tok_t = lambda n: pl.BlockSpec((n, tm), lambda i: (0, i))
    out_specs = (tok(256), tok_t(256), tok(ML_W), tok(ML_W), tok(LANES), tok_t(LANES),
                 headed_t(MLA_QK_PAD), headed(MLA_QK_PAD), headed_t(MLA_VT_ROWS),
                 dil_t(DIL_DH), tok(DIL_W), dil_t(MLA_VT_ROWS))
    in_specs = [tok(d), _const_spec(gmix.shape), _const_spec(w.shape), _const_spec(wqb.shape),
                _const_spec(wkvb.shape), _const_spec(gbias.shape), _const_spec(qlat.shape),
                _const_spec(kvlat.shape), _const_spec(hg.shape), pos, pos, pos, pos]
    kern = functools.partial(_inproj_kernel,
                             mla_scale=(MLA_NOPE + MLA_ROPE) ** -0.5 * math.log2(math.e),
                             dil_scale=DIL_DH ** -0.5 * math.log2(math.e))
    return pl.pallas_call(
        kern, grid=(nt,), in_specs=in_specs, out_specs=out_specs, out_shape=out_shape,
        compiler_params=pltpu.CompilerParams(dimension_semantics=("arbitrary",),
                                             vmem_limit_bytes=VMEM_LIMIT_BYTES),
        name="inproj",
    )(x2, gmix, w, wqb, wkvb, gbias, qlat, kvlat, hg, cosb, sinb, cosc, sinc)


def _mlstm_kernel(*refs, reverse, finalize, chunk):
    if finalize:
        (q_ref, kt_ref, v_ref, gate_ref, gatet_ref, hprev_ref, o_ref, onorm_ref,
         out_ref, ct_ref, m_ref) = refs
    else:
        (q_ref, kt_ref, v_ref, gate_ref, gatet_ref, out_ref, ct_ref, m_ref) = refs
    L = chunk
    c = pl.program_id(1)

    @pl.when(c == 0)
    def _():
        ct_ref[...] = jnp.zeros_like(ct_ref)
        m_ref[...] = jnp.zeros_like(m_ref)

    row = lax.broadcasted_iota(jnp.int32, (L, L), 0)
    col = lax.broadcasted_iota(jnp.int32, (L, L), 1)
    causal = (col >= row) if reverse else (col <= row)

    d0 = 8 if reverse else 0
    gates = gate_ref[...]
    slab = gatet_ref[d0:d0 + 8, :]
    def split3(x):
        hi = x.astype(jnp.bfloat16)
        rest = x - hi.astype(jnp.float32)
        mid = rest.astype(jnp.bfloat16)
        return hi, mid, (rest - mid.astype(jnp.float32)).astype(jnp.bfloat16)

    vis = causal.astype(jnp.bfloat16)
    vis_t = ((row >= col) if reverse else (row <= col)).astype(jnp.bfloat16)
    cum_t = sum(jnp.dot(vis, part, preferred_element_type=jnp.float32)
                for part in split3(gates))
    cum = sum(jnp.dot(part, vis_t, preferred_element_type=jnp.float32)
              for part in split3(slab))

    def rep(col):
        return jnp.broadcast_to(col, (L, LANES))

    def wide(xb, n):
        return xb if n == LANES else jnp.concatenate([xb] * (n // LANES), axis=1)

    lane = lax.broadcasted_iota(jnp.int32, (L, LANES), 1)
    ones_col = jnp.where(lane == 0, 1.0, 0.0).astype(jnp.bfloat16)
    zero_half = jnp.zeros((ML_DK, L), jnp.bfloat16)

    for hd in range(ML_HEADS):
        i_b = rep(gates[:, d0 + hd:d0 + hd + 1])
        i_row = slab[hd:hd + 1, :]
        cum_b = rep(cum_t[:, d0 + 4 + hd:d0 + 5 + hd])
        cum_row = cum[4 + hd:5 + hd, :]
        tot = cum_row[:, 0:1] if reverse else cum_row[:, L - 1:L]
        m_prev = m_ref[hd:hd + 1, 0:1]

        pair, half = hd // 2, hd % 2
        in_head = (lane >= half * ML_DK) & (lane < (half + 1) * ML_DK)
        qp = q_ref[:, pair * LANES:(pair + 1) * LANES]
        qm = jnp.where(in_head, qp, jnp.zeros_like(qp))
        kt_pair = kt_ref[pair * LANES:(pair + 1) * LANES, :]
        kt_h = kt_ref[pair * LANES + half * ML_DK:pair * LANES + (half + 1) * ML_DK, :]
        kmt = jnp.concatenate([kt_h, zero_half] if half == 0 else [zero_half, kt_h], axis=0)
        v_h = v_ref[:, hd * ML_DV:(hd + 1) * ML_DV]
        v_aug = jnp.concatenate([v_h, ones_col], axis=1)

        d_mat = jnp.where(causal, wide(cum_b, L) + (i_row - cum_row), NEG_INF)
        d_inter = cum_b + m_prev
        m_t = jnp.maximum(d_inter, rep(jnp.max(d_mat, axis=1, keepdims=True)))
        w_intra = jnp.exp(d_mat - wide(m_t, L))
        w_inter = jnp.exp(d_inter - m_t)
        s_raw = jnp.dot(qm, kt_pair, preferred_element_type=jnp.float32)
        sw = (s_raw * w_intra).astype(jnp.bfloat16)
        ct = ct_ref[hd]
        r = (jnp.dot(sw, v_aug, preferred_element_type=jnp.float32)
             + wide(w_inter, 2 * LANES) * jnp.dot(qm, ct.astype(jnp.bfloat16),
                                                  preferred_element_type=jnp.float32))
        num = r[:, :ML_DV]
        den = rep(r[:, ML_DV:ML_DV + 1])
        h_dir = num / jnp.maximum(jnp.abs(den), jnp.exp(-m_t))

        d_state = tot - cum_b + i_b
        m_new = jnp.maximum(tot + m_prev, jnp.max(d_state, axis=0, keepdims=True)[:, 0:1])
        w_s = jnp.exp(d_state - m_new)
        w_c = jnp.exp(tot + m_prev - m_new)
        vw = (v_aug.astype(jnp.float32) * wide(w_s, 2 * LANES)).astype(jnp.bfloat16)
        ct_ref[hd] = w_c * ct + jnp.dot(kmt, vw, preferred_element_type=jnp.float32)
        m_ref[hd:hd + 1, :] = jnp.broadcast_to(m_new, (1, LANES))

        sl = slice(hd * ML_DV, (hd + 1) * ML_DV)
        if finalize:
            h_sum = h_dir + hprev_ref[:, sl]
            y = _rms(h_sum, onorm_ref[hd:hd + 1, :], ML_DV)
            out_ref[:, sl] = (o_ref[:, sl].astype(jnp.float32) * y).astype(out_ref.dtype)
        else:
            out_ref[:, sl] = h_dir


def _mlstm(mlq, mlkt, mlv, gates, gates_t, hprev, mlo, onorm, *, batch, seq, chunk, reverse):
    t = mlq.shape[0]
    nc = seq // chunk
    finalize = hprev is not None
    if reverse:
        blk = lambda b, c: b * nc + (nc - 1 - c)
    else:
        blk = lambda b, c: b * nc + c
    tok = lambda n: pl.BlockSpec((chunk, n), lambda b, c: (blk(b, c), 0))
    tok_t = lambda n: pl.BlockSpec((n, chunk), lambda b, c: (0, blk(b, c)))
    in_specs = [tok(256), tok_t(256), tok(ML_W), tok(LANES), tok_t(LANES)]
    args = [mlq, mlkt, mlv, gates, gates_t]
    if finalize:
        in_specs += [tok(ML_W), tok(ML_W), pl.BlockSpec(onorm.shape, lambda b, c: (0, 0))]
        args += [hprev, mlo, onorm]
        out_dtype = jnp.bfloat16
    else:
        out_dtype = jnp.float32
    kern = functools.partial(_mlstm_kernel, reverse=reverse, finalize=finalize, chunk=chunk)
    return pl.pallas_call(
        kern, grid=(batch, nc), in_specs=in_specs, out_specs=tok(ML_W),
        out_shape=jax.ShapeDtypeStruct((t, ML_W), out_dtype),
        scratch_shapes=[pltpu.VMEM((ML_HEADS, LANES, 2 * LANES), jnp.float32),
                        pltpu.VMEM((8, LANES), jnp.float32)],
        compiler_params=pltpu.CompilerParams(dimension_semantics=("arbitrary", "arbitrary"),
                                             vmem_limit_bytes=VMEM_LIMIT_BYTES),
        name="mlstm_bwd" if reverse else "mlstm_fwd",
    )(*args)


MLA_QSUB = 2


def _mla_kernel(qt_ref, k_ref, vt_ref, o_ref, s_scr, acc_scr, *, unroll):
    tq = qt_ref.shape[2]
    tk = vt_ref.shape[2]
    nk = vt_ref.shape[0]

    def attend(sub):
        qt = qt_ref[sub]

        def scores(j):
            start = pl.multiple_of(j * tk, tk)
            return jnp.dot(k_ref[pl.ds(start, tk), :], qt, preferred_element_type=jnp.float32)

        def consume(slot, j, m_prev):
            m_new = jnp.maximum(m_prev, jnp.max(s_scr[sub, slot], axis=0, keepdims=True))
            alpha = jnp.exp2(m_prev - m_new)
            p = jnp.exp2(s_scr[sub, slot] - m_new).astype(jnp.bfloat16)
            acc_scr[sub] = alpha * acc_scr[sub] + jnp.dot(vt_ref[j], p,
                                                          preferred_element_type=jnp.float32)
            return m_new

        s_scr[sub, 0] = scores(0)
        acc_scr[sub] = jnp.zeros(acc_scr.shape[1:], acc_scr.dtype)

        def body(jj, m):
            j = unroll * jj
            for u in range(unroll):
                if not (unroll == nk and u == unroll - 1):
                    s_scr[sub, (u + 1) % 2] = scores(jnp.minimum(j + u + 1, nk - 1))
                m = consume(u % 2, j + u, m)
            return m

        lax.fori_loop(0, nk // unroll, body, jnp.full((1, tq), NEG_INF, jnp.float32))
        acc = acc_scr[sub]
        o_ref[sub * tq:(sub + 1) * tq, :] = (acc[:MLA_V] / acc[MLA_V:MLA_V + 1]).T.astype(o_ref.dtype)

    for sub in range(MLA_QSUB):
        attend(sub)


def _mla_attention(qt, k, vt, *, batch, seq, unroll):
    tile = qt.shape[-1]
    t = k.shape[1]
    nq = seq // tile
    nstep = nq // MLA_QSUB
    assert unroll % 2 == 0 and nq % unroll == 0 and nq % MLA_QSUB == 0
    return pl.pallas_call(
        functools.partial(_mla_kernel, unroll=unroll),
        grid=(batch, MLA_HEADS, nstep),
        in_specs=[pl.BlockSpec((None, MLA_QSUB, MLA_QK_PAD, tile), lambda b, h, i: (h, b * nstep + i, 0, 0)),
                  pl.BlockSpec((None, seq, MLA_QK_PAD), lambda b, h, i: (h, b, 0)),
                  pl.BlockSpec((None, nq, MLA_VT_ROWS, tile), lambda b, h, i: (h, b, 0, 0))],
        out_specs=pl.BlockSpec((MLA_QSUB * tile, MLA_V), lambda b, h, i: (b * nstep + i, h)),
        out_shape=jax.ShapeDtypeStruct((t, MLA_W), jnp.bfloat16),
        scratch_shapes=[pltpu.VMEM((MLA_QSUB, 2, tile, tile), jnp.float32),
                        pltpu.VMEM((MLA_QSUB, MLA_VT_ROWS, tile), jnp.float32)],
        compiler_params=pltpu.CompilerParams(
            dimension_semantics=("arbitrary", "arbitrary", "arbitrary"),
            vmem_limit_bytes=VMEM_LIMIT_BYTES),
        name="mla_attn",
    )(qt, k, vt)


DIL_TILE = 256
DIL_REACH = max(w // 2 for w, _ in DIL_PAIRS)
DIL_NOFF = DIL_REACH // DIL_TILE
DIL_NWIN = 2 * DIL_NOFF + 1
DIL_SUB = 4


def _dil_bias_table():
    r = np.arange(DIL_TILE)[:, None]
    c = np.arange(DIL_TILE)[None, :]
    tiles = []
    for o in range(-2 * DIL_NOFF, 2 * DIL_NOFF + 1):
        delta = o * DIL_TILE + r - c
        mult = np.zeros_like(delta)
        for window, dil in DIL_PAIRS:
            mult += ((delta % dil) == 0) & (np.abs(delta) <= window // 2)
        tiles.append(np.where(mult > 0, np.log2(np.maximum(mult, 1)), NEG_INF))
    return np.stack(tiles).astype(np.float32)


def _dil_kernel(qt_ref, k_ref, vt_ref, bias_ref, o_ref, s_scr, *, nq, nwin):
    i = pl.program_id(2)
    subs = range(DIL_SUB)
    qi = [i * DIL_SUB + sub for sub in subs]
    w0 = [jnp.clip(t - DIL_NOFF, 0, nq - nwin) for t in qi]

    def score_pass(sub):
        m = None
        for idx in range(nwin):
            start = pl.multiple_of((w0[sub] + idx) * DIL_TILE, DIL_TILE)
            bias = bias_ref[w0[sub] + idx - qi[sub] + 2 * DIL_NOFF]
            s = jnp.dot(k_ref[pl.ds(start, DIL_TILE), :], qt_ref[sub],
                        preferred_element_type=jnp.float32) + bias
            s_scr[sub, idx * DIL_TILE:(idx + 1) * DIL_TILE, :] = s
            m_cur = jnp.max(s, axis=0, keepdims=True)
            m = m_cur if idx == 0 else jnp.maximum(m, m_cur)
        return m

    def value_pass(sub, m):
        acc = None
        for idx in range(nwin):
            p = jnp.exp2(s_scr[sub, idx * DIL_TILE:(idx + 1) * DIL_TILE, :] - m)
            part = jnp.dot(vt_ref[w0[sub] + idx], p.astype(jnp.bfloat16),
                           preferred_element_type=jnp.float32)
            acc = part if acc is None else acc + part
        out_t = acc[:DIL_DH] / acc[DIL_DH:DIL_DH + 1]
        o_ref[sub * DIL_TILE:(sub + 1) * DIL_TILE, :] = out_t.T.astype(o_ref.dtype)

    m_prev = score_pass(0)
    for sub in range(1, DIL_SUB):
        m_next = score_pass(sub)
        value_pass(sub - 1, m_prev)
        m_prev = m_next
    value_pass(DIL_SUB - 1, m_prev)


def _dil_attention(dqt, dk, dvt, bias, *, batch, seq):
    t = dk.shape[0]
    nq = seq // DIL_TILE
    nwin = min(DIL_NWIN, nq)
    tq = DIL_SUB * DIL_TILE
    nstep = seq // tq
    return pl.pallas_call(
        functools.partial(_dil_kernel, nq=nq, nwin=nwin),
        grid=(batch, DIL_HEADS, nstep),
        in_specs=[pl.BlockSpec((None, DIL_SUB, DIL_DH, DIL_TILE), lambda b, h, i: (h, b * nstep + i, 0, 0)),
                  pl.BlockSpec((seq, DIL_DH), lambda b, h, i: (b, h)),
                  pl.BlockSpec((None, nq, MLA_VT_ROWS, DIL_TILE), lambda b, h, i: (h, b, 0, 0)),
                  _const_spec(bias.shape)],
        out_specs=pl.BlockSpec((tq, DIL_DH), lambda b, h, i: (b * nstep + i, h)),
        out_shape=jax.ShapeDtypeStruct((t, DIL_W), jnp.bfloat16),
        scratch_shapes=[pltpu.VMEM((DIL_SUB, nwin * DIL_TILE, DIL_TILE), jnp.float32)],
        compiler_params=pltpu.CompilerParams(
            dimension_semantics=("arbitrary", "arbitrary", "arbitrary"),
            vmem_limit_bytes=VMEM_LIMIT_BYTES),
        name="dil_attn",
    )(dqt, dk, dvt, bias)


def _outproj_kernel(x_ref, ya_ref, yb_ref, yc_ref, w_ref, o_ref):
    b0, b1 = ML_W, ML_W + MLA_W
    y = (jnp.dot(ya_ref[...], w_ref[0:b0, :], preferred_element_type=jnp.float32)
         + jnp.dot(yb_ref[...], w_ref[b0:b1, :], preferred_element_type=jnp.float32)
         + jnp.dot(yc_ref[...], w_ref[b1:, :], preferred_element_type=jnp.float32))
    o_ref[...] = x_ref[...] + y


def _outproj(x2, ya, yb, yc, w, *, tm):
    t, d = x2.shape
    tok = lambda n: pl.BlockSpec((tm, n), lambda i: (i, 0))
    return pl.pallas_call(
        _outproj_kernel, grid=(t // tm,),
        in_specs=[tok(d), tok(ML_W), tok(MLA_W), tok(DIL_W), _const_spec(w.shape)],
        out_specs=tok(d), out_shape=jax.ShapeDtypeStruct((t, d), jnp.float32),
        compiler_params=pltpu.CompilerParams(dimension_semantics=("arbitrary",),
                                             vmem_limit_bytes=VMEM_LIMIT_BYTES),
        name="outproj",
    )(x2, ya, yb, yc, w)


def _ffn_kernel(x_ref, g_ref, w1_ref, w2_ref, o_ref, h_ref):
    j = pl.program_id(1)

    @pl.when(j == 0)
    def _():
        x = x_ref[...]
        h_ref[...] = _rms(x, g_ref[...], x.shape[-1]).astype(h_ref.dtype)
        o_ref[...] = x

    u = jnp.maximum(jnp.dot(h_ref[...], w1_ref[...], preferred_element_type=jnp.float32), 0.0)
    u = (u * u).astype(jnp.bfloat16)
    o_ref[...] += jnp.dot(u, w2_ref[...], preferred_element_type=jnp.float32)


def _ffn(x2, g, w1, w2, *, tm, tf):
    t, d = x2.shape
    dff = w1.shape[1]
    return pl.pallas_call(
        _ffn_kernel, grid=(t // tm, dff // tf),
        in_specs=[pl.BlockSpec((tm, d), lambda i, j: (i, 0)),
                  pl.BlockSpec(g.shape, lambda i, j: (0, 0)),
                  pl.BlockSpec((d, tf), lambda i, j: (0, j)),
                  pl.BlockSpec((tf, d), lambda i, j: (j, 0))],
        out_specs=pl.BlockSpec((tm, d), lambda i, j: (i, 0)),
        out_shape=jax.ShapeDtypeStruct((t, d), jnp.float32),
        scratch_shapes=[pltpu.VMEM((tm, d), jnp.bfloat16)],
        compiler_params=pltpu.CompilerParams(dimension_semantics=("arbitrary", "arbitrary"),
                                             vmem_limit_bytes=VMEM_LIMIT_BYTES),
        name="ffn",
    )(x2, g, w1, w2)


def _rope_tables(seq, rot_dim, fill_cos):
    pos = jnp.arange(seq, dtype=jnp.float32)
    inv_freq = ROPE_THETA ** (-jnp.arange(0, rot_dim, 2, dtype=jnp.float32) / rot_dim)
    ang = pos[:, None] * inv_freq[None, :]
    cos, sin = jnp.cos(ang), jnp.sin(ang)
    pad = LANES - rot_dim
    cos_f = jnp.concatenate([cos, cos, jnp.full((seq, pad), fill_cos, jnp.float32)], axis=1)
    sin_f = jnp.concatenate([-sin, sin, jnp.zeros((seq, pad), jnp.float32)], axis=1)
    return cos_f, sin_f


def _pad_cols(a, n):
    return jnp.pad(a, ((0, 0), (0, n - a.shape[1])))


def _pack_w_in(w_in):
    splits = np.cumsum((256, 256, ML_W, ML_W, 4 * ML_HEADS, MLA_Q_RANK, MLA_KV_RANK + MLA_ROPE,
                        DIL_W, DIL_W, DIL_W))[:-1].tolist()
    (wq, wk, wv, wo, wg, wcq, wckv, wdq, wdk, wdv) = jnp.split(w_in, splits, axis=1)
    packed = jnp.concatenate([wq, wk, wv, wo, _pad_cols(wg, LANES), wcq, _pad_cols(wckv, 256),
                              wdq, wdk, wdv], axis=1)
    return packed.astype(jnp.bfloat16)


def _layer(x2, p, tables, *, batch, seq):
    cosb, sinb, cosc, sinc, dil_bias = tables
    w = _pack_w_in(p['w_in'])
    wqb = p['mla_w_q_b'].reshape(MLA_Q_RANK, MLA_HEADS, MLA_NOPE + MLA_ROPE)
    wqb = jnp.pad(wqb, ((0, 0), (0, 0), (0, MLA_QK_PAD - MLA_NOPE - MLA_ROPE)))
    wqb = wqb.reshape(MLA_Q_RANK, MLA_HEADS * MLA_QK_PAD).astype(jnp.bfloat16)
    wkvb = p['mla_w_kv_b'].astype(jnp.bfloat16)
    gbias = jnp.concatenate([p['ml_i_bias'][0], p['ml_f_bias'][0], p['ml_i_bias'][1], p['ml_f_bias'][1]])
    gbias = _pad_cols(gbias[None, :], LANES)
    qh, kh = p['mla_q_head_norm'], p['mla_k_head_norm']
    hg = jnp.stack([qh[:MLA_NOPE], jnp.pad(qh[MLA_NOPE:], (0, LANES - MLA_ROPE)),
                    kh[:MLA_NOPE], jnp.pad(kh[MLA_NOPE:], (0, LANES - MLA_ROPE)),
                    p['dil_q_norm'], p['dil_k_norm'],
                    jnp.zeros((LANES,), jnp.float32), jnp.zeros((LANES,), jnp.float32)])

    (mlq, mlkt, mlv, mlo, gates, gates_t, qt, k, vt, dqt, dk, dvt) = _inproj(
        x2, p['norm_mix'][None, :], w, wqb, wkvb, gbias, p['mla_q_norm'][None, :],
        p['mla_kv_norm'][None, :], hg, cosb, sinb, cosc, sinc, seq=seq, tm=MLA_TILE)

    chunk = min(256, seq)
    h_bwd = _mlstm(mlq, mlkt, mlv, gates, gates_t, None, None, None, batch=batch, seq=seq, chunk=chunk,
                   reverse=True)
    ya = _mlstm(mlq, mlkt, mlv, gates, gates_t, h_bwd, mlo, p['ml_out_norm'], batch=batch, seq=seq,
                chunk=chunk, reverse=False)
    yb = _mla_attention(qt, k, vt, batch=batch, seq=seq, unroll=min(16, seq // MLA_TILE))
    yc = _dil_attention(dqt, dk, dvt, dil_bias, batch=batch, seq=seq)

    x2 = _outproj(x2, ya, yb, yc, p['w_out'].astype(jnp.bfloat16), tm=min(512, seq))
    return _ffn(x2, p['norm_ff'][None, :], p['w_ff1'].astype(jnp.bfloat16), p['w_ff2'].astype(jnp.bfloat16),
                tm=min(512, seq), tf=1024)


def kernel(x, norm_mix, w_in, ml_i_bias, ml_f_bias, ml_out_norm, mla_q_norm, mla_w_q_b, mla_kv_norm,
           mla_w_kv_b, mla_q_head_norm, mla_k_head_norm, dil_q_norm, dil_k_norm, w_out, norm_ff,
           w_ff1, w_ff2):
    batch, seq, d = x.shape
    assert seq % (DIL_SUB * DIL_TILE) == 0
    params = dict(norm_mix=norm_mix, w_in=w_in, ml_i_bias=ml_i_bias, ml_f_bias=ml_f_bias,
                  ml_out_norm=ml_out_norm, mla_q_norm=mla_q_norm, mla_w_q_b=mla_w_q_b,
                  mla_kv_norm=mla_kv_norm, mla_w_kv_b=mla_w_kv_b, mla_q_head_norm=mla_q_head_norm,
                  mla_k_head_norm=mla_k_head_norm, dil_q_norm=dil_q_norm, dil_k_norm=dil_k_norm,
                  w_out=w_out, norm_ff=norm_ff, w_ff1=w_ff1, w_ff2=w_ff2)
    cosb, sinb = _rope_tables(seq, MLA_ROPE, 0.0)
    cosc, sinc = _rope_tables(seq, DIL_ROT, 1.0)
    tables = (cosb, sinb, cosc, sinc, jnp.asarray(_dil_bias_table()))
    x2 = x.reshape(batch * seq, d)
    for layer in range(norm_mix.shape[0]):
        x2 = _layer(x2, {name: val[layer] for name, val in params.items()}, tables, batch=batch, seq=seq)
    return x2.reshape(batch, seq, d)
```

```python
import functools
import math

import numpy as np
import jax
import jax.numpy as jnp
from jax import lax
from jax.experimental import pallas as pl
from jax.experimental.pallas import tpu as pltpu

EPS = 1e-6
NEG_INF = -1e30
ROPE_THETA = 500000.0

ML_HEADS = 4
ML_DK = 64
ML_DV = 128
ML_W = ML_HEADS * ML_DV

MLA_HEADS = 6
MLA_Q_RANK = 384
MLA_KV_RANK = 128
MLA_NOPE = 128
MLA_ROPE = 64
MLA_V = 128
MLA_W = MLA_HEADS * MLA_V
MLA_QK_PAD = 256
MLA_VT_ROWS = 144
MLA_TILE = 512

DIL_HEADS = 6
DIL_DH = 128
DIL_ROT = DIL_DH // 4
DIL_PAIRS = ((128, 1), (512, 4), (2048, 16))
DIL_W = DIL_HEADS * DIL_DH

LANES = 128
VMEM_LIMIT_BYTES = 60 * 1024 * 1024

_C_MLQ = 0
_C_MLK = 256
_C_MLV = 512
_C_MLO = 1024
_C_GATE = 1536
_C_CQ = 1664
_C_CKV = 2048
_C_DQ = 2304
_C_DK = 3072
_C_DV = 3840
_C_END = 4608


def _rms(x, gain, n):
    ms = jnp.sum(x * x, axis=-1, keepdims=True) * (1.0 / n)
    return x * lax.rsqrt(ms + EPS) * gain


def _rope_tile(x, cos_f, sin_f, half):
    lane = lax.broadcasted_iota(jnp.int32, x.shape, 1)
    sw = jnp.where(lane < half, pltpu.roll(x, LANES - half, 1), pltpu.roll(x, half, 1))
    return x * cos_f + sw * sin_f


def _log_sigmoid(x):
    return jnp.minimum(x, 0.0) - jnp.log1p(jnp.exp(-jnp.abs(x)))


def _inproj_kernel(x_ref, gmix_ref, w_ref, wqb_ref, wkvb_ref, gbias_ref, qlat_ref, kvlat_ref,
                   hg_ref, cosb_ref, sinb_ref, cosc_ref, sinc_ref,
                   mlq_ref, mlkt_ref, mlv_ref, mlo_ref, gate_ref, gatet_ref, qt_ref, k_ref, vt_ref,
                   dqt_ref, dk_ref, dvt_ref, *, mla_scale, dil_scale):
    x = x_ref[...]
    h = _rms(x, gmix_ref[...], x.shape[-1]).astype(jnp.bfloat16)

    def proj(c0, c1):
        return jnp.dot(h, w_ref[:, c0:c1], preferred_element_type=jnp.float32)

    cos_b = cosb_ref[...]
    sin_b = sinb_ref[...]
    cos_c = cosc_ref[...]
    sin_c = sinc_ref[...]
    qg_n = hg_ref[0:1, :]
    qg_r = hg_ref[1:2, :]
    kg_n = hg_ref[2:3, :]
    kg_r = hg_ref[3:4, :]
    dqg = hg_ref[4:5, :]
    dkg = hg_ref[5:6, :]

    cq = _rms(proj(_C_CQ, _C_CKV), qlat_ref[...], MLA_Q_RANK).astype(jnp.bfloat16)
    zc = proj(_C_CKV, _C_DQ)
    ckv = _rms(zc[:, :MLA_KV_RANK], kvlat_ref[...], MLA_KV_RANK).astype(jnp.bfloat16)
    zdq = proj(_C_DQ, _C_DK)
    zdk = proj(_C_DK, _C_DV)
    zq = jnp.dot(cq, wqb_ref[...], preferred_element_type=jnp.float32)
    zkv = jnp.dot(ckv, wkvb_ref[...], preferred_element_type=jnp.float32)

    k_rope = _rope_tile(_rms(zc[:, MLA_KV_RANK:], kg_r, MLA_ROPE), cos_b, sin_b, MLA_ROPE // 2)
    k_rope = k_rope.astype(jnp.bfloat16)
    pad_rows = MLA_VT_ROWS - MLA_V

    def ones_row_tile(n):
        first = lax.broadcasted_iota(jnp.int32, (pad_rows, n), 0) == 0
        return jnp.where(first, 1.0, 0.0).astype(jnp.bfloat16)

    ones_row = ones_row_tile(x.shape[0])
    ones_row_dil = ones_row_tile(DIL_TILE)
    for hd in range(MLA_HEADS):
        c = hd * MLA_QK_PAD
        q_nope = _rms(zq[:, c:c + MLA_NOPE], qg_n, MLA_NOPE)
        q_rope = _rope_tile(_rms(zq[:, c + MLA_NOPE:c + MLA_QK_PAD], qg_r, MLA_ROPE),
                            cos_b, sin_b, MLA_ROPE // 2)
        qt_ref[hd, 0, 0:MLA_NOPE, :] = (q_nope * mla_scale).T.astype(jnp.bfloat16)
        qt_ref[hd, 0, MLA_NOPE:MLA_QK_PAD, :] = (q_rope * mla_scale).T.astype(jnp.bfloat16)
        k_nope = _rms(zkv[:, c:c + MLA_NOPE], kg_n, MLA_NOPE)
        k_ref[hd, :, 0:MLA_NOPE] = k_nope.astype(jnp.bfloat16)
        k_ref[hd, :, MLA_NOPE:MLA_QK_PAD] = k_rope
        vt_ref[hd, 0, 0:MLA_V, :] = zkv[:, c + MLA_NOPE:c + MLA_QK_PAD].T.astype(jnp.bfloat16)
        vt_ref[hd, 0, MLA_V:MLA_VT_ROWS, :] = ones_row

    mlkt_ref[...] = proj(_C_MLK, _C_MLV).T.astype(jnp.bfloat16)
    mlo_ref[...] = jax.nn.sigmoid(proj(_C_MLO, _C_GATE)).astype(jnp.bfloat16)
    g = proj(_C_GATE, _C_CQ) + gbias_ref[...]
    lane = lax.broadcasted_iota(jnp.int32, g.shape, 1)
    is_forget = (lane % 8) >= 4
    g = jnp.where(is_forget, _log_sigmoid(g), g)
    gate_ref[...] = g
    gatet_ref[...] = g.T

    for hd in range(DIL_HEADS):
        c = hd * DIL_DH
        qh = _rope_tile(_rms(zdq[:, c:c + DIL_DH], dqg, DIL_DH), cos_c, sin_c, DIL_ROT // 2)
        kh = _rope_tile(_rms(zdk[:, c:c + DIL_DH], dkg, DIL_DH), cos_c, sin_c, DIL_ROT // 2)
        qh = qh * dil_scale
        for sb in range(x.shape[0] // DIL_TILE):
            dqt_ref[hd, sb] = qh[sb * DIL_TILE:(sb + 1) * DIL_TILE].T.astype(jnp.bfloat16)
        dk_ref[:, c:c + DIL_DH] = kh.astype(jnp.bfloat16)
    zdv = proj(_C_DV, _C_END)
    for hd in range(DIL_HEADS):
        c = hd * DIL_DH
        for sb in range(x.shape[0] // DIL_TILE):
            rows = slice(sb * DIL_TILE, (sb + 1) * DIL_TILE)
            dvt_ref[hd, sb, 0:DIL_DH, :] = zdv[rows, c:c + DIL_DH].T.astype(jnp.bfloat16)
            dvt_ref[hd, sb, DIL_DH:MLA_VT_ROWS, :] = ones_row_dil

    mlq_ref[...] = (proj(_C_MLQ, _C_MLK) * (ML_DK ** -0.5)).astype(jnp.bfloat16)
    mlv_ref[...] = proj(_C_MLV, _C_MLO).astype(jnp.bfloat16)


def _const_spec(shape):
    n = len(shape)
    return pl.BlockSpec(shape, lambda *_: (0,) * n, pipeline_mode=pl.Buffered(1))


def _inproj(x2, gmix, w, wqb, wkvb, gbias, qlat, kvlat, hg, cosb, sinb, cosc, sinc, *, seq, tm):
    t, d = x2.shape
    nt = t // tm
    ns = seq // tm
    tok = lambda n: pl.BlockSpec((tm, n), lambda i: (i, 0))
    pos = pl.BlockSpec((tm, LANES), lambda i: (i % ns, 0))
    headed = lambda n: pl.BlockSpec((MLA_HEADS, tm, n), lambda i: (0, i, 0))
    headed_t = lambda n: pl.BlockSpec((MLA_HEADS, 1, n, tm), lambda i: (0, i, 0, 0))
    bf = jnp.bfloat16
    out_shape = (
        jax.ShapeDtypeStruct((t, 256), bf), jax.ShapeDtypeStruct((256, t), bf),
        jax.ShapeDtypeStruct((t, ML_W), bf), jax.ShapeDtypeStruct((t, ML_W), bf),
        jax.ShapeDtypeStruct((t, LANES), jnp.float32), jax.ShapeDtypeStruct((LANES, t), jnp.float32),
        jax.ShapeDtypeStruct((MLA_HEADS, nt, MLA_QK_PAD, tm), bf),
        jax.ShapeDtypeStruct((MLA_HEADS, t, MLA_QK_PAD), bf),
        jax.ShapeDtypeStruct((MLA_HEADS, nt, MLA_VT_ROWS, tm), bf),
        jax.ShapeDtypeStruct((DIL_HEADS, t // DIL_TILE, DIL_DH, DIL_TILE), bf),
        jax.ShapeDtypeStruct((t, DIL_W), bf),
        jax.ShapeDtypeStruct((DIL_HEADS, t // DIL_TILE, MLA_VT_ROWS, DIL_TILE), bf),
    )
    dil_t = lambda n: pl.BlockSpec((DIL_HEADS, tm // DIL_TILE, n, DIL_TILE), lambda i: (0, i, 0, 0))
    tok_t = lambda n: pl.BlockSpec((n, tm), lambda i: (0, i))
    out_specs = (tok(256), tok_t(256), tok(ML_W), tok(ML_W), tok(LANES), tok_t(LANES),
                 headed_t(MLA_QK_PAD), headed(MLA_QK_PAD), headed_t(MLA_VT_ROWS),
                 dil_t(DIL_DH), tok(DIL_W), dil_t(MLA_VT_ROWS))
    in_specs = [tok(d), _const_spec(gmix.shape), _const_spec(w.shape), _const_spec(wqb.shape),
                _const_spec(wkvb.shape), _const_spec(gbias.shape), _const_spec(qlat.shape),
                _const_spec(kvlat.shape), _const_spec(hg.shape), pos, pos, pos, pos]
    kern = functools.partial(_inproj_kernel,
                             mla_scale=(MLA_NOPE + MLA_ROPE) ** -0.5 * math.log2(math.e),
                             dil_scale=DIL_DH ** -0.5 * math.log2(math.e))
    return pl.pallas_call(
        kern, grid=(nt,), in_specs=in_specs, out_specs=out_specs, out_shape=out_shape,
        compiler_params=pltpu.CompilerParams(dimension_semantics=("arbitrary",),
                                             vmem_limit_bytes=VMEM_LIMIT_BYTES),
        name="inproj",
    )(x2, gmix, w, wqb, wkvb, gbias, qlat, kvlat, hg, cosb, sinb, cosc, sinc)


def _mlstm_kernel(*refs, reverse, finalize, chunk, nb):
    q_ref, v_ref, gate_ref = refs[0:3]
    kt_refs = refs[3:3 + nb]
    gatet_refs = refs[3 + nb:3 + 2 * nb]
    rest = refs[3 + 2 * nb:]
    if finalize:
        hprev_ref, o_ref, onorm_ref, out_ref, ct_ref, m_ref = rest
    else:
        out_ref, ct_ref, m_ref = rest
    L = chunk
    c = pl.program_id(0)

    @pl.when(c == 0)
    def _():
        ct_ref[...] = jnp.zeros_like(ct_ref)
        m_ref[...] = jnp.zeros_like(m_ref)

    row = lax.broadcasted_iota(jnp.int32, (L, L), 0)
    col = lax.broadcasted_iota(jnp.int32, (L, L), 1)
    causal = (col >= row) if reverse else (col <= row)
    vis = causal.astype(jnp.bfloat16)
    vis_t = ((row >= col) if reverse else (row <= col)).astype(jnp.bfloat16)

    def split3(x):
        hi = x.astype(jnp.bfloat16)
        rest_ = x - hi.astype(jnp.float32)
        mid = rest_.astype(jnp.bfloat16)
        return hi, mid, (rest_ - mid.astype(jnp.float32)).astype(jnp.bfloat16)

    def rep(col_):
        return jnp.broadcast_to(col_, (L, LANES))

    def wide(xb, n):
        return xb if n == LANES else jnp.concatenate([xb] * (n // LANES), axis=1)

    lane = lax.broadcasted_iota(jnp.int32, (L, LANES), 1)
    ones_col = jnp.where(lane == 0, 1.0, 0.0).astype(jnp.bfloat16)
    zero_half = jnp.zeros((ML_DK, L), jnp.bfloat16)

    d0 = 8 if reverse else 0
    gates, slab, cum_t, cum = [], [], [], []
    for bi in range(nb):
        gates.append(gate_ref[bi])
        slab.append(gatet_refs[bi][d0:d0 + 8, :])
        cum_t.append(sum(jnp.dot(vis, part, preferred_element_type=jnp.float32)
                         for part in split3(gates[bi])))
        cum.append(sum(jnp.dot(part, vis_t, preferred_element_type=jnp.float32)
                       for part in split3(slab[bi])))

    for hd in range(ML_HEADS):
        for bi in range(nb):
            st = bi * ML_HEADS + hd
            i_b = rep(gates[bi][:, d0 + hd:d0 + hd + 1])
            i_row = slab[bi][hd:hd + 1, :]
            cum_b = rep(cum_t[bi][:, d0 + 4 + hd:d0 + 5 + hd])
            cum_row = cum[bi][4 + hd:5 + hd, :]
            tot = cum_row[:, 0:1] if reverse else cum_row[:, L - 1:L]
            m_prev = m_ref[st:st + 1, 0:1]

            pair, half = hd // 2, hd % 2
            in_head = (lane >= half * ML_DK) & (lane < (half + 1) * ML_DK)
            qp = q_ref[bi, :, pair * LANES:(pair + 1) * LANES]
            qm = jnp.where(in_head, qp, jnp.zeros_like(qp))
            kt_ref = kt_refs[bi]
            kt_pair = kt_ref[pair * LANES:(pair + 1) * LANES, :]
            kt_h = kt_ref[pair * LANES + half * ML_DK:pair * LANES + (half + 1) * ML_DK, :]
            kmt = jnp.concatenate([kt_h, zero_half] if half == 0 else [zero_half, kt_h], axis=0)
            v_h = v_ref[bi, :, hd * ML_DV:(hd + 1) * ML_DV]
            v_aug = jnp.concatenate([v_h, ones_col], axis=1)

            d_mat = jnp.where(causal, wide(cum_b, L) + (i_row - cum_row), NEG_INF)
            d_inter = cum_b + m_prev
            m_t = jnp.maximum(d_inter, rep(jnp.max(d_mat, axis=1, keepdims=True)))
            w_intra = jnp.exp(d_mat - wide(m_t, L))
            w_inter = jnp.exp(d_inter - m_t)
            s_raw = jnp.dot(qm, kt_pair, preferred_element_type=jnp.float32)
            sw = (s_raw * w_intra).astype(jnp.bfloat16)
            ct = ct_ref[st]
            r = (jnp.dot(sw, v_aug, preferred_element_type=jnp.float32)
                 + wide(w_inter, 2 * LANES) * jnp.dot(qm, ct.astype(jnp.bfloat16),
                                                      preferred_element_type=jnp.float32))
            num = r[:, :ML_DV]
            den = rep(r[:, ML_DV:ML_DV + 1])
            h_dir = num / jnp.maximum(jnp.abs(den), jnp.exp(-m_t))

            d_state = tot - cum_b + i_b
            m_new = jnp.maximum(tot + m_prev, jnp.max(d_state, axis=0, keepdims=True)[:, 0:1])
            w_s = jnp.exp(d_state - m_new)
            w_c = jnp.exp(tot + m_prev - m_new)
            vw = (v_aug.astype(jnp.float32) * wide(w_s, 2 * LANES)).astype(jnp.bfloat16)
            ct_ref[st] = w_c * ct + jnp.dot(kmt, vw, preferred_element_type=jnp.float32)
            m_ref[st:st + 1, :] = jnp.broadcast_to(m_new, (1, LANES))

            sl = slice(hd * ML_DV, (hd + 1) * ML_DV)
            if finalize:
                h_sum = h_dir + hprev_ref[bi, :, sl]
                y = _rms(h_sum, onorm_ref[hd:hd + 1, :], ML_DV)
                out_ref[bi, :, sl] = (o_ref[bi, :, sl].astype(jnp.float32) * y).astype(out_ref.dtype)
            else:
                out_ref[bi, :, sl] = h_dir


def _mlstm(mlq, mlkt, mlv, gates, gates_t, hprev, mlo, onorm, *, batch, seq, chunk, reverse):
    nc = seq // chunk
    finalize = hprev is not None
    blk = (lambda c: nc - 1 - c) if reverse else (lambda c: c)
    tok = lambda n: pl.BlockSpec((batch, chunk, n), lambda c: (0, blk(c), 0))
    tok_t = lambda n, bi: pl.BlockSpec((n, chunk), lambda c: (0, bi * nc + blk(c)))
    view = lambda a: a.reshape(batch, seq, a.shape[-1])
    in_specs = ([tok(256), tok(ML_W), tok(LANES)]
                + [tok_t(256, bi) for bi in range(batch)] + [tok_t(LANES, bi) for bi in range(batch)])
    args = [view(mlq), view(mlv), view(gates)] + [mlkt] * batch + [gates_t] * batch
    if finalize:
        in_specs += [tok(ML_W), tok(ML_W), pl.BlockSpec(onorm.shape, lambda c: (0, 0))]
        args += [hprev, view(mlo), onorm]
        out_dtype = jnp.bfloat16
    else:
        out_dtype = jnp.float32
    kern = functools.partial(_mlstm_kernel, reverse=reverse, finalize=finalize, chunk=chunk, nb=batch)
    states = batch * ML_HEADS
    return pl.pallas_call(
        kern, grid=(nc,), in_specs=in_specs, out_specs=tok(ML_W),
        out_shape=jax.ShapeDtypeStruct((batch, seq, ML_W), out_dtype),
        scratch_shapes=[pltpu.VMEM((states, LANES, 2 * LANES), jnp.float32),
                        pltpu.VMEM((-(-states // 8) * 8, LANES), jnp.float32)],
        compiler_params=pltpu.CompilerParams(dimension_semantics=("arbitrary",),
                                             vmem_limit_bytes=VMEM_LIMIT_BYTES),
        name="mlstm_bwd" if reverse else "mlstm_fwd",
    )(*args)


MLA_QSUB = 2


def _mla_kernel(qt_ref, k_ref, vt_ref, o_ref, s_scr, acc_scr, *, unroll):
    tq = qt_ref.shape[2]
    tk = vt_ref.shape[2]
    nk = vt_ref.shape[0]

    def attend(sub):
        qt = qt_ref[sub]

        def scores(slot, j):
            start = pl.multiple_of(j * tk, tk)
            s = jnp.dot(k_ref[pl.ds(start, tk), :], qt, preferred_element_type=jnp.float32)
            s_scr[sub, slot] = s
            return jnp.max(s, axis=0, keepdims=True)

        def consume(slot, j, m_prev, m_tile):
            m_new = jnp.maximum(m_prev, m_tile)
            alpha = jnp.exp2(m_prev - m_new)
            p = jnp.exp2(s_scr[sub, slot] - m_new).astype(jnp.bfloat16)
            acc_scr[sub] = alpha * acc_scr[sub] + jnp.dot(vt_ref[j], p,
                                                          preferred_element_type=jnp.float32)
            return m_new

        m_first = scores(0, 0)
        acc_scr[sub] = jnp.zeros(acc_scr.shape[1:], acc_scr.dtype)

        def body(jj, carry):
            m, m_tile = carry
            j = unroll * jj
            for u in range(unroll):
                m_next = m_tile
                if not (unroll == nk and u == unroll - 1):
                    m_next = scores((u + 1) % 2, jnp.minimum(j + u + 1, nk - 1))
                m = consume(u % 2, j + u, m, m_tile)
                m_tile = m_next
            return m, m_tile

        lax.fori_loop(0, nk // unroll, body, (jnp.full((1, tq), NEG_INF, jnp.float32), m_first))
        acc = acc_scr[sub]
        o_ref[sub * tq:(sub + 1) * tq, :] = (acc[:MLA_V] / acc[MLA_V:MLA_V + 1]).T.astype(o_ref.dtype)

    for sub in range(MLA_QSUB):
        attend(sub)


def _mla_attention(qt, k, vt, *, batch, seq, unroll):
    tile = qt.shape[-1]
    t = k.shape[1]
    nq = seq // tile
    nstep = nq // MLA_QSUB
    assert unroll % 2 == 0 and nq % unroll == 0 and nq % MLA_QSUB == 0
    return pl.pallas_call(
        functools.partial(_mla_kernel, unroll=unroll),
        grid=(batch, MLA_HEADS, nstep),
        in_specs=[pl.BlockSpec((None, MLA_QSUB, MLA_QK_PAD, tile), lambda b, h, i: (h, b * nstep + i, 0, 0)),
                  pl.BlockSpec((None, seq, MLA_QK_PAD), lambda b, h, i: (h, b, 0)),
                  pl.BlockSpec((None, nq, MLA_VT_ROWS, tile), lambda b, h, i: (h, b, 0, 0))],
        out_specs=pl.BlockSpec((MLA_QSUB * tile, MLA_V), lambda b, h, i: (b * nstep + i, h)),
        out_shape=jax.ShapeDtypeStruct((t, MLA_W), jnp.bfloat16),
        scratch_shapes=[pltpu.VMEM((MLA_QSUB, 2, tile, tile), jnp.float32),
                        pltpu.VMEM((MLA_QSUB, MLA_VT_ROWS, tile), jnp.float32)],
        compiler_params=pltpu.CompilerParams(
            dimension_semantics=("arbitrary", "arbitrary", "arbitrary"),
            vmem_limit_bytes=VMEM_LIMIT_BYTES),
        name="mla_attn",
    )(qt, k, vt)


DIL_TILE = 256
DIL_REACH = max(w // 2 for w, _ in DIL_PAIRS)
DIL_NOFF = DIL_REACH // DIL_TILE
DIL_NWIN = 2 * DIL_NOFF + 1
DIL_SUB = 4


def _dil_bias_table():
    r = np.arange(DIL_TILE)[:, None]
    c = np.arange(DIL_TILE)[None, :]
    tiles = []
    for o in range(-2 * DIL_NOFF, 2 * DIL_NOFF + 1):
        delta = o * DIL_TILE + r - c
        mult = np.zeros_like(delta)
        for window, dil in DIL_PAIRS:
            mult += ((delta % dil) == 0) & (np.abs(delta) <= window // 2)
        tiles.append(np.where(mult > 0, np.log2(np.maximum(mult, 1)), NEG_INF))
    return np.stack(tiles).astype(np.float32)


def _dil_kernel(qt_ref, k_ref, vt_ref, bias_ref, o_ref, s_scr, *, nq, nwin):
    i = pl.program_id(2)
    subs = range(DIL_SUB)
    qi = [i * DIL_SUB + sub for sub in subs]
    w0 = [jnp.clip(t - DIL_NOFF, 0, nq - nwin) for t in qi]

    def score_pass(sub):
        m = None
        for idx in range(nwin):
            start = pl.multiple_of((w0[sub] + idx) * DIL_TILE, DIL_TILE)
            bias = bias_ref[w0[sub] + idx - qi[sub] + 2 * DIL_NOFF]
            s = jnp.dot(k_ref[pl.ds(start, DIL_TILE), :], qt_ref[sub],
                        preferred_element_type=jnp.float32) + bias
            s_scr[sub, idx * DIL_TILE:(idx + 1) * DIL_TILE, :] = s
            m_cur = jnp.max(s, axis=0, keepdims=True)
            m = m_cur if idx == 0 else jnp.maximum(m, m_cur)
        return m

    def value_pass(sub, m):
        acc = None
        for idx in range(nwin):
            p = jnp.exp2(s_scr[sub, idx * DIL_TILE:(idx + 1) * DIL_TILE, :] - m)
            part = jnp.dot(vt_ref[w0[sub] + idx], p.astype(jnp.bfloat16),
                           preferred_element_type=jnp.float32)
            acc = part if acc is None else acc + part
        out_t = acc[:DIL_DH] / acc[DIL_DH:DIL_DH + 1]
        o_ref[sub * DIL_TILE:(sub + 1) * DIL_TILE, :] = out_t.T.astype(o_ref.dtype)

    m_prev = score_pass(0)
    for sub in range(1, DIL_SUB):
        m_next = score_pass(sub)
        value_pass(sub - 1, m_prev)
        m_prev = m_next
    value_pass(DIL_SUB - 1, m_prev)


def _dil_attention(dqt, dk, dvt, bias, *, batch, seq):
    t = dk.shape[0]
    nq = seq // DIL_TILE
    nwin = min(DIL_NWIN, nq)
    tq = DIL_SUB * DIL_TILE
    nstep = seq // tq
    return pl.pallas_call(
        functools.partial(_dil_kernel, nq=nq, nwin=nwin),
        grid=(batch, DIL_HEADS, nstep),
        in_specs=[pl.BlockSpec((None, DIL_SUB, DIL_DH, DIL_TILE), lambda b, h, i: (h, b * nstep + i, 0, 0)),
                  pl.BlockSpec((seq, DIL_DH), lambda b, h, i: (b, h)),
                  pl.BlockSpec((None, nq, MLA_VT_ROWS, DIL_TILE), lambda b, h, i: (h, b, 0, 0)),
                  _const_spec(bias.shape)],
        out_specs=pl.BlockSpec((tq, DIL_DH), lambda b, h, i: (b * nstep + i, h)),
        out_shape=jax.ShapeDtypeStruct((t, DIL_W), jnp.bfloat16),
        scratch_shapes=[pltpu.VMEM((DIL_SUB, nwin * DIL_TILE, DIL_TILE), jnp.float32)],
        compiler_params=pltpu.CompilerParams(
            dimension_semantics=("arbitrary", "arbitrary", "arbitrary"),
            vmem_limit_bytes=VMEM_LIMIT_BYTES),
        name="dil_attn",
    )(dqt, dk, dvt, bias)


def _outproj_kernel(x_ref, ya_ref, yb_ref, yc_ref, w_ref, o_ref):
    b0, b1 = ML_W, ML_W + MLA_W
    y = (jnp.dot(ya_ref[...], w_ref[0:b0, :], preferred_element_type=jnp.float32)
         + jnp.dot(yb_ref[...], w_ref[b0:b1, :], preferred_element_type=jnp.float32)
         + jnp.dot(yc_ref[...], w_ref[b1:, :], preferred_element_type=jnp.float32))
    o_ref[...] = x_ref[...] + y


def _outproj(x2, ya, yb, yc, w, *, tm):
    t, d = x2.shape
    tok = lambda n: pl.BlockSpec((tm, n), lambda i: (i, 0))
    return pl.pallas_call(
        _outproj_kernel, grid=(t // tm,),
        in_specs=[tok(d), tok(ML_W), tok(MLA_W), tok(DIL_W), _const_spec(w.shape)],
        out_specs=tok(d), out_shape=jax.ShapeDtypeStruct((t, d), jnp.float32),
        compiler_params=pltpu.CompilerParams(dimension_semantics=("arbitrary",),
                                             vmem_limit_bytes=VMEM_LIMIT_BYTES),
        name="outproj",
    )(x2, ya, yb, yc, w)


def _ffn_kernel(x_ref, g_ref, w1_ref, w2_ref, o_ref, h_ref):
    j = pl.program_id(1)

    @pl.when(j == 0)
    def _():
        x = x_ref[...]
        h_ref[...] = _rms(x, g_ref[...], x.shape[-1]).astype(h_ref.dtype)
        o_ref[...] = x

    u = jnp.maximum(jnp.dot(h_ref[...], w1_ref[...], preferred_element_type=jnp.float32), 0.0)
    u = (u * u).astype(jnp.bfloat16)
    o_ref[...] += jnp.dot(u, w2_ref[...], preferred_element_type=jnp.float32)


def _ffn(x2, g, w1, w2, *, tm, tf):
    t, d = x2.shape
    dff = w1.shape[1]
    return pl.pallas_call(
        _ffn_kernel, grid=(t // tm, dff // tf),
        in_specs=[pl.BlockSpec((tm, d), lambda i, j: (i, 0)),
                  pl.BlockSpec(g.shape, lambda i, j: (0, 0)),
                  pl.BlockSpec((d, tf), lambda i, j: (0, j)),
                  pl.BlockSpec((tf, d), lambda i, j: (j, 0))],
        out_specs=pl.BlockSpec((tm, d), lambda i, j: (i, 0)),
        out_shape=jax.ShapeDtypeStruct((t, d), jnp.float32),
        scratch_shapes=[pltpu.VMEM((tm, d), jnp.bfloat16)],
        compiler_params=pltpu.CompilerParams(dimension_semantics=("arbitrary", "arbitrary"),
                                             vmem_limit_bytes=VMEM_LIMIT_BYTES),
        name="ffn",
    )(x2, g, w1, w2)


def _rope_tables(seq, rot_dim, fill_cos):
    pos = jnp.arange(seq, dtype=jnp.float32)
    inv_freq = ROPE_THETA ** (-jnp.arange(0, rot_dim, 2, dtype=jnp.float32) / rot_dim)
    ang = pos[:, None] * inv_freq[None, :]
    cos, sin = jnp.cos(ang), jnp.sin(ang)
    pad = LANES - rot_dim
    cos_f = jnp.concatenate([cos, cos, jnp.full((seq, pad), fill_cos, jnp.float32)], axis=1)
    sin_f = jnp.concatenate([-sin, sin, jnp.zeros((seq, pad), jnp.float32)], axis=1)
    return cos_f, sin_f


def _pad_cols(a, n):
    return jnp.pad(a, ((0, 0), (0, n - a.shape[1])))


def _pack_w_in(w_in):
    splits = np.cumsum((256, 256, ML_W, ML_W, 4 * ML_HEADS, MLA_Q_RANK, MLA_KV_RANK + MLA_ROPE,
                        DIL_W, DIL_W, DIL_W))[:-1].tolist()
    (wq, wk, wv, wo, wg, wcq, wckv, wdq, wdk, wdv) = jnp.split(w_in, splits, axis=1)
    packed = jnp.concatenate([wq, wk, wv, wo, _pad_cols(wg, LANES), wcq, _pad_cols(wckv, 256),
                              wdq, wdk, wdv], axis=1)
    return packed.astype(jnp.bfloat16)


def _cast_kernel(x_ref, o_ref):
    o_ref[...] = x_ref[...].astype(o_ref.dtype)


def _to_bf16(stacked, layer, block_bytes=8 * 1024 * 1024):
    _, rows, cols = stacked.shape
    br = min(rows, max(8, block_bytes // (4 * cols)))
    assert rows % br == 0
    return pl.pallas_call(
        _cast_kernel, grid=(rows // br,),
        in_specs=[pl.BlockSpec((None, br, cols), lambda i: (layer, i, 0))],
        out_specs=pl.BlockSpec((br, cols), lambda i: (i, 0)),
        out_shape=jax.ShapeDtypeStruct((rows, cols), jnp.bfloat16),
        compiler_params=pltpu.CompilerParams(dimension_semantics=("arbitrary",),
                                             vmem_limit_bytes=VMEM_LIMIT_BYTES),
        name="to_bf16",
    )(stacked)


def _layer(x2, p, stacked, layer, tables, *, batch, seq):
    cosb, sinb, cosc, sinc, dil_bias = tables
    w = _pack_w_in(p['w_in'])
    wqb = p['mla_w_q_b'].reshape(MLA_Q_RANK, MLA_HEADS, MLA_NOPE + MLA_ROPE)
    wqb = jnp.pad(wqb, ((0, 0), (0, 0), (0, MLA_QK_PAD - MLA_NOPE - MLA_ROPE)))
    wqb = wqb.reshape(MLA_Q_RANK, MLA_HEADS * MLA_QK_PAD).astype(jnp.bfloat16)
    wkvb = p['mla_w_kv_b'].astype(jnp.bfloat16)
    gbias = jnp.concatenate([p['ml_i_bias'][0], p['ml_f_bias'][0], p['ml_i_bias'][1], p['ml_f_bias'][1]])
    gbias = _pad_cols(gbias[None, :], LANES)
    qh, kh = p['mla_q_head_norm'], p['mla_k_head_norm']
    hg = jnp.stack([qh[:MLA_NOPE], jnp.pad(qh[MLA_NOPE:], (0, LANES - MLA_ROPE)),
                    kh[:MLA_NOPE], jnp.pad(kh[MLA_NOPE:], (0, LANES - MLA_ROPE)),
                    p['dil_q_norm'], p['dil_k_norm'],
                    jnp.zeros((LANES,), jnp.float32), jnp.zeros((LANES,), jnp.float32)])

    (mlq, mlkt, mlv, mlo, gates, gates_t, qt, k, vt, dqt, dk, dvt) = _inproj(
        x2, p['norm_mix'][None, :], w, wqb, wkvb, gbias, p['mla_q_norm'][None, :],
        p['mla_kv_norm'][None, :], hg, cosb, sinb, cosc, sinc, seq=seq, tm=MLA_TILE)

    chunk = min(256, seq)
    h_bwd = _mlstm(mlq, mlkt, mlv, gates, gates_t, None, None, None, batch=batch, seq=seq, chunk=chunk,
                   reverse=True)
    ya = _mlstm(mlq, mlkt, mlv, gates, gates_t, h_bwd, mlo, p['ml_out_norm'], batch=batch, seq=seq,
                chunk=chunk, reverse=False)
    yb = _mla_attention(qt, k, vt, batch=batch, seq=seq, unroll=min(16, seq // MLA_TILE))
    yc = _dil_attention(dqt, dk, dvt, dil_bias, batch=batch, seq=seq)

    x2 = _outproj(x2, ya.reshape(batch * seq, ML_W), yb, yc, _to_bf16(stacked['w_out'], layer),
                  tm=min(512, seq))
    return _ffn(x2, p['norm_ff'][None, :], _to_bf16(stacked['w_ff1'], layer),
                _to_bf16(stacked['w_ff2'], layer), tm=min(512, seq), tf=2048)


def kernel(x, norm_mix, w_in, ml_i_bias, ml_f_bias, ml_out_norm, mla_q_norm, mla_w_q_b, mla_kv_norm,
           mla_w_kv_b, mla_q_head_norm, mla_k_head_norm, dil_q_norm, dil_k_norm, w_out, norm_ff,
           w_ff1, w_ff2):
    batch, seq, d = x.shape
    assert seq % (DIL_SUB * DIL_TILE) == 0
    params = dict(norm_mix=norm_mix, w_in=w_in, ml_i_bias=ml_i_bias, ml_f_bias=ml_f_bias,
                  ml_out_norm=ml_out_norm, mla_q_norm=mla_q_norm, mla_w_q_b=mla_w_q_b,
                  mla_kv_norm=mla_kv_norm, mla_w_kv_b=mla_w_kv_b, mla_q_head_norm=mla_q_head_norm,
                  mla_k_head_norm=mla_k_head_norm, dil_q_norm=dil_q_norm, dil_k_norm=dil_k_norm,
                  norm_ff=norm_ff)
    stacked = dict(w_out=w_out, w_ff1=w_ff1, w_ff2=w_ff2)
    cosb, sinb = _rope_tables(seq, MLA_ROPE, 0.0)
    cosc, sinc = _rope_tables(seq, DIL_ROT, 1.0)
    tables = (cosb, sinb, cosc, sinc, jnp.asarray(_dil_bias_table()))
    x2 = x.reshape(batch * seq, d)
    for layer in range(norm_mix.shape[0]):
        x2 = _layer(x2, {name: val[layer] for name, val in params.items()}, stacked, layer, tables,
                    batch=batch, seq=seq)
    return x2.reshape(batch, seq, d)
```

```python
import functools
import math

import numpy as np
import jax
import jax.numpy as jnp
from jax import lax
from jax.experimental import pallas as pl
from jax.experimental.pallas import tpu as pltpu

EPS = 1e-6
NEG_INF = -1e30
ROPE_THETA = 500000.0

ML_HEADS = 4
ML_DK = 64
ML_DV = 128
ML_W = ML_HEADS * ML_DV

MLA_HEADS = 6
MLA_Q_RANK = 384
MLA_KV_RANK = 128
MLA_NOPE = 128
MLA_ROPE = 64
MLA_V = 128
MLA_W = MLA_HEADS * MLA_V
MLA_QK_PAD = 256
MLA_VT_ROWS = 144
MLA_TILE = 512

DIL_HEADS = 6
DIL_DH = 128
DIL_ROT = DIL_DH // 4
DIL_PAIRS = ((128, 1), (512, 4), (2048, 16))
DIL_W = DIL_HEADS * DIL_DH

LANES = 128
VMEM_LIMIT_BYTES = 60 * 1024 * 1024

_C_MLQ = 0
_C_MLK = 256
_C_MLV = 512
_C_MLO = 1024
_C_GATE = 1536
_C_CQ = 1664
_C_CKV = 2048
_C_DQ = 2304
_C_DK = 3072
_C_DV = 3840
_C_END = 4608


def _rms(x, gain, n):
    ms = jnp.sum(x * x, axis=-1, keepdims=True) * (1.0 / n)
    return x * lax.rsqrt(ms + EPS) * gain


def _rope_tile(x, cos_f, sin_f, half):
    lane = lax.broadcasted_iota(jnp.int32, x.shape, 1)
    sw = jnp.where(lane < half, pltpu.roll(x, LANES - half, 1), pltpu.roll(x, half, 1))
    return x * cos_f + sw * sin_f


def _log_sigmoid(x):
    return jnp.minimum(x, 0.0) - jnp.log1p(jnp.exp(-jnp.abs(x)))


def _inproj_kernel(x_ref, gmix_ref, w_ref, wqb_ref, wkvb_ref, gbias_ref, qlat_ref, kvlat_ref,
                   hg_ref, cosb_ref, sinb_ref, cosc_ref, sinc_ref,
                   mlq_ref, mlkt_ref, mlv_ref, mlo_ref, gate_ref, gatet_ref, qt_ref, k_ref, vt_ref,
                   dqt_ref, dk_ref, dvt_ref, *, mla_scale, dil_scale):
    x = x_ref[...]
    h = _rms(x, gmix_ref[...], x.shape[-1]).astype(jnp.bfloat16)

    def proj(c0, c1):
        return jnp.dot(h, w_ref[:, c0:c1], preferred_element_type=jnp.float32)

    cos_b = cosb_ref[...]
    sin_b = sinb_ref[...]
    cos_c = cosc_ref[...]
    sin_c = sinc_ref[...]
    qg_n = hg_ref[0:1, :]
    qg_r = hg_ref[1:2, :]
    kg_n = hg_ref[2:3, :]
    kg_r = hg_ref[3:4, :]
    dqg = hg_ref[4:5, :]
    dkg = hg_ref[5:6, :]

    cq = _rms(proj(_C_CQ, _C_CKV), qlat_ref[...], MLA_Q_RANK).astype(jnp.bfloat16)
    zc = proj(_C_CKV, _C_DQ)
    ckv = _rms(zc[:, :MLA_KV_RANK], kvlat_ref[...], MLA_KV_RANK).astype(jnp.bfloat16)
    zdq = proj(_C_DQ, _C_DK)
    zdk = proj(_C_DK, _C_DV)
    zq = jnp.dot(cq, wqb_ref[...], preferred_element_type=jnp.float32)
    zkv = jnp.dot(ckv, wkvb_ref[...], preferred_element_type=jnp.float32)

    k_rope = _rope_tile(_rms(zc[:, MLA_KV_RANK:], kg_r, MLA_ROPE), cos_b, sin_b, MLA_ROPE // 2)
    k_rope = k_rope.astype(jnp.bfloat16)
    pad_rows = MLA_VT_ROWS - MLA_V

    def ones_row_tile(n):
        first = lax.broadcasted_iota(jnp.int32, (pad_rows, n), 0) == 0
        return jnp.where(first, 1.0, 0.0).astype(jnp.bfloat16)

    ones_row = ones_row_tile(x.shape[0])
    ones_row_dil = ones_row_tile(DIL_TILE)
    for hd in range(MLA_HEADS):
        c = hd * MLA_QK_PAD
        q_nope = _rms(zq[:, c:c + MLA_NOPE], qg_n, MLA_NOPE)
        q_rope = _rope_tile(_rms(zq[:, c + MLA_NOPE:c + MLA_QK_PAD], qg_r, MLA_ROPE),
                            cos_b, sin_b, MLA_ROPE // 2)
        qt_ref[hd, 0, 0:MLA_NOPE, :] = (q_nope * mla_scale).T.astype(jnp.bfloat16)
        qt_ref[hd, 0, MLA_NOPE:MLA_QK_PAD, :] = (q_rope * mla_scale).T.astype(jnp.bfloat16)
        k_nope = _rms(zkv[:, c:c + MLA_NOPE], kg_n, MLA_NOPE)
        k_ref[hd, :, 0:MLA_NOPE] = k_nope.astype(jnp.bfloat16)
        k_ref[hd, :, MLA_NOPE:MLA_QK_PAD] = k_rope
        vt_ref[hd, 0, 0:MLA_V, :] = zkv[:, c + MLA_NOPE:c + MLA_QK_PAD].T.astype(jnp.bfloat16)
        vt_ref[hd, 0, MLA_V:MLA_VT_ROWS, :] = ones_row

    mlkt_ref[...] = proj(_C_MLK, _C_MLV).T.astype(jnp.bfloat16)
    mlo_ref[...] = jax.nn.sigmoid(proj(_C_MLO, _C_GATE)).astype(jnp.bfloat16)
    g = proj(_C_GATE, _C_CQ) + gbias_ref[...]
    lane = lax.broadcasted_iota(jnp.int32, g.shape, 1)
    is_forget = (lane % 8) >= 4
    g = jnp.where(is_forget, _log_sigmoid(g), g)
    gate_ref[...] = g
    gatet_ref[...] = g.T

    for hd in range(DIL_HEADS):
        c = hd * DIL_DH
        qh = _rope_tile(_rms(zdq[:, c:c + DIL_DH], dqg, DIL_DH), cos_c, sin_c, DIL_ROT // 2)
        kh = _rope_tile(_rms(zdk[:, c:c + DIL_DH], dkg, DIL_DH), cos_c, sin_c, DIL_ROT // 2)
        qh = qh * dil_scale
        for sb in range(x.shape[0] // DIL_TILE):
            dqt_ref[hd, sb] = qh[sb * DIL_TILE:(sb + 1) * DIL_TILE].T.astype(jnp.bfloat16)
        dk_ref[:, c:c + DIL_DH] = kh.astype(jnp.bfloat16)
    zdv = proj(_C_DV, _C_END)
    for hd in range(DIL_HEADS):
        c = hd * DIL_DH
        for sb in range(x.shape[0] // DIL_TILE):
            rows = slice(sb * DIL_TILE, (sb + 1) * DIL_TILE)
            dvt_ref[hd, sb, 0:DIL_DH, :] = zdv[rows, c:c + DIL_DH].T.astype(jnp.bfloat16)
            dvt_ref[hd, sb, DIL_DH:MLA_VT_ROWS, :] = ones_row_dil

    mlq_ref[...] = (proj(_C_MLQ, _C_MLK) * (ML_DK ** -0.5)).astype(jnp.bfloat16)
    mlv_ref[...] = proj(_C_MLV, _C_MLO).astype(jnp.bfloat16)


def _const_spec(shape):
    n = len(shape)
    return pl.BlockSpec(shape, lambda *_: (0,) * n, pipeline_mode=pl.Buffered(1))


def _inproj(x2, gmix, w, wqb, wkvb, gbias, qlat, kvlat, hg, cosb, sinb, cosc, sinc, *, seq, tm):
    t, d = x2.shape
    nt = t // tm
    ns = seq // tm
    tok = lambda n: pl.BlockSpec((tm, n), lambda i: (i, 0))
    pos = pl.BlockSpec((tm, LANES), lambda i: (i % ns, 0))
    headed = lambda n: pl.BlockSpec((MLA_HEADS, tm, n), lambda i: (0, i, 0))
    headed_t = lambda n: pl.BlockSpec((MLA_HEADS, 1, n, tm), lambda i: (0, i, 0, 0))
    bf = jnp.bfloat16
    out_shape = (
        jax.ShapeDtypeStruct((t, 256), bf), jax.ShapeDtypeStruct((256, t), bf),
        jax.ShapeDtypeStruct((t, ML_W), bf), jax.ShapeDtypeStruct((t, ML_W), bf),
        jax.ShapeDtypeStruct((t, LANES), jnp.float32), jax.ShapeDtypeStruct((LANES, t), jnp.float32),
        jax.ShapeDtypeStruct((MLA_HEADS, nt, MLA_QK_PAD, tm), bf),
        jax.ShapeDtypeStruct((MLA_HEADS, t, MLA_QK_PAD), bf),
        jax.ShapeDtypeStruct((MLA_HEADS, nt, MLA_VT_ROWS, tm), bf),
        jax.ShapeDtypeStruct((DIL_HEADS, t // DIL_TILE, DIL_DH, DIL_TILE), bf),
        jax.ShapeDtypeStruct((t, DIL_W), bf),
        jax.ShapeDtypeStruct((DIL_HEADS, t // DIL_TILE, MLA_VT_ROWS, DIL_TILE), bf),
    )
    dil_t = lambda n: pl.BlockSpec((DIL_HEADS, tm // DIL_TILE, n, DIL_TILE), lambda i: (0, i, 0, 0))
    tok_t = lambda n: pl.BlockSpec((n, tm), lambda i: (0, i))
    out_specs = (tok(256), tok_t(256), tok(ML_W), tok(ML_W), tok(LANES), tok_t(LANES),
                 headed_t(MLA_QK_PAD), headed(MLA_QK_PAD), headed_t(MLA_VT_ROWS),
                 dil_t(DIL_DH), tok(DIL_W), dil_t(MLA_VT_ROWS))
    in_specs = [tok(d), _const_spec(gmix.shape), _const_spec(w.shape), _const_spec(wqb.shape),
                _const_spec(wkvb.shape), _const_spec(gbias.shape), _const_spec(qlat.shape),
                _const_spec(kvlat.shape), _const_spec(hg.shape), pos, pos, pos, pos]
    kern = functools.partial(_inproj_kernel,
                             mla_scale=(MLA_NOPE + MLA_ROPE) ** -0.5 * math.log2(math.e),
                             dil_scale=DIL_DH ** -0.5 * math.log2(math.e))
    return pl.pallas_call(
        kern, grid=(nt,), in_specs=in_specs, out_specs=out_specs, out_shape=out_shape,
        compiler_params=pltpu.CompilerParams(dimension_semantics=("arbitrary",),
                                             vmem_limit_bytes=VMEM_LIMIT_BYTES),
        name="inproj",
    )(x2, gmix, w, wqb, wkvb, gbias, qlat, kvlat, hg, cosb, sinb, cosc, sinc)


def _mlstm_kernel(*refs, reverse, chunk, nb):
    q_ref, v_ref, gate_ref = refs[0:3]
    kt_refs = refs[3:3 + nb]
    gatet_refs = refs[3 + nb:3 + 2 * nb]
    out_ref, ct_ref, m_ref = refs[3 + 2 * nb:]
    L = chunk
    c = pl.program_id(0)

    @pl.when(c == 0)
    def _():
        ct_ref[...] = jnp.zeros_like(ct_ref)
        m_ref[...] = jnp.zeros_like(m_ref)

    row = lax.broadcasted_iota(jnp.int32, (L, L), 0)
    col = lax.broadcasted_iota(jnp.int32, (L, L), 1)
    causal = (col >= row) if reverse else (col <= row)
    vis = causal.astype(jnp.bfloat16)
    vis_t = ((row >= col) if reverse else (row <= col)).astype(jnp.bfloat16)

    def split3(x):
        hi = x.astype(jnp.bfloat16)
        rest_ = x - hi.astype(jnp.float32)
        mid = rest_.astype(jnp.bfloat16)
        return hi, mid, (rest_ - mid.astype(jnp.float32)).astype(jnp.bfloat16)

    def rep(col_):
        return jnp.broadcast_to(col_, (L, LANES))

    def wide(xb, n):
        return xb if n == LANES else jnp.concatenate([xb] * (n // LANES), axis=1)

    lane = lax.broadcasted_iota(jnp.int32, (L, LANES), 1)
    ones_col = jnp.where(lane == 0, 1.0, 0.0).astype(jnp.bfloat16)
    zero_half = jnp.zeros((ML_DK, L), jnp.bfloat16)

    d0 = 8 if reverse else 0
    gates, slab, cum_t, cum = [], [], [], []
    for bi in range(nb):
        gates.append(gate_ref[bi])
        slab.append(gatet_refs[bi][d0:d0 + 8, :])
        cum_t.append(sum(jnp.dot(vis, part, preferred_element_type=jnp.float32)
                         for part in split3(gates[bi])))
        cum.append(sum(jnp.dot(part, vis_t, preferred_element_type=jnp.float32)
                       for part in split3(slab[bi])))

    for hd in range(ML_HEADS):
        for bi in range(nb):
            st = bi * ML_HEADS + hd
            i_b = rep(gates[bi][:, d0 + hd:d0 + hd + 1])
            i_row = slab[bi][hd:hd + 1, :]
            cum_b = rep(cum_t[bi][:, d0 + 4 + hd:d0 + 5 + hd])
            cum_row = cum[bi][4 + hd:5 + hd, :]
            tot = cum_row[:, 0:1] if reverse else cum_row[:, L - 1:L]
            m_prev = m_ref[st:st + 1, 0:1]

            pair, half = hd // 2, hd % 2
            in_head = (lane >= half * ML_DK) & (lane < (half + 1) * ML_DK)
            qp = q_ref[bi, :, pair * LANES:(pair + 1) * LANES]
            qm = jnp.where(in_head, qp, jnp.zeros_like(qp))
            kt_ref = kt_refs[bi]
            kt_pair = kt_ref[pair * LANES:(pair + 1) * LANES, :]
            kt_h = kt_ref[pair * LANES + half * ML_DK:pair * LANES + (half + 1) * ML_DK, :]
            kmt = jnp.concatenate([kt_h, zero_half] if half == 0 else [zero_half, kt_h], axis=0)
            v_h = v_ref[bi, :, hd * ML_DV:(hd + 1) * ML_DV]
            v_aug = jnp.concatenate([v_h, ones_col], axis=1)

            d_mat = jnp.where(causal, wide(cum_b, L) + (i_row - cum_row), NEG_INF)
            d_inter = cum_b + m_prev
            m_t = jnp.maximum(d_inter, rep(jnp.max(d_mat, axis=1, keepdims=True)))
            w_intra = jnp.exp(d_mat - wide(m_t, L))
            w_inter = jnp.exp(d_inter - m_t)
            s_raw = jnp.dot(qm, kt_pair, preferred_element_type=jnp.float32)
            sw = (s_raw * w_intra).astype(jnp.bfloat16)
            ct = ct_ref[st]
            r = (jnp.dot(sw, v_aug, preferred_element_type=jnp.float32)
                 + wide(w_inter, 2 * LANES) * jnp.dot(qm, ct.astype(jnp.bfloat16),
                                                      preferred_element_type=jnp.float32))
            num = r[:, :ML_DV]
            den = rep(r[:, ML_DV:ML_DV + 1])
            h_dir = num / jnp.maximum(jnp.abs(den), jnp.exp(-m_t))

            d_state = tot - cum_b + i_b
            m_new = jnp.maximum(tot + m_prev, jnp.max(d_state, axis=0, keepdims=True)[:, 0:1])
            w_s = jnp.exp(d_state - m_new)
            w_c = jnp.exp(tot + m_prev - m_new)
            vw = (v_aug.astype(jnp.float32) * wide(w_s, 2 * LANES)).astype(jnp.bfloat16)
            ct_ref[st] = w_c * ct + jnp.dot(kmt, vw, preferred_element_type=jnp.float32)
            m_ref[st:st + 1, :] = jnp.broadcast_to(m_new, (1, LANES))

            out_ref[bi, :, hd * ML_DV:(hd + 1) * ML_DV] = h_dir


def _mlstm(mlq, mlkt, mlv, gates, gates_t, *, batch, seq, chunk, reverse):
    nc = seq // chunk
    blk = (lambda c: nc - 1 - c) if reverse else (lambda c: c)
    tok = lambda n: pl.BlockSpec((batch, chunk, n), lambda c: (0, blk(c), 0))
    tok_t = lambda n, bi: pl.BlockSpec((n, chunk), lambda c: (0, bi * nc + blk(c)))
    view = lambda a: a.reshape(batch, seq, a.shape[-1])
    in_specs = ([tok(256), tok(ML_W), tok(LANES)]
                + [tok_t(256, bi) for bi in range(batch)] + [tok_t(LANES, bi) for bi in range(batch)])
    args = [view(mlq), view(mlv), view(gates)] + [mlkt] * batch + [gates_t] * batch
    kern = functools.partial(_mlstm_kernel, reverse=reverse, chunk=chunk, nb=batch)
    states = batch * ML_HEADS
    out = pl.pallas_call(
        kern, grid=(nc,), in_specs=in_specs, out_specs=tok(ML_W),
        out_shape=jax.ShapeDtypeStruct((batch, seq, ML_W), jnp.float32),
        scratch_shapes=[pltpu.VMEM((states, LANES, 2 * LANES), jnp.float32),
                        pltpu.VMEM((-(-states // 8) * 8, LANES), jnp.float32)],
        compiler_params=pltpu.CompilerParams(dimension_semantics=("arbitrary",),
                                             vmem_limit_bytes=VMEM_LIMIT_BYTES),
        name="mlstm_bwd" if reverse else "mlstm_fwd",
    )(*args)
    return out.reshape(batch * seq, ML_W)


MLA_QSUB = 2


def _mla_kernel(qt_ref, k_ref, vt_ref, o_ref, s_scr, acc_scr, *, unroll):
    tq = qt_ref.shape[2]
    tk = vt_ref.shape[2]
    nk = vt_ref.shape[0]

    def attend(sub):
        qt = qt_ref[sub]

        def scores(slot, j):
            start = pl.multiple_of(j * tk, tk)
            s = jnp.dot(k_ref[pl.ds(start, tk), :], qt, preferred_element_type=jnp.float32)
            s_scr[sub, slot] = s
            return jnp.max(s, axis=0, keepdims=True)

        def consume(slot, j, m_prev, m_tile):
            m_new = jnp.maximum(m_prev, m_tile)
            alpha = jnp.exp2(m_prev - m_new)
            p = jnp.exp2(s_scr[sub, slot] - m_new).astype(jnp.bfloat16)
            acc_scr[sub] = alpha * acc_scr[sub] + jnp.dot(vt_ref[j], p,
                                                          preferred_element_type=jnp.float32)
            return m_new

        m_first = scores(0, 0)
        acc_scr[sub] = jnp.zeros(acc_scr.shape[1:], acc_scr.dtype)

        def body(jj, carry):
            m, m_tile = carry
            j = unroll * jj
            for u in range(unroll):
                m_next = m_tile
                if not (unroll == nk and u == unroll - 1):
                    m_next = scores((u + 1) % 2, jnp.minimum(j + u + 1, nk - 1))
                m = consume(u % 2, j + u, m, m_tile)
                m_tile = m_next
            return m, m_tile

        lax.fori_loop(0, nk // unroll, body, (jnp.full((1, tq), NEG_INF, jnp.float32), m_first))
        acc = acc_scr[sub]
        o_ref[sub * tq:(sub + 1) * tq, :] = (acc[:MLA_V] / acc[MLA_V:MLA_V + 1]).T.astype(o_ref.dtype)

    for sub in range(MLA_QSUB):
        attend(sub)


def _mla_attention(qt, k, vt, *, batch, seq, unroll):
    tile = qt.shape[-1]
    t = k.shape[1]
    nq = seq // tile
    nstep = nq // MLA_QSUB
    assert unroll % 2 == 0 and nq % unroll == 0 and nq % MLA_QSUB == 0
    return pl.pallas_call(
        functools.partial(_mla_kernel, unroll=unroll),
        grid=(batch, MLA_HEADS, nstep),
        in_specs=[pl.BlockSpec((None, MLA_QSUB, MLA_QK_PAD, tile), lambda b, h, i: (h, b * nstep + i, 0, 0)),
                  pl.BlockSpec((None, seq, MLA_QK_PAD), lambda b, h, i: (h, b, 0)),
                  pl.BlockSpec((None, nq, MLA_VT_ROWS, tile), lambda b, h, i: (h, b, 0, 0))],
        out_specs=pl.BlockSpec((MLA_QSUB * tile, MLA_V), lambda b, h, i: (b * nstep + i, h)),
        out_shape=jax.ShapeDtypeStruct((t, MLA_W), jnp.bfloat16),
        scratch_shapes=[pltpu.VMEM((MLA_QSUB, 2, tile, tile), jnp.float32),
                        pltpu.VMEM((MLA_QSUB, MLA_VT_ROWS, tile), jnp.float32)],
        compiler_params=pltpu.CompilerParams(
            dimension_semantics=("arbitrary", "arbitrary", "arbitrary"),
            vmem_limit_bytes=VMEM_LIMIT_BYTES),
        name="mla_attn",
    )(qt, k, vt)


DIL_TILE = 256
DIL_REACH = max(w // 2 for w, _ in DIL_PAIRS)
DIL_NOFF = DIL_REACH // DIL_TILE
DIL_NWIN = 2 * DIL_NOFF + 1
DIL_SUB = 8


def _dil_bias_table():
    r = np.arange(DIL_TILE)[:, None]
    c = np.arange(DIL_TILE)[None, :]
    tiles = []
    for o in range(-2 * DIL_NOFF, 2 * DIL_NOFF + 1):
        delta = o * DIL_TILE + r - c
        mult = np.zeros_like(delta)
        for window, dil in DIL_PAIRS:
            mult += ((delta % dil) == 0) & (np.abs(delta) <= window // 2)
        tiles.append(np.where(mult > 0, np.log2(np.maximum(mult, 1)), NEG_INF))
    return np.stack(tiles).astype(np.float32)


def _dil_kernel(qt_ref, k_ref, vt_ref, bias_ref, o_ref, s_scr, *, nq, nwin):
    i = pl.program_id(2)
    subs = range(DIL_SUB)
    qi = [i * DIL_SUB + sub for sub in subs]
    w0 = [jnp.clip(t - DIL_NOFF, 0, nq - nwin) for t in qi]

    def score_pass(sub):
        m = None
        for idx in range(nwin):
            start = pl.multiple_of((w0[sub] + idx) * DIL_TILE, DIL_TILE)
            bias = bias_ref[w0[sub] + idx - qi[sub] + 2 * DIL_NOFF]
            s = jnp.dot(k_ref[pl.ds(start, DIL_TILE), :], qt_ref[sub],
                        preferred_element_type=jnp.float32) + bias
            s_scr[sub, idx * DIL_TILE:(idx + 1) * DIL_TILE, :] = s
            m_cur = jnp.max(s, axis=0, keepdims=True)
            m = m_cur if idx == 0 else jnp.maximum(m, m_cur)
        return m

    def value_pass(sub, m):
        acc = None
        for idx in range(nwin):
            p = jnp.exp2(s_scr[sub, idx * DIL_TILE:(idx + 1) * DIL_TILE, :] - m)
            part = jnp.dot(vt_ref[w0[sub] + idx], p.astype(jnp.bfloat16),
                           preferred_element_type=jnp.float32)
            acc = part if acc is None else acc + part
        out_t = acc[:DIL_DH] / acc[DIL_DH:DIL_DH + 1]
        o_ref[sub * DIL_TILE:(sub + 1) * DIL_TILE, :] = out_t.T.astype(o_ref.dtype)

    m_prev = score_pass(0)
    for sub in range(1, DIL_SUB):
        m_next = score_pass(sub)
        value_pass(sub - 1, m_prev)
        m_prev = m_next
    value_pass(DIL_SUB - 1, m_prev)


def _dil_attention(dqt, dk, dvt, bias, *, batch, seq):
    t = dk.shape[0]
    nq = seq // DIL_TILE
    nwin = min(DIL_NWIN, nq)
    tq = DIL_SUB * DIL_TILE
    nstep = seq // tq
    return pl.pallas_call(
        functools.partial(_dil_kernel, nq=nq, nwin=nwin),
        grid=(batch, DIL_HEADS, nstep),
        in_specs=[pl.BlockSpec((None, DIL_SUB, DIL_DH, DIL_TILE), lambda b, h, i: (h, b * nstep + i, 0, 0)),
                  pl.BlockSpec((seq, DIL_DH), lambda b, h, i: (b, h)),
                  pl.BlockSpec((None, nq, MLA_VT_ROWS, DIL_TILE), lambda b, h, i: (h, b, 0, 0)),
                  _const_spec(bias.shape)],
        out_specs=pl.BlockSpec((tq, DIL_DH), lambda b, h, i: (b * nstep + i, h)),
        out_shape=jax.ShapeDtypeStruct((t, DIL_W), jnp.bfloat16),
        scratch_shapes=[pltpu.VMEM((DIL_SUB, nwin * DIL_TILE, DIL_TILE), jnp.float32)],
        compiler_params=pltpu.CompilerParams(
            dimension_semantics=("arbitrary", "arbitrary", "arbitrary"),
            vmem_limit_bytes=VMEM_LIMIT_BYTES),
        name="dil_attn",
    )(dqt, dk, dvt, bias)


def _outproj_kernel(x_ref, hf_ref, hb_ref, mlo_ref, onorm_ref, yb_ref, yc_ref, w_ref, o_ref):
    h_sum = hf_ref[...] + hb_ref[...]
    gate = mlo_ref[...].astype(jnp.float32)
    ya = jnp.concatenate(
        [gate[:, hd * ML_DV:(hd + 1) * ML_DV]
         * _rms(h_sum[:, hd * ML_DV:(hd + 1) * ML_DV], onorm_ref[hd:hd + 1, :], ML_DV)
         for hd in range(ML_HEADS)], axis=1).astype(jnp.bfloat16)
    b0, b1 = ML_W, ML_W + MLA_W
    y = (jnp.dot(ya, w_ref[0:b0, :], preferred_element_type=jnp.float32)
         + jnp.dot(yb_ref[...], w_ref[b0:b1, :], preferred_element_type=jnp.float32)
         + jnp.dot(yc_ref[...], w_ref[b1:, :], preferred_element_type=jnp.float32))
    o_ref[...] = x_ref[...] + y


def _outproj(x2, hf, hb, mlo, onorm, yb, yc, w, *, tm):
    t, d = x2.shape
    tok = lambda n: pl.BlockSpec((tm, n), lambda i: (i, 0))
    return pl.pallas_call(
        _outproj_kernel, grid=(t // tm,),
        in_specs=[tok(d), tok(ML_W), tok(ML_W), tok(ML_W), _const_spec(onorm.shape),
                  tok(MLA_W), tok(DIL_W), _const_spec(w.shape)],
        out_specs=tok(d), out_shape=jax.ShapeDtypeStruct((t, d), jnp.float32),
        compiler_params=pltpu.CompilerParams(dimension_semantics=("arbitrary",),
                                             vmem_limit_bytes=VMEM_LIMIT_BYTES),
        name="outproj",
    )(x2, hf, hb, mlo, onorm, yb, yc, w)


def _ffn_kernel(x_ref, g_ref, w1_ref, w2_ref, o_ref, h_ref):
    j = pl.program_id(1)

    @pl.when(j == 0)
    def _():
        x = x_ref[...]
        h_ref[...] = _rms(x, g_ref[...], x.shape[-1]).astype(h_ref.dtype)
        o_ref[...] = x

    u = jnp.maximum(jnp.dot(h_ref[...], w1_ref[...], preferred_element_type=jnp.float32), 0.0)
    u = (u * u).astype(jnp.bfloat16)
    o_ref[...] += jnp.dot(u, w2_ref[...], preferred_element_type=jnp.float32)


def _ffn(x2, g, w1, w2, *, tm, tf):
    t, d = x2.shape
    dff = w1.shape[1]
    return pl.pallas_call(
        _ffn_kernel, grid=(t // tm, dff // tf),
        in_specs=[pl.BlockSpec((tm, d), lambda i, j: (i, 0)),
                  pl.BlockSpec(g.shape, lambda i, j: (0, 0)),
                  pl.BlockSpec((d, tf), lambda i, j: (0, j)),
                  pl.BlockSpec((tf, d), lambda i, j: (j, 0))],
        out_specs=pl.BlockSpec((tm, d), lambda i, j: (i, 0)),
        out_shape=jax.ShapeDtypeStruct((t, d), jnp.float32),
        scratch_shapes=[pltpu.VMEM((tm, d), jnp.bfloat16)],
        compiler_params=pltpu.CompilerParams(dimension_semantics=("arbitrary", "arbitrary"),
                                             vmem_limit_bytes=VMEM_LIMIT_BYTES),
        name="ffn",
    )(x2, g, w1, w2)


def _rope_tables(seq, rot_dim, fill_cos):
    pos = jnp.arange(seq, dtype=jnp.float32)
    inv_freq = ROPE_THETA ** (-jnp.arange(0, rot_dim, 2, dtype=jnp.float32) / rot_dim)
    ang = pos[:, None] * inv_freq[None, :]
    cos, sin = jnp.cos(ang), jnp.sin(ang)
    pad = LANES - rot_dim
    cos_f = jnp.concatenate([cos, cos, jnp.full((seq, pad), fill_cos, jnp.float32)], axis=1)
    sin_f = jnp.concatenate([-sin, sin, jnp.zeros((seq, pad), jnp.float32)], axis=1)
    return cos_f, sin_f


def _pad_cols(a, n):
    return jnp.pad(a, ((0, 0), (0, n - a.shape[1])))


def _pack_w_in(w_in):
    splits = np.cumsum((256, 256, ML_W, ML_W, 4 * ML_HEADS, MLA_Q_RANK, MLA_KV_RANK + MLA_ROPE,
                        DIL_W, DIL_W, DIL_W))[:-1].tolist()
    (wq, wk, wv, wo, wg, wcq, wckv, wdq, wdk, wdv) = jnp.split(w_in, splits, axis=1)
    packed = jnp.concatenate([wq, wk, wv, wo, _pad_cols(wg, LANES), wcq, _pad_cols(wckv, 256),
                              wdq, wdk, wdv], axis=1)
    return packed.astype(jnp.bfloat16)


def _cast_kernel(x_ref, o_ref):
    o_ref[...] = x_ref[...].astype(o_ref.dtype)


def _to_bf16(stacked, layer, block_bytes=8 * 1024 * 1024):
    _, rows, cols = stacked.shape
    br = min(rows, max(8, block_bytes // (4 * cols)))
    assert rows % br == 0
    return pl.pallas_call(
        _cast_kernel, grid=(rows // br,),
        in_specs=[pl.BlockSpec((None, br, cols), lambda i: (layer, i, 0))],
        out_specs=pl.BlockSpec((br, cols), lambda i: (i, 0)),
        out_shape=jax.ShapeDtypeStruct((rows, cols), jnp.bfloat16),
        compiler_params=pltpu.CompilerParams(dimension_semantics=("arbitrary",),
                                             vmem_limit_bytes=VMEM_LIMIT_BYTES),
        name="to_bf16",
    )(stacked)


def _layer(x2, p, stacked, layer, tables, *, batch, seq):
    cosb, sinb, cosc, sinc, dil_bias = tables
    w = _pack_w_in(p['w_in'])
    wqb = p['mla_w_q_b'].reshape(MLA_Q_RANK, MLA_HEADS, MLA_NOPE + MLA_ROPE)
    wqb = jnp.pad(wqb, ((0, 0), (0, 0), (0, MLA_QK_PAD - MLA_NOPE - MLA_ROPE)))
    wqb = wqb.reshape(MLA_Q_RANK, MLA_HEADS * MLA_QK_PAD).astype(jnp.bfloat16)
    wkvb = p['mla_w_kv_b'].astype(jnp.bfloat16)
    gbias = jnp.concatenate([p['ml_i_bias'][0], p['ml_f_bias'][0], p['ml_i_bias'][1], p['ml_f_bias'][1]])
    gbias = _pad_cols(gbias[None, :], LANES)
    qh, kh = p['mla_q_head_norm'], p['mla_k_head_norm']
    hg = jnp.stack([qh[:MLA_NOPE], jnp.pad(qh[MLA_NOPE:], (0, LANES - MLA_ROPE)),
                    kh[:MLA_NOPE], jnp.pad(kh[MLA_NOPE:], (0, LANES - MLA_ROPE)),
                    p['dil_q_norm'], p['dil_k_norm'],
                    jnp.zeros((LANES,), jnp.float32), jnp.zeros((LANES,), jnp.float32)])

    (mlq, mlkt, mlv, mlo, gates, gates_t, qt, k, vt, dqt, dk, dvt) = _inproj(
        x2, p['norm_mix'][None, :], w, wqb, wkvb, gbias, p['mla_q_norm'][None, :],
        p['mla_kv_norm'][None, :], hg, cosb, sinb, cosc, sinc, seq=seq, tm=MLA_TILE)

    chunk = min(256, seq)
    h_bwd = _mlstm(mlq, mlkt, mlv, gates, gates_t, batch=batch, seq=seq, chunk=chunk, reverse=True)
    h_fwd = _mlstm(mlq, mlkt, mlv, gates, gates_t, batch=batch, seq=seq, chunk=chunk, reverse=False)
    yb = _mla_attention(qt, k, vt, batch=batch, seq=seq, unroll=min(16, seq // MLA_TILE))
    yc = _dil_attention(dqt, dk, dvt, dil_bias, batch=batch, seq=seq)

    x2 = _outproj(x2, h_fwd, h_bwd, mlo, p['ml_out_norm'], yb, yc, _to_bf16(stacked['w_out'], layer),
                  tm=min(512, seq))
    return _ffn(x2, p['norm_ff'][None, :], _to_bf16(stacked['w_ff1'], layer),
                _to_bf16(stacked['w_ff2'], layer), tm=min(512, seq), tf=2048)


def kernel(x, norm_mix, w_in, ml_i_bias, ml_f_bias, ml_out_norm, mla_q_norm, mla_w_q_b, mla_kv_norm,
           mla_w_kv_b, mla_q_head_norm, mla_k_head_norm, dil_q_norm, dil_k_norm, w_out, norm_ff,
           w_ff1, w_ff2):
    batch, seq, d = x.shape
    assert seq % (DIL_SUB * DIL_TILE) == 0
    params = dict(norm_mix=norm_mix, w_in=w_in, ml_i_bias=ml_i_bias, ml_f_bias=ml_f_bias,
                  ml_out_norm=ml_out_norm, mla_q_norm=mla_q_norm, mla_w_q_b=mla_w_q_b,
                  mla_kv_norm=mla_kv_norm, mla_w_kv_b=mla_w_kv_b, mla_q_head_norm=mla_q_head_norm,
                  mla_k_head_norm=mla_k_head_norm, dil_q_norm=dil_q_norm, dil_k_norm=dil_k_norm,
                  norm_ff=norm_ff)
    stacked = dict(w_out=w_out, w_ff1=w_ff1, w_ff2=w_ff2)
    cosb, sinb = _rope_tables(seq, MLA_ROPE, 0.0)
    cosc, sinc = _rope_tables(seq, DIL_ROT, 1.0)
    tables = (cosb, sinb, cosc, sinc, jnp.asarray(_dil_bias_table()))
    x2 = x.reshape(batch * seq, d)
    for layer in range(norm_mix.shape[0]):
        x2 = _layer(x2, {name: val[layer] for name, val in params.items()}, stacked, layer, tables,
                    batch=batch, seq=seq)
    return x2.reshape(batch, seq, d)
```

```python
import functools
import math

import numpy as np
import jax
import jax.numpy as jnp
from jax import lax
from jax.experimental import pallas as pl
from jax.experimental.pallas import tpu as pltpu

EPS = 1e-6
NEG_INF = -1e30
ROPE_THETA = 500000.0

ML_HEADS = 4
ML_DK = 64
ML_DV = 128
ML_W = ML_HEADS * ML_DV

MLA_HEADS = 6
MLA_Q_RANK = 384
MLA_KV_RANK = 128
MLA_NOPE = 128
MLA_ROPE = 64
MLA_V = 128
MLA_W = MLA_HEADS * MLA_V
MLA_QK_PAD = 256
MLA_VT_ROWS = 144
MLA_TILE = 512

DIL_HEADS = 6
DIL_DH = 128
DIL_ROT = DIL_DH // 4
DIL_PAIRS = ((128, 1), (512, 4), (2048, 16))
DIL_W = DIL_HEADS * DIL_DH

LANES = 128
VMEM_LIMIT_BYTES = 60 * 1024 * 1024

ML_CHUNK = 256
MLA_UNROLL = 16
ROW_TILE = 512
FFN_CHUNK = 2048

_C_MLQ = 0
_C_MLK = 256
_C_MLV = 512
_C_MLO = 1024
_C_GATE = 1536
_C_CQ = 1664
_C_CKV = 2048
_C_DQ = 2304
_C_DK = 3072
_C_DV = 3840
_C_END = 4608


def _rms(x, gain, n):
    ms = jnp.sum(x * x, axis=-1, keepdims=True) * (1.0 / n)
    return x * lax.rsqrt(ms + EPS) * gain


def _rope_tile(x, cos_f, sin_f, half):
    lane = lax.broadcasted_iota(jnp.int32, x.shape, 1)
    sw = jnp.where(lane < half, pltpu.roll(x, LANES - half, 1), pltpu.roll(x, half, 1))
    return x * cos_f + sw * sin_f


def _log_sigmoid(x):
    return jnp.minimum(x, 0.0) - jnp.log1p(jnp.exp(-jnp.abs(x)))


def _inproj_kernel(x_ref, gmix_ref, w_ref, wqb_ref, wkvb_ref, gbias_ref, qlat_ref, kvlat_ref,
                   hg_ref, cosb_ref, sinb_ref, cosc_ref, sinc_ref,
                   mlq_ref, mlkt_ref, mlv_ref, mlo_ref, gate_ref, gatet_ref, qt_ref, k_ref, vt_ref,
                   dqt_ref, dk_ref, dvt_ref, *, mla_scale, dil_scale):
    x = x_ref[...]
    h = _rms(x, gmix_ref[...], x.shape[-1]).astype(jnp.bfloat16)

    def proj(c0, c1):
        return jnp.dot(h, w_ref[:, c0:c1], preferred_element_type=jnp.float32)

    cos_b = cosb_ref[...]
    sin_b = sinb_ref[...]
    cos_c = cosc_ref[...]
    sin_c = sinc_ref[...]
    qg_n = hg_ref[0:1, :]
    qg_r = hg_ref[1:2, :]
    kg_n = hg_ref[2:3, :]
    kg_r = hg_ref[3:4, :]
    dqg = hg_ref[4:5, :]
    dkg = hg_ref[5:6, :]

    cq = _rms(proj(_C_CQ, _C_CKV), qlat_ref[...], MLA_Q_RANK).astype(jnp.bfloat16)
    zc = proj(_C_CKV, _C_DQ)
    ckv = _rms(zc[:, :MLA_KV_RANK], kvlat_ref[...], MLA_KV_RANK).astype(jnp.bfloat16)
    zdq = proj(_C_DQ, _C_DK)
    zdk = proj(_C_DK, _C_DV)
    zq = jnp.dot(cq, wqb_ref[...], preferred_element_type=jnp.float32)
    zkv = jnp.dot(ckv, wkvb_ref[...], preferred_element_type=jnp.float32)

    k_rope = _rope_tile(_rms(zc[:, MLA_KV_RANK:], kg_r, MLA_ROPE), cos_b, sin_b, MLA_ROPE // 2)
    k_rope = k_rope.astype(jnp.bfloat16)
    pad_rows = MLA_VT_ROWS - MLA_V

    def ones_row_tile(n):
        first = lax.broadcasted_iota(jnp.int32, (pad_rows, n), 0) == 0
        return jnp.where(first, 1.0, 0.0).astype(jnp.bfloat16)

    ones_row = ones_row_tile(x.shape[0])
    ones_row_dil = ones_row_tile(DIL_TILE)
    for hd in range(MLA_HEADS):
        c = hd * MLA_QK_PAD
        q_nope = _rms(zq[:, c:c + MLA_NOPE], qg_n, MLA_NOPE)
        q_rope = _rope_tile(_rms(zq[:, c + MLA_NOPE:c + MLA_QK_PAD], qg_r, MLA_ROPE),
                            cos_b, sin_b, MLA_ROPE // 2)
        qt_ref[hd, 0, 0:MLA_NOPE, :] = (q_nope * mla_scale).T.astype(jnp.bfloat16)
        qt_ref[hd, 0, MLA_NOPE:MLA_QK_PAD, :] = (q_rope * mla_scale).T.astype(jnp.bfloat16)
        k_nope = _rms(zkv[:, c:c + MLA_NOPE], kg_n, MLA_NOPE)
        k_ref[hd, :, 0:MLA_NOPE] = k_nope.astype(jnp.bfloat16)
        k_ref[hd, :, MLA_NOPE:MLA_QK_PAD] = k_rope
        vt_ref[hd, 0, 0:MLA_V, :] = zkv[:, c + MLA_NOPE:c + MLA_QK_PAD].T.astype(jnp.bfloat16)
        vt_ref[hd, 0, MLA_V:MLA_VT_ROWS, :] = ones_row

    mlkt_ref[...] = proj(_C_MLK, _C_MLV).T.astype(jnp.bfloat16)
    mlo_ref[...] = jax.nn.sigmoid(proj(_C_MLO, _C_GATE)).astype(jnp.bfloat16)
    g = proj(_C_GATE, _C_CQ) + gbias_ref[...]
    lane = lax.broadcasted_iota(jnp.int32, g.shape, 1)
    is_forget = (lane % 8) >= 4
    g = jnp.where(is_forget, _log_sigmoid(g), g)
    gate_ref[...] = g
    gatet_ref[...] = g.T

    for hd in range(DIL_HEADS):
        c = hd * DIL_DH
        qh = _rope_tile(_rms(zdq[:, c:c + DIL_DH], dqg, DIL_DH), cos_c, sin_c, DIL_ROT // 2)
        kh = _rope_tile(_rms(zdk[:, c:c + DIL_DH], dkg, DIL_DH), cos_c, sin_c, DIL_ROT // 2)
        qh = qh * dil_scale
        for sb in range(x.shape[0] // DIL_TILE):
            dqt_ref[hd, sb] = qh[sb * DIL_TILE:(sb + 1) * DIL_TILE].T.astype(jnp.bfloat16)
        dk_ref[:, c:c + DIL_DH] = kh.astype(jnp.bfloat16)
    zdv = proj(_C_DV, _C_END)
    for hd in range(DIL_HEADS):
        c = hd * DIL_DH
        for sb in range(x.shape[0] // DIL_TILE):
            rows = slice(sb * DIL_TILE, (sb + 1) * DIL_TILE)
            dvt_ref[hd, sb, 0:DIL_DH, :] = zdv[rows, c:c + DIL_DH].T.astype(jnp.bfloat16)
            dvt_ref[hd, sb, DIL_DH:MLA_VT_ROWS, :] = ones_row_dil

    mlq_ref[...] = (proj(_C_MLQ, _C_MLK) * (ML_DK ** -0.5)).astype(jnp.bfloat16)
    mlv_ref[...] = proj(_C_MLV, _C_MLO).astype(jnp.bfloat16)


def _const_spec(shape):
    n = len(shape)
    return pl.BlockSpec(shape, lambda *_: (0,) * n, pipeline_mode=pl.Buffered(1))


def _inproj(x2, gmix, w, wqb, wkvb, gbias, qlat, kvlat, hg, cosb, sinb, cosc, sinc, *, seq, tm):
    t, d = x2.shape
    nt = t // tm
    ns = seq // tm
    tok = lambda n: pl.BlockSpec((tm, n), lambda i: (i, 0))
    pos = pl.BlockSpec((tm, LANES), lambda i: (i % ns, 0))
    headed = lambda n: pl.BlockSpec((MLA_HEADS, tm, n), lambda i: (0, i, 0))
    headed_t = lambda n: pl.BlockSpec((MLA_HEADS, 1, n, tm), lambda i: (0, i, 0, 0))
    bf = jnp.bfloat16
    out_shape = (
        jax.ShapeDtypeStruct((t, 256), bf), jax.ShapeDtypeStruct((256, t), bf),
        jax.ShapeDtypeStruct((t, ML_W), bf), jax.ShapeDtypeStruct((t, ML_W), bf),
        jax.ShapeDtypeStruct((t, LANES), jnp.float32), jax.ShapeDtypeStruct((LANES, t), jnp.float32),
        jax.ShapeDtypeStruct((MLA_HEADS, nt, MLA_QK_PAD, tm), bf),
        jax.ShapeDtypeStruct((MLA_HEADS, t, MLA_QK_PAD), bf),
        jax.ShapeDtypeStruct((MLA_HEADS, nt, MLA_VT_ROWS, tm), bf),
        jax.ShapeDtypeStruct((DIL_HEADS, t // DIL_TILE, DIL_DH, DIL_TILE), bf),
        jax.ShapeDtypeStruct((t, DIL_W), bf),
        jax.ShapeDtypeStruct((DIL_HEADS, t // DIL_TILE, MLA_VT_ROWS, DIL_TILE), bf),
    )
    dil_t = lambda n: pl.BlockSpec((DIL_HEADS, tm // DIL_TILE, n, DIL_TILE), lambda i: (0, i, 0, 0))
    tok_t = lambda n: pl.BlockSpec((n, tm), lambda i: (0, i))
    out_specs = (tok(256), tok_t(256), tok(ML_W), tok(ML_W), tok(LANES), tok_t(LANES),
                 headed_t(MLA_QK_PAD), headed(MLA_QK_PAD), headed_t(MLA_VT_ROWS),
                 dil_t(DIL_DH), tok(DIL_W), dil_t(MLA_VT_ROWS))
    in_specs = [tok(d), _const_spec(gmix.shape), _const_spec(w.shape), _const_spec(wqb.shape),
                _const_spec(wkvb.shape), _const_spec(gbias.shape), _const_spec(qlat.shape),
                _const_spec(kvlat.shape), _const_spec(hg.shape), pos, pos, pos, pos]
    kern = functools.partial(_inproj_kernel,
                             mla_scale=(MLA_NOPE + MLA_ROPE) ** -0.5 * math.log2(math.e),
                             dil_scale=DIL_DH ** -0.5 * math.log2(math.e))
    return pl.pallas_call(
        kern, grid=(nt,), in_specs=in_specs, out_specs=out_specs, out_shape=out_shape,
        compiler_params=pltpu.CompilerParams(dimension_semantics=("arbitrary",),
                                             vmem_limit_bytes=VMEM_LIMIT_BYTES),
        name="inproj",
    )(x2, gmix, w, wqb, wkvb, gbias, qlat, kvlat, hg, cosb, sinb, cosc, sinc)


def _mlstm_kernel(*refs, reverse, chunk, nb):
    q_ref, v_ref, gate_ref = refs[0:3]
    kt_refs = refs[3:3 + nb]
    gatet_refs = refs[3 + nb:3 + 2 * nb]
    out_ref, ct_ref, m_ref = refs[3 + 2 * nb:]
    L = chunk
    c = pl.program_id(0)

    @pl.when(c == 0)
    def _():
        ct_ref[...] = jnp.zeros_like(ct_ref)
        m_ref[...] = jnp.zeros_like(m_ref)

    row = lax.broadcasted_iota(jnp.int32, (L, L), 0)
    col = lax.broadcasted_iota(jnp.int32, (L, L), 1)
    causal = (col >= row) if reverse else (col <= row)
    vis = causal.astype(jnp.bfloat16)
    vis_t = ((row >= col) if reverse else (row <= col)).astype(jnp.bfloat16)

    def split3(x):
        hi = x.astype(jnp.bfloat16)
        rest_ = x - hi.astype(jnp.float32)
        mid = rest_.astype(jnp.bfloat16)
        return hi, mid, (rest_ - mid.astype(jnp.float32)).astype(jnp.bfloat16)

    def rep(col_):
        return jnp.broadcast_to(col_, (L, LANES))

    def wide(xb, n):
        return xb if n == LANES else jnp.concatenate([xb] * (n // LANES), axis=1)

    lane = lax.broadcasted_iota(jnp.int32, (L, LANES), 1)
    ones_col = jnp.where(lane == 0, 1.0, 0.0).astype(jnp.bfloat16)
    zero_half = jnp.zeros((ML_DK, L), jnp.bfloat16)

    d0 = 8 if reverse else 0
    gates, slab, cum_t, cum = [], [], [], []
    for bi in range(nb):
        gates.append(gate_ref[bi])
        slab.append(gatet_refs[bi][d0:d0 + 8, :])
        cum_t.append(sum(jnp.dot(vis, part, preferred_element_type=jnp.float32)
                         for part in split3(gates[bi])))
        cum.append(sum(jnp.dot(part, vis_t, preferred_element_type=jnp.float32)
                       for part in split3(slab[bi])))

    for hd in range(ML_HEADS):
        for bi in range(nb):
            st = bi * ML_HEADS + hd
            i_b = rep(gates[bi][:, d0 + hd:d0 + hd + 1])
            i_row = slab[bi][hd:hd + 1, :]
            cum_b = rep(cum_t[bi][:, d0 + 4 + hd:d0 + 5 + hd])
            cum_row = cum[bi][4 + hd:5 + hd, :]
            tot = cum_row[:, 0:1] if reverse else cum_row[:, L - 1:L]
            m_prev = m_ref[st:st + 1, 0:1]

            pair, half = hd // 2, hd % 2
            in_head = (lane >= half * ML_DK) & (lane < (half + 1) * ML_DK)
            qp = q_ref[bi, :, pair * LANES:(pair + 1) * LANES]
            qm = jnp.where(in_head, qp, jnp.zeros_like(qp))
            kt_ref = kt_refs[bi]
            kt_pair = kt_ref[pair * LANES:(pair + 1) * LANES, :]
            kt_h = kt_ref[pair * LANES + half * ML_DK:pair * LANES + (half + 1) * ML_DK, :]
            kmt = jnp.concatenate([kt_h, zero_half] if half == 0 else [zero_half, kt_h], axis=0)
            v_h = v_ref[bi, :, hd * ML_DV:(hd + 1) * ML_DV]
            v_aug = jnp.concatenate([v_h, ones_col], axis=1)

            d_mat = jnp.where(causal, wide(cum_b, L) + (i_row - cum_row), NEG_INF)
            d_inter = cum_b + m_prev
            m_t = jnp.maximum(d_inter, rep(jnp.max(d_mat, axis=1, keepdims=True)))
            w_intra = jnp.exp(d_mat - wide(m_t, L))
            w_inter = jnp.exp(d_inter - m_t)
            s_raw = jnp.dot(qm, kt_pair, preferred_element_type=jnp.float32)
            sw = (s_raw * w_intra).astype(jnp.bfloat16)
            ct = ct_ref[st]
            r = (jnp.dot(sw, v_aug, preferred_element_type=jnp.float32)
                 + wide(w_inter, 2 * LANES) * jnp.dot(qm, ct.astype(jnp.bfloat16),
                                                      preferred_element_type=jnp.float32))
            num = r[:, :ML_DV]
            den = rep(r[:, ML_DV:ML_DV + 1])
            h_dir = num / jnp.maximum(jnp.abs(den), jnp.exp(-m_t))

            d_state = tot - cum_b + i_b
            m_new = jnp.maximum(tot + m_prev, jnp.max(d_state, axis=0, keepdims=True)[:, 0:1])
            w_s = jnp.exp(d_state - m_new)
            w_c = jnp.exp(tot + m_prev - m_new)
            vw = (v_aug.astype(jnp.float32) * wide(w_s, 2 * LANES)).astype(jnp.bfloat16)
            ct_ref[st] = w_c * ct + jnp.dot(kmt, vw, preferred_element_type=jnp.float32)
            m_ref[st:st + 1, :] = jnp.broadcast_to(m_new, (1, LANES))

            out_ref[bi, :, hd * ML_DV:(hd + 1) * ML_DV] = h_dir


def _mlstm(mlq, mlkt, mlv, gates, gates_t, *, batch, seq, chunk, reverse):
    nc = seq // chunk
    blk = (lambda c: nc - 1 - c) if reverse else (lambda c: c)
    tok = lambda n: pl.BlockSpec((batch, chunk, n), lambda c: (0, blk(c), 0))
    tok_t = lambda n, bi: pl.BlockSpec((n, chunk), lambda c: (0, bi * nc + blk(c)))
    view = lambda a: a.reshape(batch, seq, a.shape[-1])
    in_specs = ([tok(256), tok(ML_W), tok(LANES)]
                + [tok_t(256, bi) for bi in range(batch)] + [tok_t(LANES, bi) for bi in range(batch)])
    args = [view(mlq), view(mlv), view(gates)] + [mlkt] * batch + [gates_t] * batch
    kern = functools.partial(_mlstm_kernel, reverse=reverse, chunk=chunk, nb=batch)
    states = batch * ML_HEADS
    out = pl.pallas_call(
        kern, grid=(nc,), in_specs=in_specs, out_specs=tok(ML_W),
        out_shape=jax.ShapeDtypeStruct((batch, seq, ML_W), jnp.float32),
        scratch_shapes=[pltpu.VMEM((states, LANES, 2 * LANES), jnp.float32),
                        pltpu.VMEM((-(-states // 8) * 8, LANES), jnp.float32)],
        compiler_params=pltpu.CompilerParams(dimension_semantics=("arbitrary",),
                                             vmem_limit_bytes=VMEM_LIMIT_BYTES),
        name="mlstm_bwd" if reverse else "mlstm_fwd",
    )(*args)
    return out.reshape(batch * seq, ML_W)


MLA_QSUB = 2


def _mla_kernel(qt_ref, k_ref, vt_ref, o_ref, s_scr, acc_scr, *, unroll):
    tq = qt_ref.shape[2]
    tk = vt_ref.shape[2]
    nk = vt_ref.shape[0]

    def attend(sub):
        qt = qt_ref[sub]

        def scores(slot, j):
            start = pl.multiple_of(j * tk, tk)
            s = jnp.dot(k_ref[pl.ds(start, tk), :], qt, preferred_element_type=jnp.float32)
            s_scr[sub, slot] = s
            return jnp.max(s, axis=0, keepdims=True)

        def consume(slot, j, m_prev, m_tile):
            m_new = jnp.maximum(m_prev, m_tile)
            alpha = jnp.exp2(m_prev - m_new)
            p = jnp.exp2(s_scr[sub, slot] - m_new).astype(jnp.bfloat16)
            acc_scr[sub] = alpha * acc_scr[sub] + jnp.dot(vt_ref[j], p,
                                                          preferred_element_type=jnp.float32)
            return m_new

        m_first = scores(0, 0)
        acc_scr[sub] = jnp.zeros(acc_scr.shape[1:], acc_scr.dtype)

        def body(jj, carry):
            m, m_tile = carry
            j = unroll * jj
            for u in range(unroll):
                m_next = m_tile
                if not (unroll == nk and u == unroll - 1):
                    m_next = scores((u + 1) % 2, jnp.minimum(j + u + 1, nk - 1))
                m = consume(u % 2, j + u, m, m_tile)
                m_tile = m_next
            return m, m_tile

        lax.fori_loop(0, nk // unroll, body, (jnp.full((1, tq), NEG_INF, jnp.float32), m_first))
        acc = acc_scr[sub]
        o_ref[sub * tq:(sub + 1) * tq, :] = (acc[:MLA_V] / acc[MLA_V:MLA_V + 1]).T.astype(o_ref.dtype)

    for sub in range(MLA_QSUB):
        attend(sub)


def _mla_attention(qt, k, vt, *, batch, seq, unroll):
    tile = qt.shape[-1]
    t = k.shape[1]
    nq = seq // tile
    nstep = nq // MLA_QSUB
    assert unroll % 2 == 0 and nq % unroll == 0 and nq % MLA_QSUB == 0
    return pl.pallas_call(
        functools.partial(_mla_kernel, unroll=unroll),
        grid=(batch, MLA_HEADS, nstep),
        in_specs=[pl.BlockSpec((None, MLA_QSUB, MLA_QK_PAD, tile), lambda b, h, i: (h, b * nstep + i, 0, 0)),
                  pl.BlockSpec((None, seq, MLA_QK_PAD), lambda b, h, i: (h, b, 0)),
                  pl.BlockSpec((None, nq, MLA_VT_ROWS, tile), lambda b, h, i: (h, b, 0, 0))],
        out_specs=pl.BlockSpec((MLA_QSUB * tile, MLA_V), lambda b, h, i: (b * nstep + i, h)),
        out_shape=jax.ShapeDtypeStruct((t, MLA_W), jnp.bfloat16),
        scratch_shapes=[pltpu.VMEM((MLA_QSUB, 2, tile, tile), jnp.float32),
                        pltpu.VMEM((MLA_QSUB, MLA_VT_ROWS, tile), jnp.float32)],
        compiler_params=pltpu.CompilerParams(
            dimension_semantics=("arbitrary", "arbitrary", "arbitrary"),
            vmem_limit_bytes=VMEM_LIMIT_BYTES),
        name="mla_attn",
    )(qt, k, vt)


DIL_TILE = 256
DIL_REACH = max(w // 2 for w, _ in DIL_PAIRS)
DIL_NOFF = DIL_REACH // DIL_TILE
DIL_NWIN = 2 * DIL_NOFF + 1
DIL_SUB = 8


def _dil_bias_table():
    r = np.arange(DIL_TILE)[:, None]
    c = np.arange(DIL_TILE)[None, :]
    tiles = []
    for o in range(-2 * DIL_NOFF, 2 * DIL_NOFF + 1):
        delta = o * DIL_TILE + r - c
        mult = np.zeros_like(delta)
        for window, dil in DIL_PAIRS:
            mult += ((delta % dil) == 0) & (np.abs(delta) <= window // 2)
        tiles.append(np.where(mult > 0, np.log2(np.maximum(mult, 1)), NEG_INF))
    return np.stack(tiles).astype(np.float32)


def _dil_kernel(qt_ref, k_ref, vt_ref, bias_ref, o_ref, s_scr, *, nq, nwin):
    i = pl.program_id(2)
    subs = range(DIL_SUB)
    qi = [i * DIL_SUB + sub for sub in subs]
    w0 = [jnp.clip(t - DIL_NOFF, 0, nq - nwin) for t in qi]

    def score_pass(sub):
        m = None
        for idx in range(nwin):
            start = pl.multiple_of((w0[sub] + idx) * DIL_TILE, DIL_TILE)
            bias = bias_ref[w0[sub] + idx - qi[sub] + 2 * DIL_NOFF]
            s = jnp.dot(k_ref[pl.ds(start, DIL_TILE), :], qt_ref[sub],
                        preferred_element_type=jnp.float32) + bias
            s_scr[sub, idx * DIL_TILE:(idx + 1) * DIL_TILE, :] = s
            m_cur = jnp.max(s, axis=0, keepdims=True)
            m = m_cur if idx == 0 else jnp.maximum(m, m_cur)
        return m

    def value_pass(sub, m):
        acc = None
        for idx in range(nwin):
            p = jnp.exp2(s_scr[sub, idx * DIL_TILE:(idx + 1) * DIL_TILE, :] - m)
            part = jnp.dot(vt_ref[w0[sub] + idx], p.astype(jnp.bfloat16),
                           preferred_element_type=jnp.float32)
            acc = part if acc is None else acc + part
        out_t = acc[:DIL_DH] / acc[DIL_DH:DIL_DH + 1]
        o_ref[sub * DIL_TILE:(sub + 1) * DIL_TILE, :] = out_t.T.astype(o_ref.dtype)

    m_prev = score_pass(0)
    for sub in range(1, DIL_SUB):
        m_next = score_pass(sub)
        value_pass(sub - 1, m_prev)
        m_prev = m_next
    value_pass(DIL_SUB - 1, m_prev)


def _dil_attention(dqt, dk, dvt, bias, *, batch, seq):
    t = dk.shape[0]
    nq = seq // DIL_TILE
    nwin = min(DIL_NWIN, nq)
    tq = DIL_SUB * DIL_TILE
    nstep = seq // tq
    return pl.pallas_call(
        functools.partial(_dil_kernel, nq=nq, nwin=nwin),
        grid=(batch, DIL_HEADS, nstep),
        in_specs=[pl.BlockSpec((None, DIL_SUB, DIL_DH, DIL_TILE), lambda b, h, i: (h, b * nstep + i, 0, 0)),
                  pl.BlockSpec((seq, DIL_DH), lambda b, h, i: (b, h)),
                  pl.BlockSpec((None, nq, MLA_VT_ROWS, DIL_TILE), lambda b, h, i: (h, b, 0, 0)),
                  _const_spec(bias.shape)],
        out_specs=pl.BlockSpec((tq, DIL_DH), lambda b, h, i: (b * nstep + i, h)),
        out_shape=jax.ShapeDtypeStruct((t, DIL_W), jnp.bfloat16),
        scratch_shapes=[pltpu.VMEM((DIL_SUB, nwin * DIL_TILE, DIL_TILE), jnp.float32)],
        compiler_params=pltpu.CompilerParams(
            dimension_semantics=("arbitrary", "arbitrary", "arbitrary"),
            vmem_limit_bytes=VMEM_LIMIT_BYTES),
        name="dil_attn",
    )(dqt, dk, dvt, bias)


def _outproj_kernel(x_ref, hf_ref, hb_ref, mlo_ref, onorm_ref, yb_ref, yc_ref, w_ref, o_ref):
    h_sum = hf_ref[...] + hb_ref[...]
    gate = mlo_ref[...].astype(jnp.float32)
    ya = jnp.concatenate(
        [gate[:, hd * ML_DV:(hd + 1) * ML_DV]
         * _rms(h_sum[:, hd * ML_DV:(hd + 1) * ML_DV], onorm_ref[hd:hd + 1, :], ML_DV)
         for hd in range(ML_HEADS)], axis=1).astype(jnp.bfloat16)
    b0, b1 = ML_W, ML_W + MLA_W
    y = (jnp.dot(ya, w_ref[0:b0, :], preferred_element_type=jnp.float32)
         + jnp.dot(yb_ref[...], w_ref[b0:b1, :], preferred_element_type=jnp.float32)
         + jnp.dot(yc_ref[...], w_ref[b1:, :], preferred_element_type=jnp.float32))
    o_ref[...] = x_ref[...] + y


def _outproj(x2, hf, hb, mlo, onorm, yb, yc, w, *, tm):
    t, d = x2.shape
    tok = lambda n: pl.BlockSpec((tm, n), lambda i: (i, 0))
    return pl.pallas_call(
        _outproj_kernel, grid=(t // tm,),
        in_specs=[tok(d), tok(ML_W), tok(ML_W), tok(ML_W), _const_spec(onorm.shape),
                  tok(MLA_W), tok(DIL_W), _const_spec(w.shape)],
        out_specs=tok(d), out_shape=jax.ShapeDtypeStruct((t, d), jnp.float32),
        compiler_params=pltpu.CompilerParams(dimension_semantics=("arbitrary",),
                                             vmem_limit_bytes=VMEM_LIMIT_BYTES),
        name="outproj",
    )(x2, hf, hb, mlo, onorm, yb, yc, w)


def _ffn_kernel(x_ref, g_ref, w1_ref, w2_ref, o_ref, h_ref):
    j = pl.program_id(1)

    @pl.when(j == 0)
    def _():
        x = x_ref[...]
        h_ref[...] = _rms(x, g_ref[...], x.shape[-1]).astype(h_ref.dtype)
        o_ref[...] = x

    u = jnp.maximum(jnp.dot(h_ref[...], w1_ref[...], preferred_element_type=jnp.float32), 0.0)
    u = (u * u).astype(jnp.bfloat16)
    o_ref[...] += jnp.dot(u, w2_ref[...], preferred_element_type=jnp.float32)


def _ffn(x2, g, w1, w2, *, tm, tf):
    t, d = x2.shape
    dff = w1.shape[1]
    return pl.pallas_call(
        _ffn_kernel, grid=(t // tm, dff // tf),
        in_specs=[pl.BlockSpec((tm, d), lambda i, j: (i, 0)),
                  pl.BlockSpec(g.shape, lambda i, j: (0, 0)),
                  pl.BlockSpec((d, tf), lambda i, j: (0, j)),
                  pl.BlockSpec((tf, d), lambda i, j: (j, 0))],
        out_specs=pl.BlockSpec((tm, d), lambda i, j: (i, 0)),
        out_shape=jax.ShapeDtypeStruct((t, d), jnp.float32),
        scratch_shapes=[pltpu.VMEM((tm, d), jnp.bfloat16)],
        compiler_params=pltpu.CompilerParams(dimension_semantics=("arbitrary", "arbitrary"),
                                             vmem_limit_bytes=VMEM_LIMIT_BYTES),
        name="ffn",
    )(x2, g, w1, w2)


def _rope_tables(seq, rot_dim, fill_cos):
    pos = jnp.arange(seq, dtype=jnp.float32)
    inv_freq = ROPE_THETA ** (-jnp.arange(0, rot_dim, 2, dtype=jnp.float32) / rot_dim)
    ang = pos[:, None] * inv_freq[None, :]
    cos, sin = jnp.cos(ang), jnp.sin(ang)
    pad = LANES - rot_dim
    cos_f = jnp.concatenate([cos, cos, jnp.full((seq, pad), fill_cos, jnp.float32)], axis=1)
    sin_f = jnp.concatenate([-sin, sin, jnp.zeros((seq, pad), jnp.float32)], axis=1)
    return cos_f, sin_f


def _pad_cols(a, n):
    return jnp.pad(a, ((0, 0), (0, n - a.shape[1])))


def _pack_w_in(w_in):
    splits = np.cumsum((256, 256, ML_W, ML_W, 4 * ML_HEADS, MLA_Q_RANK, MLA_KV_RANK + MLA_ROPE,
                        DIL_W, DIL_W, DIL_W))[:-1].tolist()
    (wq, wk, wv, wo, wg, wcq, wckv, wdq, wdk, wdv) = jnp.split(w_in, splits, axis=1)
    packed = jnp.concatenate([wq, wk, wv, wo, _pad_cols(wg, LANES), wcq, _pad_cols(wckv, 256),
                              wdq, wdk, wdv], axis=1)
    return packed.astype(jnp.bfloat16)


def _cast_kernel(x_ref, o_ref):
    o_ref[...] = x_ref[...].astype(o_ref.dtype)


def _to_bf16(stacked, layer, block_bytes=8 * 1024 * 1024):
    _, rows, cols = stacked.shape
    br = min(rows, max(8, block_bytes // (4 * cols)))
    assert rows % br == 0
    return pl.pallas_call(
        _cast_kernel, grid=(rows // br,),
        in_specs=[pl.BlockSpec((None, br, cols), lambda i: (layer, i, 0))],
        out_specs=pl.BlockSpec((br, cols), lambda i: (i, 0)),
        out_shape=jax.ShapeDtypeStruct((rows, cols), jnp.bfloat16),
        compiler_params=pltpu.CompilerParams(dimension_semantics=("arbitrary",),
                                             vmem_limit_bytes=VMEM_LIMIT_BYTES),
        name="to_bf16",
    )(stacked)


def _layer(x2, p, stacked, layer, tables, *, batch, seq):
    cosb, sinb, cosc, sinc, dil_bias = tables
    w = _pack_w_in(p['w_in'])
    wqb = p['mla_w_q_b'].reshape(MLA_Q_RANK, MLA_HEADS, MLA_NOPE + MLA_ROPE)
    wqb = jnp.pad(wqb, ((0, 0), (0, 0), (0, MLA_QK_PAD - MLA_NOPE - MLA_ROPE)))
    wqb = wqb.reshape(MLA_Q_RANK, MLA_HEADS * MLA_QK_PAD).astype(jnp.bfloat16)
    wkvb = p['mla_w_kv_b'].astype(jnp.bfloat16)
    gbias = jnp.concatenate([p['ml_i_bias'][0], p['ml_f_bias'][0], p['ml_i_bias'][1], p['ml_f_bias'][1]])
    gbias = _pad_cols(gbias[None, :], LANES)
    qh, kh = p['mla_q_head_norm'], p['mla_k_head_norm']
    hg = jnp.stack([qh[:MLA_NOPE], jnp.pad(qh[MLA_NOPE:], (0, LANES - MLA_ROPE)),
                    kh[:MLA_NOPE], jnp.pad(kh[MLA_NOPE:], (0, LANES - MLA_ROPE)),
                    p['dil_q_norm'], p['dil_k_norm'],
                    jnp.zeros((LANES,), jnp.float32), jnp.zeros((LANES,), jnp.float32)])

    (mlq, mlkt, mlv, mlo, gates, gates_t, qt, k, vt, dqt, dk, dvt) = _inproj(
        x2, p['norm_mix'][None, :], w, wqb, wkvb, gbias, p['mla_q_norm'][None, :],
        p['mla_kv_norm'][None, :], hg, cosb, sinb, cosc, sinc, seq=seq, tm=MLA_TILE)

    h_bwd = _mlstm(mlq, mlkt, mlv, gates, gates_t, batch=batch, seq=seq, chunk=ML_CHUNK, reverse=True)
    h_fwd = _mlstm(mlq, mlkt, mlv, gates, gates_t, batch=batch, seq=seq, chunk=ML_CHUNK, reverse=False)
    yb = _mla_attention(qt, k, vt, batch=batch, seq=seq, unroll=min(MLA_UNROLL, seq // MLA_TILE))
    yc = _dil_attention(dqt, dk, dvt, dil_bias, batch=batch, seq=seq)

    x2 = _outproj(x2, h_fwd, h_bwd, mlo, p['ml_out_norm'], yb, yc, _to_bf16(stacked['w_out'], layer),
                  tm=ROW_TILE)
    return _ffn(x2, p['norm_ff'][None, :], _to_bf16(stacked['w_ff1'], layer),
                _to_bf16(stacked['w_ff2'], layer), tm=ROW_TILE, tf=FFN_CHUNK)


def kernel(x, norm_mix, w_in, ml_i_bias, ml_f_bias, ml_out_norm, mla_q_norm, mla_w_q_b, mla_kv_norm,
           mla_w_kv_b, mla_q_head_norm, mla_k_head_norm, dil_q_norm, dil_k_norm, w_out, norm_ff,
           w_ff1, w_ff2):
    batch, seq, d = x.shape
    for tile in (DIL_SUB * DIL_TILE, MLA_QSUB * MLA_TILE, ML_CHUNK, ROW_TILE):
        assert seq % tile == 0, (seq, tile)
    params = dict(norm_mix=norm_mix, w_in=w_in, ml_i_bias=ml_i_bias, ml_f_bias=ml_f_bias,
                  ml_out_norm=ml_out_norm, mla_q_norm=mla_q_norm, mla_w_q_b=mla_w_q_b,
                  mla_kv_norm=mla_kv_norm, mla_w_kv_b=mla_w_kv_b, mla_q_head_norm=mla_q_head_norm,
                  mla_k_head_norm=mla_k_head_norm, dil_q_norm=dil_q_norm, dil_k_norm=dil_k_norm,
                  norm_ff=norm_ff)
    stacked = dict(w_out=w_out, w_ff1=w_ff1, w_ff2=w_ff2)
    cosb, sinb = _rope_tables(seq, MLA_ROPE, 0.0)
    cosc, sinc = _rope_tables(seq, DIL_ROT, 1.0)
    tables = (cosb, sinb, cosc, sinc, jnp.asarray(_dil_bias_table()))
    x2 = x.reshape(batch * seq, d)
    for layer in range(norm_mix.shape[0]):
        x2 = _layer(x2, {name: val[layer] for name, val in params.items()}, stacked, layer, tables,
                    batch=batch, seq=seq)
    return x2.reshape(batch, seq, d)
```

```python
import functools
import math

import numpy as np
import jax
import jax.numpy as jnp
from jax import lax
from jax.experimental import pallas as pl
from jax.experimental.pallas import tpu as pltpu

EPS = 1e-6
NEG_INF = -1e30
ROPE_THETA = 500000.0

ML_HEADS = 4
ML_DK = 64
ML_DV = 128
ML_W = ML_HEADS * ML_DV

MLA_HEADS = 6
MLA_Q_RANK = 384
MLA_KV_RANK = 128
MLA_NOPE = 128
MLA_ROPE = 64
MLA_V = 128
MLA_W = MLA_HEADS * MLA_V
MLA_QK_PAD = 256
MLA_VT_ROWS = 144
MLA_TILE = 512

DIL_HEADS = 6
DIL_DH = 128
DIL_ROT = DIL_DH // 4
DIL_PAIRS = ((128, 1), (512, 4), (2048, 16))
DIL_W = DIL_HEADS * DIL_DH

LANES = 128
VMEM_LIMIT_BYTES = 60 * 1024 * 1024

ML_CHUNK = 256
MLA_UNROLL = 16
ROW_TILE = 512
FFN_CHUNK = 2048

_C_MLQ = 0
_C_MLK = 256
_C_MLV = 512
_C_MLO = 1024
_C_GATE = 1536
_C_CQ = 1664
_C_CKV = 2048
_C_DQ = 2304
_C_DK = 3072
_C_DV = 3840
_C_END = 4608


def _rms(x, gain, n):
    ms = jnp.sum(x * x, axis=-1, keepdims=True) * (1.0 / n)
    return x * lax.rsqrt(ms + EPS) * gain


def _rope_tile(x, cos_f, sin_f, half):
    lane = lax.broadcasted_iota(jnp.int32, x.shape, 1)
    sw = jnp.where(lane < half, pltpu.roll(x, LANES - half, 1), pltpu.roll(x, half, 1))
    return x * cos_f + sw * sin_f


def _log_sigmoid(x):
    return jnp.minimum(x, 0.0) - jnp.log1p(jnp.exp(-jnp.abs(x)))


def _inproj_kernel(x_ref, gmix_ref, w_ref, wqb_ref, wkvb_ref, gbias_ref, qlat_ref, kvlat_ref,
                   hg_ref, cosb_ref, sinb_ref, cosc_ref, sinc_ref,
                   mlq_ref, mlkt_ref, mlv_ref, mlo_ref, gate_ref, gatet_ref, qt_ref, k_ref, vt_ref,
                   dqt_ref, dk_ref, dvt_ref, *, mla_scale, dil_scale):
    x = x_ref[...]
    h = _rms(x, gmix_ref[...], x.shape[-1]).astype(jnp.bfloat16)

    def proj(c0, c1):
        return jnp.dot(h, w_ref[:, c0:c1], preferred_element_type=jnp.float32)

    cos_b = cosb_ref[...]
    sin_b = sinb_ref[...]
    cos_c = cosc_ref[...]
    sin_c = sinc_ref[...]
    qg_n = hg_ref[0:1, :]
    qg_r = hg_ref[1:2, :]
    kg_n = hg_ref[2:3, :]
    kg_r = hg_ref[3:4, :]
    dqg = hg_ref[4:5, :]
    dkg = hg_ref[5:6, :]

    cq = _rms(proj(_C_CQ, _C_CKV), qlat_ref[...], MLA_Q_RANK).astype(jnp.bfloat16)
    zc = proj(_C_CKV, _C_DQ)
    ckv = _rms(zc[:, :MLA_KV_RANK], kvlat_ref[...], MLA_KV_RANK).astype(jnp.bfloat16)
    zdq = proj(_C_DQ, _C_DK)
    zdk = proj(_C_DK, _C_DV)
    zq = jnp.dot(cq, wqb_ref[...], preferred_element_type=jnp.float32)
    zkv = jnp.dot(ckv, wkvb_ref[...], preferred_element_type=jnp.float32)

    k_rope = _rope_tile(_rms(zc[:, MLA_KV_RANK:], kg_r, MLA_ROPE), cos_b, sin_b, MLA_ROPE // 2)
    k_rope = k_rope.astype(jnp.bfloat16)
    pad_rows = MLA_VT_ROWS - MLA_V

    def ones_row_tile(n):
        first = lax.broadcasted_iota(jnp.int32, (pad_rows, n), 0) == 0
        return jnp.where(first, 1.0, 0.0).astype(jnp.bfloat16)

    ones_row = ones_row_tile(x.shape[0])
    ones_row_dil = ones_row_tile(DIL_TILE)
    for hd in range(MLA_HEADS):
        c = hd * MLA_QK_PAD
        q_nope = _rms(zq[:, c:c + MLA_NOPE], qg_n, MLA_NOPE)
        q_rope = _rope_tile(_rms(zq[:, c + MLA_NOPE:c + MLA_QK_PAD], qg_r, MLA_ROPE),
                            cos_b, sin_b, MLA_ROPE // 2)
        qt_ref[hd, 0, 0:MLA_NOPE, :] = (q_nope * mla_scale).T.astype(jnp.bfloat16)
        qt_ref[hd, 0, MLA_NOPE:MLA_QK_PAD, :] = (q_rope * mla_scale).T.astype(jnp.bfloat16)
        k_nope = _rms(zkv[:, c:c + MLA_NOPE], kg_n, MLA_NOPE)
        k_ref[hd, :, 0:MLA_NOPE] = k_nope.astype(jnp.bfloat16)
        k_ref[hd, :, MLA_NOPE:MLA_QK_PAD] = k_rope
        vt_ref[hd, 0, 0:MLA_V, :] = zkv[:, c + MLA_NOPE:c + MLA_QK_PAD].T.astype(jnp.bfloat16)
        vt_ref[hd, 0, MLA_V:MLA_VT_ROWS, :] = ones_row

    mlkt_ref[...] = proj(_C_MLK, _C_MLV).T.astype(jnp.bfloat16)
    mlo_ref[...] = jax.nn.sigmoid(proj(_C_MLO, _C_GATE)).astype(jnp.bfloat16)
    g = proj(_C_GATE, _C_CQ) + gbias_ref[...]
    lane = lax.broadcasted_iota(jnp.int32, g.shape, 1)
    is_forget = (lane % 8) >= 4
    g = jnp.where(is_forget, _log_sigmoid(g), g)
    gate_ref[...] = g
    gatet_ref[...] = g.T

    for hd in range(DIL_HEADS):
        c = hd * DIL_DH
        qh = _rope_tile(_rms(zdq[:, c:c + DIL_DH], dqg, DIL_DH), cos_c, sin_c, DIL_ROT // 2)
        kh = _rope_tile(_rms(zdk[:, c:c + DIL_DH], dkg, DIL_DH), cos_c, sin_c, DIL_ROT // 2)
        qh = qh * dil_scale
        for sb in range(x.shape[0] // DIL_TILE):
            dqt_ref[hd, sb] = qh[sb * DIL_TILE:(sb + 1) * DIL_TILE].T.astype(jnp.bfloat16)
        dk_ref[:, c:c + DIL_DH] = kh.astype(jnp.bfloat16)
    zdv = proj(_C_DV, _C_END)
    for hd in range(DIL_HEADS):
        c = hd * DIL_DH
        for sb in range(x.shape[0] // DIL_TILE):
            rows = slice(sb * DIL_TILE, (sb + 1) * DIL_TILE)
            dvt_ref[hd, sb, 0:DIL_DH, :] = zdv[rows, c:c + DIL_DH].T.astype(jnp.bfloat16)
            dvt_ref[hd, sb, DIL_DH:MLA_VT_ROWS, :] = ones_row_dil

    mlq_ref[...] = (proj(_C_MLQ, _C_MLK) * (ML_DK ** -0.5)).astype(jnp.bfloat16)
    mlv_ref[...] = proj(_C_MLV, _C_MLO).astype(jnp.bfloat16)


def _const_spec(shape):
    n = len(shape)
    return pl.BlockSpec(shape, lambda *_: (0,) * n, pipeline_mode=pl.Buffered(1))


def _inproj(x2, gmix, w, wqb, wkvb, gbias, qlat, kvlat, hg, cosb, sinb, cosc, sinc, *, seq, tm):
    t, d = x2.shape
    nt = t // tm
    ns = seq // tm
    tok = lambda n: pl.BlockSpec((tm, n), lambda i: (i, 0))
    pos = pl.BlockSpec((tm, LANES), lambda i: (i % ns, 0))
    headed = lambda n: pl.BlockSpec((MLA_HEADS, tm, n), lambda i: (0, i, 0))
    headed_t = lambda n: pl.BlockSpec((MLA_HEADS, 1, n, tm), lambda i: (0, i, 0, 0))
    bf = jnp.bfloat16
    out_shape = (
        jax.ShapeDtypeStruct((t, 256), bf), jax.ShapeDtypeStruct((256, t), bf),
        jax.ShapeDtypeStruct((t, ML_W), bf), jax.ShapeDtypeStruct((t, ML_W), bf),
        jax.ShapeDtypeStruct((t, LANES), jnp.float32), jax.ShapeDtypeStruct((LANES, t), jnp.float32),
        jax.ShapeDtypeStruct((MLA_HEADS, nt, MLA_QK_PAD, tm), bf),
        jax.ShapeDtypeStruct((MLA_HEADS, t, MLA_QK_PAD), bf),
        jax.ShapeDtypeStruct((MLA_HEADS, nt, MLA_VT_ROWS, tm), bf),
        jax.ShapeDtypeStruct((DIL_HEADS, t // DIL_TILE, DIL_DH, DIL_TILE), bf),
        jax.ShapeDtypeStruct((t, DIL_W), bf),
        jax.ShapeDtypeStruct((DIL_HEADS, t // DIL_TILE, MLA_VT_ROWS, DIL_TILE), bf),
    )
    dil_t = lambda n: pl.BlockSpec((DIL_HEADS, tm // DIL_TILE, n, DIL_TILE), lambda i: (0, i, 0, 0))
    tok_t = lambda n: pl.BlockSpec((n, tm), lambda i: (0, i))
    out_specs = (tok(256), tok_t(256), tok(ML_W), tok(ML_W), tok(LANES), tok_t(LANES),
                 headed_t(MLA_QK_PAD), headed(MLA_QK_PAD), headed_t(MLA_VT_ROWS),
                 dil_t(DIL_DH), tok(DIL_W), dil_t(MLA_VT_ROWS))
    in_specs = [tok(d), _const_spec(gmix.shape), _const_spec(w.shape), _const_spec(wqb.shape),
                _const_spec(wkvb.shape), _const_spec(gbias.shape), _const_spec(qlat.shape),
                _const_spec(kvlat.shape), _const_spec(hg.shape), pos, pos, pos, pos]
    kern = functools.partial(_inproj_kernel,
                             mla_scale=(MLA_NOPE + MLA_ROPE) ** -0.5 * math.log2(math.e),
                             dil_scale=DIL_DH ** -0.5 * math.log2(math.e))
    return pl.pallas_call(
        kern, grid=(nt,), in_specs=in_specs, out_specs=out_specs, out_shape=out_shape,
        compiler_params=pltpu.CompilerParams(dimension_semantics=("arbitrary",),
                                             vmem_limit_bytes=VMEM_LIMIT_BYTES),
        name="inproj",
    )(x2, gmix, w, wqb, wkvb, gbias, qlat, kvlat, hg, cosb, sinb, cosc, sinc)


def _mlstm_kernel(*refs, chunk, nb):
    per = 3 + 2 * nb
    out_f, out_b, ct_ref, m_ref = refs[2 * per:]

    @pl.when(pl.program_id(0) == 0)
    def _():
        ct_ref[...] = jnp.zeros_like(ct_ref)
        m_ref[...] = jnp.zeros_like(m_ref)

    _mlstm_direction(refs[0:per], out_f, ct_ref, m_ref, reverse=False, slot0=0, chunk=chunk, nb=nb)
    _mlstm_direction(refs[per:2 * per], out_b, ct_ref, m_ref, reverse=True, slot0=nb * ML_HEADS,
                     chunk=chunk, nb=nb)


def _mlstm_direction(refs, out_ref, ct_ref, m_ref, *, reverse, slot0, chunk, nb):
    q_ref, v_ref, gate_ref = refs[0:3]
    kt_refs = refs[3:3 + nb]
    gatet_refs = refs[3 + nb:3 + 2 * nb]
    L = chunk

    row = lax.broadcasted_iota(jnp.int32, (L, L), 0)
    col = lax.broadcasted_iota(jnp.int32, (L, L), 1)
    causal = (col >= row) if reverse else (col <= row)
    vis = causal.astype(jnp.bfloat16)
    vis_t = ((row >= col) if reverse else (row <= col)).astype(jnp.bfloat16)

    def split3(x):
        hi = x.astype(jnp.bfloat16)
        rest_ = x - hi.astype(jnp.float32)
        mid = rest_.astype(jnp.bfloat16)
        return hi, mid, (rest_ - mid.astype(jnp.float32)).astype(jnp.bfloat16)

    def rep(col_):
        return jnp.broadcast_to(col_, (L, LANES))

    def wide(xb, n):
        return xb if n == LANES else jnp.concatenate([xb] * (n // LANES), axis=1)

    lane = lax.broadcasted_iota(jnp.int32, (L, LANES), 1)
    ones_col = jnp.where(lane == 0, 1.0, 0.0).astype(jnp.bfloat16)
    zero_half = jnp.zeros((ML_DK, L), jnp.bfloat16)

    d0 = 8 if reverse else 0
    gates, slab, cum_t, cum = [], [], [], []
    for bi in range(nb):
        gates.append(gate_ref[bi])
        slab.append(gatet_refs[bi][d0:d0 + 8, :])
        cum_t.append(sum(jnp.dot(vis, part, preferred_element_type=jnp.float32)
                         for part in split3(gates[bi])))
        cum.append(sum(jnp.dot(part, vis_t, preferred_element_type=jnp.float32)
                       for part in split3(slab[bi])))

    for hd in range(ML_HEADS):
        for bi in range(nb):
            st = slot0 + bi * ML_HEADS + hd
            i_b = rep(gates[bi][:, d0 + hd:d0 + hd + 1])
            i_row = slab[bi][hd:hd + 1, :]
            cum_b = rep(cum_t[bi][:, d0 + 4 + hd:d0 + 5 + hd])
            cum_row = cum[bi][4 + hd:5 + hd, :]
            tot = cum_row[:, 0:1] if reverse else cum_row[:, L - 1:L]
            m_prev = m_ref[st:st + 1, 0:1]

            pair, half = hd // 2, hd % 2
            in_head = (lane >= half * ML_DK) & (lane < (half + 1) * ML_DK)
            qp = q_ref[bi, :, pair * LANES:(pair + 1) * LANES]
            qm = jnp.where(in_head, qp, jnp.zeros_like(qp))
            kt_ref = kt_refs[bi]
            kt_pair = kt_ref[pair * LANES:(pair + 1) * LANES, :]
            kt_h = kt_ref[pair * LANES + half * ML_DK:pair * LANES + (half + 1) * ML_DK, :]
            kmt = jnp.concatenate([kt_h, zero_half] if half == 0 else [zero_half, kt_h], axis=0)
            v_h = v_ref[bi, :, hd * ML_DV:(hd + 1) * ML_DV]
            v_aug = jnp.concatenate([v_h, ones_col], axis=1)

            d_mat = jnp.where(causal, wide(cum_b, L) + (i_row - cum_row), NEG_INF)
            d_inter = cum_b + m_prev
            m_t = jnp.maximum(d_inter, rep(jnp.max(d_mat, axis=1, keepdims=True)))
            w_intra = jnp.exp(d_mat - wide(m_t, L))
            w_inter = jnp.exp(d_inter - m_t)
            s_raw = jnp.dot(qm, kt_pair, preferred_element_type=jnp.float32)
            sw = (s_raw * w_intra).astype(jnp.bfloat16)
            ct = ct_ref[st]
            r = (jnp.dot(sw, v_aug, preferred_element_type=jnp.float32)
                 + wide(w_inter, 2 * LANES) * jnp.dot(qm, ct.astype(jnp.bfloat16),
                                                      preferred_element_type=jnp.float32))
            num = r[:, :ML_DV]
            den = rep(r[:, ML_DV:ML_DV + 1])
            h_dir = num / jnp.maximum(jnp.abs(den), jnp.exp(-m_t))

            d_state = tot - cum_b + i_b
            m_new = jnp.maximum(tot + m_prev, jnp.max(d_state, axis=0, keepdims=True)[:, 0:1])
            w_s = jnp.exp(d_state - m_new)
            w_c = jnp.exp(tot + m_prev - m_new)
            vw = (v_aug.astype(jnp.float32) * wide(w_s, 2 * LANES)).astype(jnp.bfloat16)
            ct_ref[st] = w_c * ct + jnp.dot(kmt, vw, preferred_element_type=jnp.float32)
            m_ref[st:st + 1, :] = jnp.broadcast_to(m_new, (1, LANES))

            out_ref[bi, :, hd * ML_DV:(hd + 1) * ML_DV] = h_dir


def _mlstm(mlq, mlkt, mlv, gates, gates_t, *, batch, seq, chunk):
    nc = seq // chunk
    view = lambda a: a.reshape(batch, seq, a.shape[-1])
    in_specs, args, out_specs = [], [], []
    for blk in (lambda c: c, lambda c: nc - 1 - c):
        tok = lambda n, blk=blk: pl.BlockSpec((batch, chunk, n), lambda c: (0, blk(c), 0))
        tok_t = lambda n, bi, blk=blk: pl.BlockSpec((n, chunk), lambda c: (0, bi * nc + blk(c)))
        in_specs += ([tok(256), tok(ML_W), tok(LANES)]
                     + [tok_t(256, bi) for bi in range(batch)]
                     + [tok_t(LANES, bi) for bi in range(batch)])
        args += [view(mlq), view(mlv), view(gates)] + [mlkt] * batch + [gates_t] * batch
        out_specs.append(tok(ML_W))
    states = 2 * batch * ML_HEADS
    out_shape = jax.ShapeDtypeStruct((batch, seq, ML_W), jnp.float32)
    h_fwd, h_bwd = pl.pallas_call(
        functools.partial(_mlstm_kernel, chunk=chunk, nb=batch),
        grid=(nc,), in_specs=in_specs, out_specs=tuple(out_specs), out_shape=(out_shape, out_shape),
        scratch_shapes=[pltpu.VMEM((states, LANES, 2 * LANES), jnp.float32),
                        pltpu.VMEM((-(-states // 8) * 8, LANES), jnp.float32)],
        compiler_params=pltpu.CompilerParams(dimension_semantics=("arbitrary",),
                                             vmem_limit_bytes=VMEM_LIMIT_BYTES),
        name="mlstm",
    )(*args)
    return h_fwd.reshape(batch * seq, ML_W), h_bwd.reshape(batch * seq, ML_W)


MLA_QSUB = 4


def _mla_kernel(qt_ref, k_ref, vt_ref, o_ref, s_scr, acc_scr, *, unroll):
    tq = qt_ref.shape[2]
    tk = vt_ref.shape[2]
    nk = vt_ref.shape[0]

    def attend(sub):
        qt = qt_ref[sub]

        def scores(slot, j):
            start = pl.multiple_of(j * tk, tk)
            s = jnp.dot(k_ref[pl.ds(start, tk), :], qt, preferred_element_type=jnp.float32)
            s_scr[sub, slot] = s
            return jnp.max(s, axis=0, keepdims=True)

        def consume(slot, j, m_prev, m_tile):
            m_new = jnp.maximum(m_prev, m_tile)
            alpha = jnp.exp2(m_prev - m_new)
            p = jnp.exp2(s_scr[sub, slot] - m_new).astype(jnp.bfloat16)
            acc_scr[sub] = alpha * acc_scr[sub] + jnp.dot(vt_ref[j], p,
                                                          preferred_element_type=jnp.float32)
            return m_new

        m_first = scores(0, 0)
        acc_scr[sub] = jnp.zeros(acc_scr.shape[1:], acc_scr.dtype)

        def body(jj, carry):
            m, m_tile = carry
            j = unroll * jj
            for u in range(unroll):
                m_next = m_tile
                if not (unroll == nk and u == unroll - 1):
                    m_next = scores((u + 1) % 2, jnp.minimum(j + u + 1, nk - 1))
                m = consume(u % 2, j + u, m, m_tile)
                m_tile = m_next
            return m, m_tile

        lax.fori_loop(0, nk // unroll, body, (jnp.full((1, tq), NEG_INF, jnp.float32), m_first))
        acc = acc_scr[sub]
        o_ref[sub * tq:(sub + 1) * tq, :] = (acc[:MLA_V] / acc[MLA_V:MLA_V + 1]).T.astype(o_ref.dtype)

    for sub in range(MLA_QSUB):
        attend(sub)


def _mla_attention(qt, k, vt, *, batch, seq, unroll):
    tile = qt.shape[-1]
    t = k.shape[1]
    nq = seq // tile
    nstep = nq // MLA_QSUB
    assert unroll % 2 == 0 and nq % unroll == 0 and nq % MLA_QSUB == 0
    return pl.pallas_call(
        functools.partial(_mla_kernel, unroll=unroll),
        grid=(batch, MLA_HEADS, nstep),
        in_specs=[pl.BlockSpec((None, MLA_QSUB, MLA_QK_PAD, tile), lambda b, h, i: (h, b * nstep + i, 0, 0)),
                  pl.BlockSpec((None, seq, MLA_QK_PAD), lambda b, h, i: (h, b, 0)),
                  pl.BlockSpec((None, nq, MLA_VT_ROWS, tile), lambda b, h, i: (h, b, 0, 0))],
        out_specs=pl.BlockSpec((MLA_QSUB * tile, MLA_V), lambda b, h, i: (b * nstep + i, h)),
        out_shape=jax.ShapeDtypeStruct((t, MLA_W), jnp.bfloat16),
        scratch_shapes=[pltpu.VMEM((MLA_QSUB, 2, tile, tile), jnp.float32),
                        pltpu.VMEM((MLA_QSUB, MLA_VT_ROWS, tile), jnp.float32)],
        compiler_params=pltpu.CompilerParams(
            dimension_semantics=("arbitrary", "arbitrary", "arbitrary"),
            vmem_limit_bytes=VMEM_LIMIT_BYTES),
        name="mla_attn",
    )(qt, k, vt)


DIL_TILE = 256
DIL_REACH = max(w // 2 for w, _ in DIL_PAIRS)
DIL_NOFF = DIL_REACH // DIL_TILE
DIL_NWIN = 2 * DIL_NOFF + 1
DIL_SUB = 8
DIL_LEAD = 2


def _dil_bias_table():
    r = np.arange(DIL_TILE)[:, None]
    c = np.arange(DIL_TILE)[None, :]
    tiles = []
    for o in range(-2 * DIL_NOFF, 2 * DIL_NOFF + 1):
        delta = o * DIL_TILE + r - c
        mult = np.zeros_like(delta)
        for window, dil in DIL_PAIRS:
            mult += ((delta % dil) == 0) & (np.abs(delta) <= window // 2)
        tiles.append(np.where(mult > 0, np.log2(np.maximum(mult, 1)), NEG_INF))
    return np.stack(tiles).astype(np.float32)


def _dil_kernel(qt_ref, k_ref, vt_ref, bias_ref, o_ref, s_scr, *, nq, nwin):
    i = pl.program_id(2)
    subs = range(DIL_SUB)
    qi = [i * DIL_SUB + sub for sub in subs]
    w0 = [jnp.clip(t - DIL_NOFF, 0, nq - nwin) for t in qi]

    def score_pass(sub):
        m = None
        for idx in range(nwin):
            start = pl.multiple_of((w0[sub] + idx) * DIL_TILE, DIL_TILE)
            bias = bias_ref[w0[sub] + idx - qi[sub] + 2 * DIL_NOFF]
            s = jnp.dot(k_ref[pl.ds(start, DIL_TILE), :], qt_ref[sub],
                        preferred_element_type=jnp.float32) + bias
            s_scr[sub, idx * DIL_TILE:(idx + 1) * DIL_TILE, :] = s
            m_cur = jnp.max(s, axis=0, keepdims=True)
            m = m_cur if idx == 0 else jnp.maximum(m, m_cur)
        return m

    def value_pass(sub, m):
        acc = None
        for idx in range(nwin):
            p = jnp.exp2(s_scr[sub, idx * DIL_TILE:(idx + 1) * DIL_TILE, :] - m)
            part = jnp.dot(vt_ref[w0[sub] + idx], p.astype(jnp.bfloat16),
                           preferred_element_type=jnp.float32)
            acc = part if acc is None else acc + part
        out_t = acc[:DIL_DH] / acc[DIL_DH:DIL_DH + 1]
        o_ref[sub * DIL_TILE:(sub + 1) * DIL_TILE, :] = out_t.T.astype(o_ref.dtype)

    ms = [score_pass(sub) for sub in range(min(DIL_LEAD, DIL_SUB))]
    for sub in range(DIL_SUB):
        if sub + DIL_LEAD < DIL_SUB:
            ms.append(score_pass(sub + DIL_LEAD))
        value_pass(sub, ms[sub])


def _dil_attention(dqt, dk, dvt, bias, *, batch, seq):
    t = dk.shape[0]
    nq = seq // DIL_TILE
    nwin = min(DIL_NWIN, nq)
    tq = DIL_SUB * DIL_TILE
    nstep = seq // tq
    return pl.pallas_call(
        functools.partial(_dil_kernel, nq=nq, nwin=nwin),
        grid=(batch, DIL_HEADS, nstep),
        in_specs=[pl.BlockSpec((None, DIL_SUB, DIL_DH, DIL_TILE), lambda b, h, i: (h, b * nstep + i, 0, 0)),
                  pl.BlockSpec((seq, DIL_DH), lambda b, h, i: (b, h)),
                  pl.BlockSpec((None, nq, MLA_VT_ROWS, DIL_TILE), lambda b, h, i: (h, b, 0, 0)),
                  _const_spec(bias.shape)],
        out_specs=pl.BlockSpec((tq, DIL_DH), lambda b, h, i: (b * nstep + i, h)),
        out_shape=jax.ShapeDtypeStruct((t, DIL_W), jnp.bfloat16),
        scratch_shapes=[pltpu.VMEM((DIL_SUB, nwin * DIL_TILE, DIL_TILE), jnp.float32)],
        compiler_params=pltpu.CompilerParams(
            dimension_semantics=("arbitrary", "arbitrary", "arbitrary"),
            vmem_limit_bytes=VMEM_LIMIT_BYTES),
        name="dil_attn",
    )(dqt, dk, dvt, bias)


def _outproj_kernel(x_ref, hf_ref, hb_ref, mlo_ref, onorm_ref, yb_ref, yc_ref, w_ref, gff_ref,
                    o_ref, h_ref):
    h_sum = hf_ref[...] + hb_ref[...]
    gate = mlo_ref[...].astype(jnp.float32)
    ya = jnp.concatenate(
        [gate[:, hd * ML_DV:(hd + 1) * ML_DV]
         * _rms(h_sum[:, hd * ML_DV:(hd + 1) * ML_DV], onorm_ref[hd:hd + 1, :], ML_DV)
         for hd in range(ML_HEADS)], axis=1).astype(jnp.bfloat16)
    b0, b1 = ML_W, ML_W + MLA_W
    y = (jnp.dot(ya, w_ref[0:b0, :], preferred_element_type=jnp.float32)
         + jnp.dot(yb_ref[...], w_ref[b0:b1, :], preferred_element_type=jnp.float32)
         + jnp.dot(yc_ref[...], w_ref[b1:, :], preferred_element_type=jnp.float32))
    x_new = x_ref[...] + y
    o_ref[...] = x_new
    h_ref[...] = _rms(x_new, gff_ref[...], x_new.shape[-1]).astype(h_ref.dtype)


def _outproj(x2, hf, hb, mlo, onorm, yb, yc, w, gff, *, tm):
    t, d = x2.shape
    tok = lambda n: pl.BlockSpec((tm, n), lambda i: (i, 0))
    return pl.pallas_call(
        _outproj_kernel, grid=(t // tm,),
        in_specs=[tok(d), tok(ML_W), tok(ML_W), tok(ML_W), _const_spec(onorm.shape),
                  tok(MLA_W), tok(DIL_W), _const_spec(w.shape), _const_spec(gff.shape)],
        out_specs=(tok(d), tok(d)),
        out_shape=(jax.ShapeDtypeStruct((t, d), jnp.float32), jax.ShapeDtypeStruct((t, d), jnp.bfloat16)),
        compiler_params=pltpu.CompilerParams(dimension_semantics=("arbitrary",),
                                             vmem_limit_bytes=VMEM_LIMIT_BYTES),
        name="outproj",
    )(x2, hf, hb, mlo, onorm, yb, yc, w, gff)


def _ffn_kernel(x_ref, h_ref, w1_ref, w2_ref, o_ref):
    @pl.when(pl.program_id(1) == 0)
    def _():
        o_ref[...] = x_ref[...]

    u =jnp.maximum(jnp.dot(h_ref[...], w1_ref[...], preferred_element_type=jnp.float32), 0.0)
    u = (u * u).astype(jnp.bfloat16)
    o_ref[...] += jnp.dot(u, w2_ref[...], preferred_element_type=jnp.float32)


def _ffn(x2, h2, w1, w2, *, tm, tf):
    t, d = x2.shape
    dff = w1.shape[1]
    row = pl.BlockSpec((tm, d), lambda i, j: (i, 0))
    return pl.pallas_call(
        _ffn_kernel, grid=(t // tm, dff // tf),
        in_specs=[row, row,
                  pl.BlockSpec((d, tf), lambda i, j: (0, j)),
                  pl.BlockSpec((tf, d), lambda i, j: (j, 0))],
        out_specs=row,
        out_shape=jax.ShapeDtypeStruct((t, d), jnp.float32),
        compiler_params=pltpu.CompilerParams(dimension_semantics=("arbitrary", "arbitrary"),
                                             vmem_limit_bytes=VMEM_LIMIT_BYTES),
        name="ffn",
    )(x2, h2, w1, w2)


def _rope_tables(seq, rot_dim, fill_cos):
    pos = jnp.arange(seq, dtype=jnp.float32)
    inv_freq = ROPE_THETA ** (-jnp.arange(0, rot_dim, 2, dtype=jnp.float32) / rot_dim)
    ang = pos[:, None] * inv_freq[None, :]
    cos, sin = jnp.cos(ang), jnp.sin(ang)
    pad = LANES - rot_dim
    cos_f = jnp.concatenate([cos, cos, jnp.full((seq, pad), fill_cos, jnp.float32)], axis=1)
    sin_f = jnp.concatenate([-sin, sin, jnp.zeros((seq, pad), jnp.float32)], axis=1)
    return cos_f, sin_f


def _pad_cols(a, n):
    return jnp.pad(a, ((0, 0), (0, n - a.shape[1])))


def _pack_w_in(w_in):
    splits = np.cumsum((256, 256, ML_W, ML_W, 4 * ML_HEADS, MLA_Q_RANK, MLA_KV_RANK + MLA_ROPE,
                        DIL_W, DIL_W, DIL_W))[:-1].tolist()
    (wq, wk, wv, wo, wg, wcq, wckv, wdq, wdk, wdv) = jnp.split(w_in, splits, axis=1)
    packed = jnp.concatenate([wq, wk, wv, wo, _pad_cols(wg, LANES), wcq, _pad_cols(wckv, 256),
                              wdq, wdk, wdv], axis=1)
    return packed.astype(jnp.bfloat16)


def _cast_kernel(x_ref, o_ref):
    o_ref[...] = x_ref[...].astype(o_ref.dtype)


def _to_bf16(stacked, layer, block_bytes=8 * 1024 * 1024):
    _, rows, cols = stacked.shape
    br = min(rows, max(8, block_bytes // (4 * cols)))
    assert rows % br == 0
    return pl.pallas_call(
        _cast_kernel, grid=(rows // br,),
        in_specs=[pl.BlockSpec((None, br, cols), lambda i: (layer, i, 0))],
        out_specs=pl.BlockSpec((br, cols), lambda i: (i, 0)),
        out_shape=jax.ShapeDtypeStruct((rows, cols), jnp.bfloat16),
        compiler_params=pltpu.CompilerParams(dimension_semantics=("arbitrary",),
                                             vmem_limit_bytes=VMEM_LIMIT_BYTES),
        name="to_bf16",
    )(stacked)


def _layer(x2, p, stacked, layer, tables, *, batch, seq):
    cosb, sinb, cosc, sinc, dil_bias = tables
    w = _pack_w_in(p['w_in'])
    wqb = p['mla_w_q_b'].reshape(MLA_Q_RANK, MLA_HEADS, MLA_NOPE + MLA_ROPE)
    wqb = jnp.pad(wqb, ((0, 0), (0, 0), (0, MLA_QK_PAD - MLA_NOPE - MLA_ROPE)))
    wqb = wqb.reshape(MLA_Q_RANK, MLA_HEADS * MLA_QK_PAD).astype(jnp.bfloat16)
    wkvb = p['mla_w_kv_b'].astype(jnp.bfloat16)
    gbias = jnp.concatenate([p['ml_i_bias'][0], p['ml_f_bias'][0], p['ml_i_bias'][1], p['ml_f_bias'][1]])
    gbias = _pad_cols(gbias[None, :], LANES)
    qh, kh = p['mla_q_head_norm'], p['mla_k_head_norm']
    hg = jnp.stack([qh[:MLA_NOPE], jnp.pad(qh[MLA_NOPE:], (0, LANES - MLA_ROPE)),
                    kh[:MLA_NOPE], jnp.pad(kh[MLA_NOPE:], (0, LANES - MLA_ROPE)),
                    p['dil_q_norm'], p['dil_k_norm'],
                    jnp.zeros((LANES,), jnp.float32), jnp.zeros((LANES,), jnp.float32)])

    (mlq, mlkt, mlv, mlo, gates, gates_t, qt, k, vt, dqt, dk, dvt) = _inproj(
        x2, p['norm_mix'][None, :], w, wqb, wkvb, gbias, p['mla_q_norm'][None, :],
        p['mla_kv_norm'][None, :], hg, cosb, sinb, cosc, sinc, seq=seq, tm=MLA_TILE)

    h_fwd, h_bwd = _mlstm(mlq, mlkt, mlv, gates, gates_t, batch=batch, seq=seq, chunk=ML_CHUNK)
    yb = _mla_attention(qt, k, vt, batch=batch, seq=seq, unroll=min(MLA_UNROLL, seq // MLA_TILE))
    yc = _dil_attention(dqt, dk, dvt, dil_bias, batch=batch, seq=seq)

    x2, h2 = _outproj(x2, h_fwd, h_bwd, mlo, p['ml_out_norm'], yb, yc, _to_bf16(stacked['w_out'], layer),
                      p['norm_ff'][None, :], tm=ROW_TILE)
    return _ffn(x2, h2, _to_bf16(stacked['w_ff1'], layer), _to_bf16(stacked['w_ff2'], layer),
                tm=ROW_TILE, tf=FFN_CHUNK)


def kernel(x, norm_mix, w_in, ml_i_bias, ml_f_bias, ml_out_norm, mla_q_norm, mla_w_q_b, mla_kv_norm,
           mla_w_kv_b, mla_q_head_norm, mla_k_head_norm, dil_q_norm, dil_k_norm, w_out, norm_ff,
           w_ff1, w_ff2):
    batch, seq, d = x.shape
    for tile in (DIL_SUB * DIL_TILE, MLA_QSUB * MLA_TILE, ML_CHUNK, ROW_TILE):
        assert seq % tile == 0, (seq, tile)
    params = dict(norm_mix=norm_mix, w_in=w_in, ml_i_bias=ml_i_bias, ml_f_bias=ml_f_bias,
                  ml_out_norm=ml_out_norm, mla_q_norm=mla_q_norm, mla_w_q_b=mla_w_q_b,
                  mla_kv_norm=mla_kv_norm, mla_w_kv_b=mla_w_kv_b, mla_q_head_norm=mla_q_head_norm,
                  mla_k_head_norm=mla_k_head_norm, dil_q_norm=dil_q_norm, dil_k_norm=dil_k_norm,
                  norm_ff=norm_ff)
    stacked = dict(w_out=w_out, w_ff1=w_ff1, w_ff2=w_ff2)
    cosb, sinb = _rope_tables(seq, MLA_ROPE, 0.0)
    cosc, sinc = _rope_tables(seq, DIL_ROT, 1.0)
    tables = (cosb, sinb, cosc, sinc, jnp.asarray(_dil_bias_table()))
    x2 = x.reshape(batch * seq, d)
    for layer in range(norm_mix.shape[0]):
        x2 = _layer(x2, {name: val[layer] for name, val in params.items()}, stacked, layer, tables,
                    batch=batch, seq=seq)
    return x2.reshape(batch, seq, d)
```

```python
import functools
import math

import numpy as np
import jax
import jax.numpy as jnp
from jax import lax
from jax.experimental import pallas as pl
from jax.experimental.pallas import tpu as pltpu

EPS = 1e-6
NEG_INF = -1e30
ROPE_THETA = 500000.0

ML_HEADS = 4
ML_DK = 64
ML_DV = 128
ML_W = ML_HEADS * ML_DV

MLA_HEADS = 6
MLA_Q_RANK = 384
MLA_KV_RANK = 128
MLA_NOPE = 128
MLA_ROPE = 64
MLA_V = 128
MLA_W = MLA_HEADS * MLA_V
MLA_QK_PAD = 256
MLA_VT_ROWS = 144
MLA_TILE = 512

DIL_HEADS = 6
DIL_DH = 128
DIL_ROT = DIL_DH // 4
DIL_PAIRS = ((128, 1), (512, 4), (2048, 16))
DIL_W = DIL_HEADS * DIL_DH

LANES = 128
VMEM_LIMIT_BYTES = 60 * 1024 * 1024

ML_CHUNK = 256
MLA_UNROLL = 16
ROW_TILE = 512
FFN_CHUNK = 2048

_C_MLQ = 0
_C_MLK = 256
_C_MLV = 512
_C_MLO = 1024
_C_GATE = 1536
_C_CQ = 1664
_C_CKV = 2048
_C_DQ = 2304
_C_DK = 3072
_C_DV = 3840
_C_END = 4608


def _rms(x, gain, n):
    ms = jnp.sum(x * x, axis=-1, keepdims=True) * (1.0 / n)
    return x * lax.rsqrt(ms + EPS) * gain


def _rope_tile(x, cos_f, sin_f, half):
    lane = lax.broadcasted_iota(jnp.int32, x.shape, 1)
    sw = jnp.where(lane < half, pltpu.roll(x, LANES - half, 1), pltpu.roll(x, half, 1))
    return x * cos_f + sw * sin_f


def _log_sigmoid(x):
    return jnp.minimum(x, 0.0) - jnp.log1p(jnp.exp(-jnp.abs(x)))


def _inproj_kernel(x_ref, gmix_ref, w_ref, wqb_ref, wkvb_ref, gbias_ref, qlat_ref, kvlat_ref,
                   hg_ref, cosb_ref, sinb_ref, cosc_ref, sinc_ref,
                   mlq_ref, mlkt_ref, mlv_ref, mlo_ref, gate_ref, gatet_ref, qt_ref, k_ref, vt_ref,
                   dqt_ref, dk_ref, dvt_ref, *, mla_scale, dil_scale):
    x = x_ref[...]
    h = _rms(x, gmix_ref[...], x.shape[-1]).astype(jnp.bfloat16)

    def proj(c0, c1):
        return jnp.dot(h, w_ref[:, c0:c1], preferred_element_type=jnp.float32)

    cos_b = cosb_ref[...]
    sin_b = sinb_ref[...]
    cos_c = cosc_ref[...]
    sin_c = sinc_ref[...]
    qg_n = hg_ref[0:1, :]
    qg_r = hg_ref[1:2, :]
    kg_n = hg_ref[2:3, :]
    kg_r = hg_ref[3:4, :]
    dqg = hg_ref[4:5, :]
    dkg = hg_ref[5:6, :]

    cq = _rms(proj(_C_CQ, _C_CKV), qlat_ref[...], MLA_Q_RANK).astype(jnp.bfloat16)
    zc = proj(_C_CKV, _C_DQ)
    ckv = _rms(zc[:, :MLA_KV_RANK], kvlat_ref[...], MLA_KV_RANK).astype(jnp.bfloat16)
    zdq = proj(_C_DQ, _C_DK)
    zdk = proj(_C_DK, _C_DV)
    zq = jnp.dot(cq, wqb_ref[...], preferred_element_type=jnp.float32)
    zkv = jnp.dot(ckv, wkvb_ref[...], preferred_element_type=jnp.float32)

    k_rope = _rope_tile(_rms(zc[:, MLA_KV_RANK:], kg_r, MLA_ROPE), cos_b, sin_b, MLA_ROPE // 2)
    k_rope = k_rope.astype(jnp.bfloat16)
    pad_rows = MLA_VT_ROWS - MLA_V

    def ones_row_tile(n):
        first = lax.broadcasted_iota(jnp.int32, (pad_rows, n), 0) == 0
        return jnp.where(first, 1.0, 0.0).astype(jnp.bfloat16)

    ones_row = ones_row_tile(x.shape[0])
    ones_row_dil = ones_row_tile(DIL_TILE)
    for hd in range(MLA_HEADS):
        c = hd * MLA_QK_PAD
        q_nope = _rms(zq[:, c:c + MLA_NOPE], qg_n, MLA_NOPE)
        q_rope = _rope_tile(_rms(zq[:, c + MLA_NOPE:c + MLA_QK_PAD], qg_r, MLA_ROPE),
                            cos_b, sin_b, MLA_ROPE // 2)
        qt_ref[hd, 0, 0:MLA_NOPE, :] = (q_nope * mla_scale).T.astype(jnp.bfloat16)
        qt_ref[hd, 0, MLA_NOPE:MLA_QK_PAD, :] = (q_rope * mla_scale).T.astype(jnp.bfloat16)
        k_nope = _rms(zkv[:, c:c + MLA_NOPE], kg_n, MLA_NOPE)
        k_ref[hd, :, 0:MLA_NOPE] = k_nope.astype(jnp.bfloat16)
        k_ref[hd, :, MLA_NOPE:MLA_QK_PAD] = k_rope
        vt_ref[hd, 0, 0:MLA_V, :] = zkv[:, c + MLA_NOPE:c + MLA_QK_PAD].T.astype(jnp.bfloat16)
        vt_ref[hd, 0, MLA_V:MLA_VT_ROWS, :] = ones_row

    mlkt_ref[...] = proj(_C_MLK, _C_MLV).T.astype(jnp.bfloat16)
    mlo_ref[...] = jax.nn.sigmoid(proj(_C_MLO, _C_GATE)).astype(jnp.bfloat16)
    g = proj(_C_GATE, _C_CQ) + gbias_ref[...]
    lane = lax.broadcasted_iota(jnp.int32, g.shape, 1)
    is_forget = (lane % 8) >= 4
    g = jnp.where(is_forget, _log_sigmoid(g), g)
    gate_ref[...] = g
    gatet_ref[...] = g.T

    for hd in range(DIL_HEADS):
        c = hd * DIL_DH
        qh = _rope_tile(_rms(zdq[:, c:c + DIL_DH], dqg, DIL_DH), cos_c, sin_c, DIL_ROT // 2)
        kh = _rope_tile(_rms(zdk[:, c:c + DIL_DH], dkg, DIL_DH), cos_c, sin_c, DIL_ROT // 2)
        qh = qh * dil_scale
        for sb in range(x.shape[0] // DIL_TILE):
            dqt_ref[hd, sb] = qh[sb * DIL_TILE:(sb + 1) * DIL_TILE].T.astype(jnp.bfloat16)
        dk_ref[:, c:c + DIL_DH] = kh.astype(jnp.bfloat16)
    zdv = proj(_C_DV, _C_END)
    for hd in range(DIL_HEADS):
        c = hd * DIL_DH
        for sb in range(x.shape[0] // DIL_TILE):
            rows = slice(sb * DIL_TILE, (sb + 1) * DIL_TILE)
            dvt_ref[hd, sb, 0:DIL_DH, :] = zdv[rows, c:c + DIL_DH].T.astype(jnp.bfloat16)
            dvt_ref[hd, sb, DIL_DH:MLA_VT_ROWS, :] = ones_row_dil

    mlq_ref[...] = (proj(_C_MLQ, _C_MLK) * (ML_DK ** -0.5)).astype(jnp.bfloat16)
    mlv_ref[...] = proj(_C_MLV, _C_MLO).astype(jnp.bfloat16)


def _const_spec(shape):
    n = len(shape)
    return pl.BlockSpec(shape, lambda *_: (0,) * n, pipeline_mode=pl.Buffered(1))


def _inproj(x2, gmix, w, wqb, wkvb, gbias, qlat, kvlat, hg, cosb, sinb, cosc, sinc, *, seq, tm):
    t, d = x2.shape
    nt = t // tm
    ns = seq // tm
    tok = lambda n: pl.BlockSpec((tm, n), lambda i: (i, 0))
    pos = pl.BlockSpec((tm, LANES), lambda i: (i % ns, 0))
    headed = lambda n: pl.BlockSpec((MLA_HEADS, tm, n), lambda i: (0, i, 0))
    headed_t = lambda n: pl.BlockSpec((MLA_HEADS, 1, n, tm), lambda i: (0, i, 0, 0))
    bf = jnp.bfloat16
    out_shape = (
        jax.ShapeDtypeStruct((t, 256), bf), jax.ShapeDtypeStruct((256, t), bf),
        jax.ShapeDtypeStruct((t, ML_W), bf), jax.ShapeDtypeStruct((t, ML_W), bf),
        jax.ShapeDtypeStruct((t, LANES), jnp.float32), jax.ShapeDtypeStruct((LANES, t), jnp.float32),
        jax.ShapeDtypeStruct((MLA_HEADS, nt, MLA_QK_PAD, tm), bf),
        jax.ShapeDtypeStruct((MLA_HEADS, t, MLA_QK_PAD), bf),
        jax.ShapeDtypeStruct((MLA_HEADS, nt, MLA_VT_ROWS, tm), bf),
        jax.ShapeDtypeStruct((DIL_HEADS, t // DIL_TILE, DIL_DH, DIL_TILE), bf),
        jax.ShapeDtypeStruct((t, DIL_W), bf),
        jax.ShapeDtypeStruct((DIL_HEADS, t // DIL_TILE, MLA_VT_ROWS, DIL_TILE), bf),
    )
    dil_t = lambda n: pl.BlockSpec((DIL_HEADS, tm // DIL_TILE, n, DIL_TILE), lambda i: (0, i, 0, 0))
    tok_t = lambda n: pl.BlockSpec((n, tm), lambda i: (0, i))
    out_specs = (tok(256), tok_t(256), tok(ML_W), tok(ML_W), tok(LANES), tok_t(LANES),
                 headed_t(MLA_QK_PAD), headed(MLA_QK_PAD), headed_t(MLA_VT_ROWS),
                 dil_t(DIL_DH), tok(DIL_W), dil_t(MLA_VT_ROWS))
    in_specs = [tok(d), _const_spec(gmix.shape), _const_spec(w.shape), _const_spec(wqb.shape),
                _const_spec(wkvb.shape), _const_spec(gbias.shape), _const_spec(qlat.shape),
                _const_spec(kvlat.shape), _const_spec(hg.shape), pos, pos, pos, pos]
    kern = functools.partial(_inproj_kernel,
                             mla_scale=(MLA_NOPE + MLA_ROPE) ** -0.5 * math.log2(math.e),
                             dil_scale=DIL_DH ** -0.5 * math.log2(math.e))
    return pl.pallas_call(
        kern, grid=(nt,), in_specs=in_specs, out_specs=out_specs, out_shape=out_shape,
        compiler_params=pltpu.CompilerParams(dimension_semantics=("arbitrary",),
                                             vmem_limit_bytes=VMEM_LIMIT_BYTES),
        name="inproj",
    )(x2, gmix, w, wqb, wkvb, gbias, qlat, kvlat, hg, cosb, sinb, cosc, sinc)


def _mlstm_kernel(*refs, chunk, nb):
    per = 3 + 2 * nb
    out_f, out_b, ct_ref, m_ref = refs[2 * per:]

    @pl.when(pl.program_id(0) == 0)
    def _():
        ct_ref[...] = jnp.zeros_like(ct_ref)
        m_ref[...] = jnp.zeros_like(m_ref)

    _mlstm_direction(refs[0:per], out_f, ct_ref, m_ref, reverse=False, slot0=0, chunk=chunk, nb=nb)
    _mlstm_direction(refs[per:2 * per], out_b, ct_ref, m_ref, reverse=True, slot0=nb * ML_HEADS,
                     chunk=chunk, nb=nb)


def _mlstm_direction(refs, out_ref, ct_ref, m_ref, *, reverse, slot0, chunk, nb):
    q_ref, v_ref, gate_ref = refs[0:3]
    kt_refs = refs[3:3 + nb]
    gatet_refs = refs[3 + nb:3 + 2 * nb]
    L = chunk

    row = lax.broadcasted_iota(jnp.int32, (L, L), 0)
    col = lax.broadcasted_iota(jnp.int32, (L, L), 1)
    causal = (col >= row) if reverse else (col <= row)
    vis = causal.astype(jnp.bfloat16)
    vis_t = ((row >= col) if reverse else (row <= col)).astype(jnp.bfloat16)

    def split3(x):
        hi = x.astype(jnp.bfloat16)
        rest_ = x - hi.astype(jnp.float32)
        mid = rest_.astype(jnp.bfloat16)
        return hi, mid, (rest_ - mid.astype(jnp.float32)).astype(jnp.bfloat16)

    def rep(col_):
        return jnp.broadcast_to(col_, (L, LANES))

    def wide(xb, n):
        return xb if n == LANES else jnp.concatenate([xb] * (n // LANES), axis=1)

    lane = lax.broadcasted_iota(jnp.int32, (L, LANES), 1)
    ones_col = jnp.where(lane == 0, 1.0, 0.0).astype(jnp.bfloat16)
    zero_half = jnp.zeros((ML_DK, L), jnp.bfloat16)

    d0 = 8 if reverse else 0
    gates, slab, cum_t, cum = [], [], [], []
    for bi in range(nb):
        gates.append(gate_ref[bi])
        slab.append(gatet_refs[bi][d0:d0 + 8, :])
        cum_t.append(sum(jnp.dot(vis, part, preferred_element_type=jnp.float32)
                         for part in split3(gates[bi])))
        cum.append(sum(jnp.dot(part, vis_t, preferred_element_type=jnp.float32)
                       for part in split3(slab[bi])))

    for hd in range(ML_HEADS):
        for bi in range(nb):
            st = slot0 + bi * ML_HEADS + hd
            i_b = rep(gates[bi][:, d0 + hd:d0 + hd + 1])
            i_row = slab[bi][hd:hd + 1, :]
            cum_b = rep(cum_t[bi][:, d0 + 4 + hd:d0 + 5 + hd])
            cum_row = cum[bi][4 + hd:5 + hd, :]
            tot = cum_row[:, 0:1] if reverse else cum_row[:, L - 1:L]
            m_prev = m_ref[st:st + 1, 0:1]

            pair, half = hd // 2, hd % 2
            in_head = (lane >= half * ML_DK) & (lane < (half + 1) * ML_DK)
            qp = q_ref[bi, :, pair * LANES:(pair + 1) * LANES]
            qm = jnp.where(in_head, qp, jnp.zeros_like(qp))
            kt_ref = kt_refs[bi]
            kt_pair = kt_ref[pair * LANES:(pair + 1) * LANES, :]
            kt_h = kt_ref[pair * LANES + half * ML_DK:pair * LANES + (half + 1) * ML_DK, :]
            kmt = jnp.concatenate([kt_h, zero_half] if half == 0 else [zero_half, kt_h], axis=0)
            v_h = v_ref[bi, :, hd * ML_DV:(hd + 1) * ML_DV]
            v_aug = jnp.concatenate([v_h, ones_col], axis=1)

            d_mat = jnp.where(causal, wide(cum_b, L) + (i_row - cum_row), NEG_INF)
            d_inter = cum_b + m_prev
            m_t = jnp.maximum(d_inter, rep(jnp.max(d_mat, axis=1, keepdims=True)))
            w_intra = jnp.exp(d_mat - wide(m_t, L))
            w_inter = jnp.exp(d_inter - m_t)
            s_raw = jnp.dot(qm, kt_pair, preferred_element_type=jnp.float32)
            sw = (s_raw * w_intra).astype(jnp.bfloat16)
            ct = ct_ref[st]
            r = (jnp.dot(sw, v_aug, preferred_element_type=jnp.float32)
                 + wide(w_inter, 2 * LANES) * jnp.dot(qm, ct.astype(jnp.bfloat16),
                                                      preferred_element_type=jnp.float32))
            num = r[:, :ML_DV]
            den = rep(r[:, ML_DV:ML_DV + 1])
            h_dir = num / jnp.maximum(jnp.abs(den), jnp.exp(-m_t))

            d_state = tot - cum_b + i_b
            m_new = jnp.maximum(tot + m_prev, jnp.max(d_state, axis=0, keepdims=True)[:, 0:1])
            w_s = jnp.exp(d_state - m_new)
            w_c = jnp.exp(tot + m_prev - m_new)
            vw = (v_aug.astype(jnp.float32) * wide(w_s, 2 * LANES)).astype(jnp.bfloat16)
            ct_ref[st] = w_c * ct + jnp.dot(kmt, vw, preferred_element_type=jnp.float32)
            m_ref[st:st + 1, :] = jnp.broadcast_to(m_new, (1, LANES))

            out_ref[bi, :, hd * ML_DV:(hd + 1) * ML_DV] = h_dir.astype(out_ref.dtype)


def _mlstm(mlq, mlkt, mlv, gates, gates_t, *, batch, seq, chunk):
    nc = seq // chunk
    view = lambda a: a.reshape(batch, seq, a.shape[-1])
    in_specs, args, out_specs = [], [], []
    for blk in (lambda c: c, lambda c: nc - 1 - c):
        tok = lambda n, blk=blk: pl.BlockSpec((batch, chunk, n), lambda c: (0, blk(c), 0))
        tok_t = lambda n, bi, blk=blk: pl.BlockSpec((n, chunk), lambda c: (0, bi * nc + blk(c)))
        in_specs += ([tok(256), tok(ML_W), tok(LANES)]
                     + [tok_t(256, bi) for bi in range(batch)]
                     + [tok_t(LANES, bi) for bi in range(batch)])
        args += [view(mlq), view(mlv), view(gates)] + [mlkt] * batch + [gates_t] * batch
        out_specs.append(tok(ML_W))
    states = 2 * batch * ML_HEADS
    out_shape = jax.ShapeDtypeStruct((batch, seq, ML_W), jnp.bfloat16)
    h_fwd, h_bwd = pl.pallas_call(
        functools.partial(_mlstm_kernel, chunk=chunk, nb=batch),
        grid=(nc,), in_specs=in_specs, out_specs=tuple(out_specs), out_shape=(out_shape, out_shape),
        scratch_shapes=[pltpu.VMEM((states, LANES, 2 * LANES), jnp.float32),
                        pltpu.VMEM((-(-states // 8) * 8, LANES), jnp.float32)],
        compiler_params=pltpu.CompilerParams(dimension_semantics=("arbitrary",),
                                             vmem_limit_bytes=VMEM_LIMIT_BYTES),
        name="mlstm",
    )(*args)
    return h_fwd.reshape(batch * seq, ML_W), h_bwd.reshape(batch * seq, ML_W)


MLA_QSUB = 2


def _mla_kernel(qt_ref, k_ref, vt_ref, o_ref, s_scr, acc_scr, *, unroll):
    tq = qt_ref.shape[2]
    tk = vt_ref.shape[2]
    nk = vt_ref.shape[0]

    def attend(sub):
        qt = qt_ref[sub]

        def scores(slot, j):
            start = pl.multiple_of(j * tk, tk)
            s = jnp.dot(k_ref[pl.ds(start, tk), :], qt, preferred_element_type=jnp.float32)
            s_scr[sub, slot] = s
            return jnp.max(s, axis=0, keepdims=True)

        def consume(slot, j, m_prev, m_tile):
            m_new = jnp.maximum(m_prev, m_tile)
            alpha = jnp.exp2(m_prev - m_new)
            p = jnp.exp2(s_scr[sub, slot] - m_new).astype(jnp.bfloat16)
            acc_scr[sub] = alpha * acc_scr[sub] + jnp.dot(vt_ref[j], p,
                                                          preferred_element_type=jnp.float32)
            return m_new

        m_first = scores(0, 0)
        acc_scr[sub] = jnp.zeros(acc_scr.shape[1:], acc_scr.dtype)

        def body(jj, carry):
            m, m_tile = carry
            j = unroll * jj
            for u in range(unroll):
                m_next = m_tile
                if not (unroll == nk and u == unroll - 1):
                    m_next = scores((u + 1) % 2, jnp.minimum(j + u + 1, nk - 1))
                m = consume(u % 2, j + u, m, m_tile)
                m_tile = m_next
            return m, m_tile

        lax.fori_loop(0, nk // unroll, body, (jnp.full((1, tq), NEG_INF, jnp.float32), m_first))
        acc = acc_scr[sub]
        o_ref[sub * tq:(sub + 1) * tq, :] = (acc[:MLA_V] / acc[MLA_V:MLA_V + 1]).T.astype(o_ref.dtype)

    for sub in range(MLA_QSUB):
        attend(sub)


def _mla_attention(qt, k, vt, *, batch, seq, unroll):
    tile = qt.shape[-1]
    t = k.shape[1]
    nq = seq // tile
    nstep = nq // MLA_QSUB
    assert unroll % 2 == 0 and nq % unroll == 0 and nq % MLA_QSUB == 0
    return pl.pallas_call(
        functools.partial(_mla_kernel, unroll=unroll),
        grid=(batch, MLA_HEADS, nstep),
        in_specs=[pl.BlockSpec((None, MLA_QSUB, MLA_QK_PAD, tile), lambda b, h, i: (h, b * nstep + i, 0, 0)),
                  pl.BlockSpec((None, seq, MLA_QK_PAD), lambda b, h, i: (h, b, 0)),
                  pl.BlockSpec((None, nq, MLA_VT_ROWS, tile), lambda b, h, i: (h, b, 0, 0))],
        out_specs=pl.BlockSpec((MLA_QSUB * tile, MLA_V), lambda b, h, i: (b * nstep + i, h)),
        out_shape=jax.ShapeDtypeStruct((t, MLA_W), jnp.bfloat16),
        scratch_shapes=[pltpu.VMEM((MLA_QSUB, 2, tile, tile), jnp.float32),
                        pltpu.VMEM((MLA_QSUB, MLA_VT_ROWS, tile), jnp.float32)],
        compiler_params=pltpu.CompilerParams(
            dimension_semantics=("arbitrary", "arbitrary", "arbitrary"),
            vmem_limit_bytes=VMEM_LIMIT_BYTES),
        name="mla_attn",
    )(qt, k, vt)


DIL_TILE = 256
DIL_REACH = max(w // 2 for w, _ in DIL_PAIRS)
DIL_NOFF = DIL_REACH // DIL_TILE
DIL_NWIN = 2 * DIL_NOFF + 1
DIL_SUB = 8


def _dil_bias_table():
    r = np.arange(DIL_TILE)[:, None]
    c = np.arange(DIL_TILE)[None, :]
    tiles = []
    for o in range(-2 * DIL_NOFF, 2 * DIL_NOFF + 1):
        delta = o * DIL_TILE + r - c
        mult = np.zeros_like(delta)
        for window, dil in DIL_PAIRS:
            mult += ((delta % dil) == 0) & (np.abs(delta) <= window // 2)
        tiles.append(np.where(mult > 0, np.log2(np.maximum(mult, 1)), NEG_INF))
    return np.stack(tiles).astype(np.float32)


def _dil_kernel(qt_ref, k_ref, vt_ref, bias_ref, o_ref, s_scr, *, nq, nwin):
    i = pl.program_id(2)
    subs = range(DIL_SUB)
    qi = [i * DIL_SUB + sub for sub in subs]
    w0 = [jnp.clip(t - DIL_NOFF, 0, nq - nwin) for t in qi]

    def score_pass(sub):
        m = None
        for idx in range(nwin):
            start = pl.multiple_of((w0[sub] + idx) * DIL_TILE, DIL_TILE)
            bias = bias_ref[w0[sub] + idx - qi[sub] + 2 * DIL_NOFF]
            s = jnp.dot(k_ref[pl.ds(start, DIL_TILE), :], qt_ref[sub],
                        preferred_element_type=jnp.float32) + bias
            s_scr[sub, idx * DIL_TILE:(idx + 1) * DIL_TILE, :] = s
            m_cur = jnp.max(s, axis=0, keepdims=True)
            m = m_cur if idx == 0 else jnp.maximum(m, m_cur)
        return m

    def value_pass(sub, m):
        acc = None
        for idx in range(nwin):
            p = jnp.exp2(s_scr[sub, idx * DIL_TILE:(idx + 1) * DIL_TILE, :] - m)
            part = jnp.dot(vt_ref[w0[sub] + idx], p.astype(jnp.bfloat16),
                           preferred_element_type=jnp.float32)
            acc = part if acc is None else acc + part
        out_t = acc[:DIL_DH] / acc[DIL_DH:DIL_DH + 1]
        o_ref[sub * DIL_TILE:(sub + 1) * DIL_TILE, :] = out_t.T.astype(o_ref.dtype)

    m_prev = score_pass(0)
    for sub in range(1, DIL_SUB):
        m_next = score_pass(sub)
        value_pass(sub - 1, m_prev)
        m_prev = m_next
    value_pass(DIL_SUB - 1, m_prev)


def _dil_attention(dqt, dk, dvt, bias, *, batch, seq):
    t = dk.shape[0]
    nq = seq // DIL_TILE
    nwin = min(DIL_NWIN, nq)
    tq = DIL_SUB * DIL_TILE
    nstep = seq // tq
    return pl.pallas_call(
        functools.partial(_dil_kernel, nq=nq, nwin=nwin),
        grid=(batch, DIL_HEADS, nstep),
        in_specs=[pl.BlockSpec((None, DIL_SUB, DIL_DH, DIL_TILE), lambda b, h, i: (h, b * nstep + i, 0, 0)),
                  pl.BlockSpec((seq, DIL_DH), lambda b, h, i: (b, h)),
                  pl.BlockSpec((None, nq, MLA_VT_ROWS, DIL_TILE), lambda b, h, i: (h, b, 0, 0)),
                  _const_spec(bias.shape)],
        out_specs=pl.BlockSpec((tq, DIL_DH), lambda b, h, i: (b * nstep + i, h)),
        out_shape=jax.ShapeDtypeStruct((t, DIL_W), jnp.bfloat16),
        scratch_shapes=[pltpu.VMEM((DIL_SUB, nwin * DIL_TILE, DIL_TILE), jnp.float32)],
        compiler_params=pltpu.CompilerParams(
            dimension_semantics=("arbitrary", "arbitrary", "arbitrary"),
            vmem_limit_bytes=VMEM_LIMIT_BYTES),
        name="dil_attn",
    )(dqt, dk, dvt, bias)


def _outproj_kernel(x_ref, hf_ref, hb_ref, mlo_ref, onorm_ref, yb_ref, yc_ref, w_ref, gff_ref,
                    o_ref, h_ref):
    h_sum = hf_ref[...].astype(jnp.float32) + hb_ref[...].astype(jnp.float32)
    gate = mlo_ref[...].astype(jnp.float32)
    ya = jnp.concatenate(
        [gate[:, hd * ML_DV:(hd + 1) * ML_DV]
         * _rms(h_sum[:, hd * ML_DV:(hd + 1) * ML_DV], onorm_ref[hd:hd + 1, :], ML_DV)
         for hd in range(ML_HEADS)], axis=1).astype(jnp.bfloat16)
    b0, b1 = ML_W, ML_W + MLA_W
    y = (jnp.dot(ya, w_ref[0:b0, :], preferred_element_type=jnp.float32)
         + jnp.dot(yb_ref[...], w_ref[b0:b1, :], preferred_element_type=jnp.float32)
         + jnp.dot(yc_ref[...], w_ref[b1:, :], preferred_element_type=jnp.float32))
    x_new = x_ref[...] + y
    o_ref[...] = x_new
    h_ref[...] = _rms(x_new, gff_ref[...], x_new.shape[-1]).astype(h_ref.dtype)


def _outproj(x2, hf, hb, mlo, onorm, yb, yc, w, gff, *, tm):
    t, d = x2.shape
    tok = lambda n: pl.BlockSpec((tm, n), lambda i: (i, 0))
    return pl.pallas_call(
        _outproj_kernel, grid=(t // tm,),
        in_specs=[tok(d), tok(ML_W), tok(ML_W), tok(ML_W), _const_spec(onorm.shape),
                  tok(MLA_W), tok(DIL_W), _const_spec(w.shape), _const_spec(gff.shape)],
        out_specs=(tok(d), tok(d)),
        out_shape=(jax.ShapeDtypeStruct((t, d), jnp.float32), jax.ShapeDtypeStruct((t, d), jnp.bfloat16)),
        compiler_params=pltpu.CompilerParams(dimension_semantics=("arbitrary",),
                                             vmem_limit_bytes=VMEM_LIMIT_BYTES),
        name="outproj",
    )(x2, hf, hb, mlo, onorm, yb, yc, w, gff)


def _ffn_kernel(x_ref, h_ref, w1_ref, w2_ref, o_ref):
    @pl.when(pl.program_id(1) == 0)
    def _():
        o_ref[...] = x_ref[...]

    u =jnp.maximum(jnp.dot(h_ref[...], w1_ref[...], preferred_element_type=jnp.float32), 0.0)
    u = (u * u).astype(jnp.bfloat16)
    o_ref[...] += jnp.dot(u, w2_ref[...], preferred_element_type=jnp.float32)


def _ffn(x2, h2, w1, w2, *, tm, tf):
    t, d = x2.shape
    dff = w1.shape[1]
    row = pl.BlockSpec((tm, d), lambda i, j: (i, 0))
    return pl.pallas_call(
        _ffn_kernel, grid=(t // tm, dff // tf),
        in_specs=[row, row,
                  pl.BlockSpec((d, tf), lambda i, j: (0, j)),
                  pl.BlockSpec((tf, d), lambda i, j: (j, 0))],
        out_specs=row,
        out_shape=jax.ShapeDtypeStruct((t, d), jnp.float32),
        compiler_params=pltpu.CompilerParams(dimension_semantics=("arbitrary", "arbitrary"),
                                             vmem_limit_bytes=VMEM_LIMIT_BYTES),
        name="ffn",
    )(x2, h2, w1, w2)


def _rope_tables(seq, rot_dim, fill_cos):
    pos = jnp.arange(seq, dtype=jnp.float32)
    inv_freq = ROPE_THETA ** (-jnp.arange(0, rot_dim, 2, dtype=jnp.float32) / rot_dim)
    ang = pos[:, None] * inv_freq[None, :]
    cos, sin = jnp.cos(ang), jnp.sin(ang)
    pad = LANES - rot_dim
    cos_f = jnp.concatenate([cos, cos, jnp.full((seq, pad), fill_cos, jnp.float32)], axis=1)
    sin_f = jnp.concatenate([-sin, sin, jnp.zeros((seq, pad), jnp.float32)], axis=1)
    return cos_f, sin_f


def _pad_cols(a, n):
    return jnp.pad(a, ((0, 0), (0, n - a.shape[1])))


def _pack_w_in(w_in):
    splits = np.cumsum((256, 256, ML_W, ML_W, 4 * ML_HEADS, MLA_Q_RANK, MLA_KV_RANK + MLA_ROPE,
                        DIL_W, DIL_W, DIL_W))[:-1].tolist()
    (wq, wk, wv, wo, wg, wcq, wckv, wdq, wdk, wdv) = jnp.split(w_in, splits, axis=1)
    packed = jnp.concatenate([wq, wk, wv, wo, _pad_cols(wg, LANES), wcq, _pad_cols(wckv, 256),
                              wdq, wdk, wdv], axis=1)
    return packed.astype(jnp.bfloat16)


def _cast_kernel(x_ref, o_ref):
    o_ref[...] = x_ref[...].astype(o_ref.dtype)


def _to_bf16(stacked, layer, block_bytes=8 * 1024 * 1024):
    _, rows, cols = stacked.shape
    br = min(rows, max(8, block_bytes // (4 * cols)))
    assert rows % br == 0
    return pl.pallas_call(
        _cast_kernel, grid=(rows // br,),
        in_specs=[pl.BlockSpec((None, br, cols), lambda i: (layer, i, 0))],
        out_specs=pl.BlockSpec((br, cols), lambda i: (i, 0)),
        out_shape=jax.ShapeDtypeStruct((rows, cols), jnp.bfloat16),
        compiler_params=pltpu.CompilerParams(dimension_semantics=("arbitrary",),
                                             vmem_limit_bytes=VMEM_LIMIT_BYTES),
        name="to_bf16",
    )(stacked)


def _layer(x2, p, stacked, layer, tables, *, batch, seq):
    cosb, sinb, cosc, sinc, dil_bias = tables
    w = _pack_w_in(p['w_in'])
    wqb = p['mla_w_q_b'].reshape(MLA_Q_RANK, MLA_HEADS, MLA_NOPE + MLA_ROPE)
    wqb = jnp.pad(wqb, ((0, 0), (0, 0), (0, MLA_QK_PAD - MLA_NOPE - MLA_ROPE)))
    wqb = wqb.reshape(MLA_Q_RANK, MLA_HEADS * MLA_QK_PAD).astype(jnp.bfloat16)
    wkvb = p['mla_w_kv_b'].astype(jnp.bfloat16)
    gbias = jnp.concatenate([p['ml_i_bias'][0], p['ml_f_bias'][0], p['ml_i_bias'][1], p['ml_f_bias'][1]])
    gbias = _pad_cols(gbias[None, :], LANES)
    qh, kh = p['mla_q_head_norm'], p['mla_k_head_norm']
    hg = jnp.stack([qh[:MLA_NOPE], jnp.pad(qh[MLA_NOPE:], (0, LANES - MLA_ROPE)),
                    kh[:MLA_NOPE], jnp.pad(kh[MLA_NOPE:], (0, LANES - MLA_ROPE)),
                    p['dil_q_norm'], p['dil_k_norm'],
                    jnp.zeros((LANES,), jnp.float32), jnp.zeros((LANES,), jnp.float32)])

    (mlq, mlkt, mlv, mlo, gates, gates_t, qt, k, vt, dqt, dk, dvt) = _inproj(
        x2, p['norm_mix'][None, :], w, wqb, wkvb, gbias, p['mla_q_norm'][None, :],
        p['mla_kv_norm'][None, :], hg, cosb, sinb, cosc, sinc, seq=seq, tm=MLA_TILE)

    h_fwd, h_bwd = _mlstm(mlq, mlkt, mlv, gates, gates_t, batch=batch, seq=seq, chunk=ML_CHUNK)
    yb = _mla_attention(qt, k, vt, batch=batch, seq=seq, unroll=min(MLA_UNROLL, seq // MLA_TILE))
    yc = _dil_attention(dqt, dk, dvt, dil_bias, batch=batch, seq=seq)

    x2, h2 = _outproj(x2, h_fwd, h_bwd, mlo, p['ml_out_norm'], yb, yc, _to_bf16(stacked['w_out'], layer),
                      p['norm_ff'][None, :], tm=ROW_TILE)
    return _ffn(x2, h2, _to_bf16(stacked['w_ff1'], layer), _to_bf16(stacked['w_ff2'], layer),
                tm=ROW_TILE, tf=FFN_CHUNK)


def kernel(x, norm_mix, w_in, ml_i_bias, ml_f_bias, ml_out_norm, mla_q_norm, mla_w_q_b, mla_kv_norm,
           mla_w_kv_b, mla_q_head_norm, mla_k_head_norm, dil_q_norm, dil_k_norm, w_out, norm_ff,
           w_ff1, w_ff2):
    batch, seq, d = x.shape
    for tile in (DIL_SUB * DIL_TILE, MLA_QSUB * MLA_TILE, ML_CHUNK, ROW_TILE):
        assert seq % tile == 0, (seq, tile)
    params = dict(norm_mix=norm_mix, w_in=w_in, ml_i_bias=ml_i_bias, ml_f_bias=ml_f_bias,
                  ml_out_norm=ml_out_norm, mla_q_norm=mla_q_norm, mla_w_q_b=mla_w_q_b,
                  mla_kv_norm=mla_kv_norm, mla_w_kv_b=mla_w_kv_b, mla_q_head_norm=mla_q_head_norm,
                  mla_k_head_norm=mla_k_head_norm, dil_q_norm=dil_q_norm, dil_k_norm=dil_k_norm,
                  norm_ff=norm_ff)
    stacked = dict(w_out=w_out, w_ff1=w_ff1, w_ff2=w_ff2)
    cosb, sinb = _rope_tables(seq, MLA_ROPE, 0.0)
    cosc, sinc = _rope_tables(seq, DIL_ROT, 1.0)
    tables = (cosb, sinb, cosc, sinc, jnp.asarray(_dil_bias_table()))
    x2 = x.reshape(batch * seq, d)
    for layer in range(norm_mix.shape[0]):
        x2 = _layer(x2, {name: val[layer] for name, val in params.items()}, stacked, layer, tables,
                    batch=batch, seq=seq)
    return x2.reshape(batch, seq, d)
```
